```python
import math
import jax, jax.numpy as jnp
from jax import lax
import numpy as np

D_MODEL = 1024
BATCH = 8
SEQ = 8192
DEPTH = 1

ROPE_THETA = 10000.0
NORM_EPS = 1e-6
Q_BLOCK = 128

MLA_HEADS = 8
MLA_NOPE_DIM = 64
MLA_ROPE_DIM = 32
MLA_QK_DIM = MLA_NOPE_DIM + MLA_ROPE_DIM
MLA_V_DIM = 64
MLA_Q_LORA = 768
MLA_KV_LORA = 256
MLA_WIDTH = MLA_HEADS * MLA_V_DIM

DIFF_HEADS = 4
DIFF_HEAD_DIM = 64
DIFF_V_DIM = 2 * DIFF_HEAD_DIM
DIFF_WIDTH = DIFF_HEADS * DIFF_V_DIM

N_BRANCHES = 2
IN_WIDTH = MLA_Q_LORA + MLA_KV_LORA + MLA_ROPE_DIM + 3 * DIFF_WIDTH + N_BRANCHES * D_MODEL

N_EXPERTS = 32
TOP_K = 4
D_EXPERT = D_MODEL
SWIGLU_ALPHA = 1.702
SWIGLU_LIMIT = 7.0

kernel_name = 'hybrid_mla_diffattn_gated_moe'


def rms_norm(x, g):
    xf = x.astype(jnp.float32)
    y = xf * lax.rsqrt(jnp.mean(xf * xf, axis=-1, keepdims=True) + NORM_EPS)
    return (y * g.astype(jnp.float32)).astype(x.dtype)


def rope_tables(seq, dim):
    inv_freq = 1.0 / (ROPE_THETA ** (jnp.arange(0, dim, 2, dtype=jnp.float32) / dim))
    ang = jnp.arange(seq, dtype=jnp.float32)[:, None] * inv_freq[None, :]
    return jnp.cos(ang), jnp.sin(ang)


def apply_rope(x, cos, sin):
    xf = x.astype(jnp.float32)
    x1, x2 = jnp.split(xf, 2, axis=-1)
    c = cos[None, :, None, :]
    s = sin[None, :, None, :]
    return jnp.concatenate([x1 * c - x2 * s, x2 * c + x1 * s], axis=-1).astype(x.dtype)


def causal_block_attention(q, k, v, scale):
    b, s, h, dq = q.shape
    dv = v.shape[-1]
    nb = s // Q_BLOCK
    q_blocks = q.reshape(b, nb, Q_BLOCK, h, dq).transpose(1, 0, 2, 3, 4)
    key_pos = jnp.arange(s)

    def one_block(args):
        q_blk, blk = args
        sc = jnp.einsum('bqhd,bkhd->bhqk', q_blk, k, preferred_element_type=jnp.float32) * scale
        q_pos = blk * Q_BLOCK + jnp.arange(Q_BLOCK)
        mask = key_pos[None, :] <= q_pos[:, None]
        sc = jnp.where(mask[None, None], sc, -jnp.inf)
        p = jax.nn.softmax(sc, axis=-1)
        return jnp.einsum('bhqk,bkhd->bqhd', p.astype(v.dtype), v)

    out = lax.map(one_block, (q_blocks, jnp.arange(nb)))
    return out.transpose(1, 0, 2, 3, 4).reshape(b, s, h, dv)


def mla_branch(q_lat, kv_lat, k_pe, q_lat_norm, kv_lat_norm, w_uq, w_ukv, q_norm, k_norm, cos, sin):
    b, s, _ = q_lat.shape
    q = (rms_norm(q_lat, q_lat_norm) @ w_uq).reshape(b, s, MLA_HEADS, MLA_QK_DIM)
    kv = (rms_norm(kv_lat, kv_lat_norm) @ w_ukv).reshape(b, s, MLA_HEADS, MLA_NOPE_DIM + MLA_V_DIM)
    k_nope, v = kv[..., :MLA_NOPE_DIM], kv[..., MLA_NOPE_DIM:]
    k_pe = jnp.broadcast_to(k_pe[:, :, None, :], (b, s, MLA_HEADS, MLA_ROPE_DIM))
    k = jnp.concatenate([k_nope, k_pe], axis=-1)
    q = rms_norm(q, q_norm)
    k = rms_norm(k, k_norm)
    q = jnp.concatenate([q[..., :MLA_NOPE_DIM], apply_rope(q[..., MLA_NOPE_DIM:], cos, sin)], axis=-1)
    k = jnp.concatenate([k[..., :MLA_NOPE_DIM], apply_rope(k[..., MLA_NOPE_DIM:], cos, sin)], axis=-1)
    o = causal_block_attention(q, k, v, 1.0 / math.sqrt(MLA_QK_DIM))
    return o.reshape(b, s, MLA_WIDTH)


def diff_branch(q, k, v, q_norm, k_norm, lq1, lk1, lq2, lk2, subln, lambda_init, cos, sin):
    b, s, _ = q.shape
    q = rms_norm(q.reshape(b, s, 2 * DIFF_HEADS, DIFF_HEAD_DIM), q_norm)
    k = rms_norm(k.reshape(b, s, 2 * DIFF_HEADS, DIFF_HEAD_DIM), k_norm)
    q = apply_rope(q, cos, sin)
    k = apply_rope(k, cos, sin)
    v = jnp.repeat(v.reshape(b, s, DIFF_HEADS, DIFF_V_DIM), 2, axis=2)
    o = causal_block_attention(q, k, v, 1.0 / math.sqrt(DIFF_HEAD_DIM))
    o = o.reshape(b, s, DIFF_HEADS, 2, DIFF_V_DIM)
    lam = (jnp.exp(jnp.sum(lq1.astype(jnp.float32) * lk1.astype(jnp.float32)))
           - jnp.exp(jnp.sum(lq2.astype(jnp.float32) * lk2.astype(jnp.float32)))
           + lambda_init)
    o = o[..., 0, :] - lam.astype(o.dtype) * o[..., 1, :]
    o = rms_norm(o, subln) * (1.0 - lambda_init)
    return o.reshape(b, s, DIFF_WIDTH)


def moe_ffn(h, w_router, b_router, w1, b1, w2, b2):
    b, s, d = h.shape
    t = h.reshape(b * s, d)
    logits = (t @ w_router).astype(jnp.float32) + b_router.astype(jnp.float32)
    top_val, top_idx = lax.top_k(logits, TOP_K)
    top_w = jax.nn.softmax(top_val, axis=-1)
    gates = jnp.einsum('nk,nke->ne', top_w, jax.nn.one_hot(top_idx, N_EXPERTS, dtype=jnp.float32)).astype(h.dtype)
    out = jnp.zeros_like(t)
    for e in range(N_EXPERTS):
        u = t @ w1[e] + b1[e]
        glu = jnp.minimum(u[:, :D_EXPERT], SWIGLU_LIMIT)
        lin = jnp.clip(u[:, D_EXPERT:], -SWIGLU_LIMIT, SWIGLU_LIMIT)
        a = glu * jax.nn.sigmoid(SWIGLU_ALPHA * glu) * (lin + 1.0)
        out = out + gates[:, e:e + 1] * (a @ w2[e] + b2[e])
    return out.reshape(b, s, d)


def setup_inputs(seed: int = 0) -> dict:
    key = jax.random.key(seed)
    ks = jax.random.split(key, 32)
    f32 = jnp.float32

    def w(k, shape, fan_in):
        return jax.random.normal(k, shape, f32) * (fan_in ** -0.5)

    def gain(k, shape):
        return 1.0 + 0.01 * jax.random.normal(k, shape, f32)

    L = DEPTH
    return {
        'x': jax.random.normal(ks[0], (BATCH, SEQ, D_MODEL), f32),
        'attn_norm': gain(ks[1], (L, D_MODEL)),
        'w_in': w(ks[2], (L, D_MODEL, IN_WIDTH), D_MODEL),
        'mla_q_lat_norm': gain(ks[3], (L, MLA_Q_LORA)),
        'mla_kv_lat_norm': gain(ks[4], (L, MLA_KV_LORA)),
        'mla_w_uq': w(ks[5], (L, MLA_Q_LORA, MLA_HEADS * MLA_QK_DIM), MLA_Q_LORA),
        'mla_w_ukv': w(ks[6], (L, MLA_KV_LORA, MLA_HEADS * (MLA_NOPE_DIM + MLA_V_DIM)), MLA_KV_LORA),
        'mla_q_norm': gain(ks[7], (L, MLA_QK_DIM)),
        'mla_k_norm': gain(ks[8], (L, MLA_QK_DIM)),
        'diff_q_norm': gain(ks[9], (L, DIFF_HEAD_DIM)),
        'diff_k_norm': gain(ks[10], (L, DIFF_HEAD_DIM)),
        'diff_lambda_q1': 0.1 * jax.random.normal(ks[11], (L, DIFF_HEAD_DIM), f32),
        'diff_lambda_k1': 0.1 * jax.random.normal(ks[12], (L, DIFF_HEAD_DIM), f32),
        'diff_lambda_q2': 0.1 * jax.random.normal(ks[13], (L, DIFF_HEAD_DIM), f32),
        'diff_lambda_k2': 0.1 * jax.random.normal(ks[14], (L, DIFF_HEAD_DIM), f32),
        'diff_subln': gain(ks[15], (L, DIFF_V_DIM)),
        'w_branch_mla': w(ks[16], (L, MLA_WIDTH, D_MODEL), MLA_WIDTH),
        'w_branch_diff': w(ks[17], (L, DIFF_WIDTH, D_MODEL), DIFF_WIDTH),
        'w_out': w(ks[18], (L, D_MODEL, D_MODEL), D_MODEL),
        'ffn_norm': gain(ks[19], (L, D_MODEL)),
        'w_router': w(ks[20], (L, D_MODEL, N_EXPERTS), D_MODEL),
        'b_router': 0.01 * jax.random.normal(ks[21], (L, N_EXPERTS), f32),
        'w_mlp1': w(ks[22], (L, N_EXPERTS, D_MODEL, 2 * D_EXPERT), D_MODEL),
        'b_mlp1': 0.01 * jax.random.normal(ks[23], (L, N_EXPERTS, 2 * D_EXPERT), f32),
        'w_mlp2': w(ks[24], (L, N_EXPERTS, D_EXPERT, D_MODEL), D_EXPERT),
        'b_mlp2': 0.01 * jax.random.normal(ks[25], (L, N_EXPERTS, D_MODEL), f32),
    }


def reference(x, attn_norm, w_in, mla_q_lat_norm, mla_kv_lat_norm, mla_w_uq, mla_w_ukv,
              mla_q_norm, mla_k_norm, diff_q_norm, diff_k_norm, diff_lambda_q1, diff_lambda_k1,
              diff_lambda_q2, diff_lambda_k2, diff_subln, w_branch_mla, w_branch_diff, w_out,
              ffn_norm, w_router, b_router, w_mlp1, b_mlp1, w_mlp2, b_mlp2):
    s = x.shape[1]
    cos_mla, sin_mla = rope_tables(s, MLA_ROPE_DIM)
    cos_diff, sin_diff = rope_tables(s, DIFF_HEAD_DIM)
    split_points = list(np.cumsum([MLA_Q_LORA, MLA_KV_LORA, MLA_ROPE_DIM,
                                   DIFF_WIDTH, DIFF_WIDTH, DIFF_WIDTH, D_MODEL]))
    h = x
    for i in range(DEPTH):
        lambda_init = 0.8 - 0.6 * math.exp(-0.3 * i)
        xn = rms_norm(h, attn_norm[i])
        proj = xn @ w_in[i]
        q_lat, kv_lat, k_pe, dq, dk, dv, gate_a, gate_b = jnp.split(proj, split_points, axis=-1)
        o_a = mla_branch(q_lat, kv_lat, k_pe, mla_q_lat_norm[i], mla_kv_lat_norm[i], mla_w_uq[i],
                         mla_w_ukv[i], mla_q_norm[i], mla_k_norm[i], cos_mla, sin_mla)
        o_b = diff_branch(dq, dk, dv, diff_q_norm[i], diff_k_norm[i], diff_lambda_q1[i], diff_lambda_k1[i],
                          diff_lambda_q2[i], diff_lambda_k2[i], diff_subln[i], lambda_init, cos_diff, sin_diff)
        merged = (jax.nn.sigmoid(gate_a) * (o_a @ w_branch_mla[i])
                  + jax.nn.sigmoid(gate_b) * (o_b @ w_branch_diff[i]))
        h = h + merged @ w_out[i]
        h = h + moe_ffn(rms_norm(h, ffn_norm[i]), w_router[i], b_router[i],
                        w_mlp1[i], b_mlp1[i], w_mlp2[i], b_mlp2[i])
    return h
```

```python
import functools
import math

import jax
import jax.numpy as jnp
import numpy as np
from jax import lax
from jax.experimental import pallas as pl
from jax.experimental.pallas import tpu as pltpu

D_MODEL = 1024
ROPE_THETA = 10000.0
NORM_EPS = 1e-6
MLA_HEADS = 8
MLA_NOPE = 64
MLA_ROPE = 32
MLA_QK = MLA_NOPE + MLA_ROPE
MLA_V = 64
MLA_Q_LORA = 768
MLA_KV_LORA = 256
DIFF_HEADS = 4
DIFF_D = 64
DIFF_V = 128
N_EXPERTS = 32
TOP_K = 4
D_EXPERT = 1024
SWIGLU_ALPHA = 1.702
SWIGLU_LIMIT = 7.0
LAMBDA_INIT = 0.8 - 0.6 * math.exp(-0.3 * 0)

LANES = 128
V7X_VMEM_LIMIT = 56 * 1024 * 1024

PREP_TM = 256
ATT_TQ = 512
ATT_TK = 512
ROUTE_TM = 256
FFN_TM = 512

NEG_BIG = -1e30

_C_LAT = 0
_C_KPE = 1024
_C_DQ = 1280
_C_DK = 2304
_C_DV = 3328
_C_GATE = 3840
_C_END = 5888


def _f32dot(a, b):
    return jnp.dot(a, b, preferred_element_type=jnp.float32)


def _dot_nt(a, b):
    return lax.dot_general(a, b, (((1,), (1,)), ((), ())), preferred_element_type=jnp.float32)


def _rms_scale(x, width):
    return lax.rsqrt(jnp.sum(x * x, axis=-1, keepdims=True) * (1.0 / width) + NORM_EPS)


def _lane_iota(shape):
    return lax.broadcasted_iota(jnp.int32, shape, len(shape) - 1)


def _prep_kernel(x_ref, gattn_ref, w1_ref, glat_ref, wq_ref, wkv_ref, tabs_ref,
                 qa_ref, ka_ref, va_ref, qd_ref, kd_ref, vd_ref, sg_ref):
    x = x_ref[0]
    xn = x * _rms_scale(x, D_MODEL) * gattn_ref[...]
    proj = _f32dot(xn.astype(jnp.bfloat16), w1_ref[...])

    glat = glat_ref[...]
    q_lat = proj[:, 0:MLA_Q_LORA]
    kv_lat = proj[:, MLA_Q_LORA:MLA_Q_LORA + MLA_KV_LORA]
    qn = q_lat * _rms_scale(q_lat, MLA_Q_LORA) * glat[:, 0:MLA_Q_LORA]
    kvn = kv_lat * _rms_scale(kv_lat, MLA_KV_LORA) * glat[:, MLA_Q_LORA:]
    qq = _f32dot(qn.astype(jnp.bfloat16), wq_ref[...])
    kk = _f32dot(kvn.astype(jnp.bfloat16), wkv_ref[...])

    kpe = proj[:, _C_KPE:_C_KPE + LANES]
    kper = proj[:, _C_KPE + LANES:_C_KPE + 2 * LANES]
    cq, sq = tabs_ref[0], tabs_ref[1]
    ck, sk = tabs_ref[2], tabs_ref[3]
    for h in range(MLA_HEADS):
        lo, hi = h * LANES, (h + 1) * LANES
        qh = qq[:, lo:hi]
        qa_ref[0, h] = (_rms_scale(qh, MLA_QK) * (qh * cq + qq[:, 1024 + lo:1024 + hi] * sq)).astype(jnp.bfloat16)
        kh = kk[:, lo:hi] + kpe
        ka_ref[0, h] = (_rms_scale(kh, MLA_QK) * (kh * ck + kper * sk)).astype(jnp.bfloat16)
    for j in range(MLA_HEADS // 2):
        va_ref[0, j] = kk[:, 1024 + j * LANES:1024 + (j + 1) * LANES].astype(jnp.bfloat16)

    cqd, sqd = tabs_ref[4], tabs_ref[5]
    ckd, skd = tabs_ref[6], tabs_ref[7]
    lane = _lane_iota((x.shape[0], LANES))
    first = lane < DIFF_D

    def half_norm(v):
        sqv = v * v
        s_all = jnp.sum(sqv, axis=-1, keepdims=True)
        s_lo = jnp.sum(jnp.where(first, sqv, 0.0), axis=-1, keepdims=True)
        r_lo = lax.rsqrt(s_lo * (1.0 / DIFF_D) + NORM_EPS)
        r_hi = lax.rsqrt((s_all - s_lo) * (1.0 / DIFF_D) + NORM_EPS)
        return jnp.where(first, r_lo, r_hi)

    for h in range(DIFF_HEADS):
        lo, hi = h * LANES, (h + 1) * LANES
        dq = proj[:, _C_DQ + lo:_C_DQ + hi]
        dqr = proj[:, _C_DQ + 512 + lo:_C_DQ + 512 + hi]
        qv = half_norm(dq) * (dq * cqd + dqr * sqd)
        qd_ref[0, 2 * h] = jnp.where(first, qv, 0.0).astype(jnp.bfloat16)
        qd_ref[0, 2 * h + 1] = jnp.where(first, 0.0, qv).astype(jnp.bfloat16)
        dk = proj[:, _C_DK + lo:_C_DK + hi]
        dkr = proj[:, _C_DK + 512 + lo:_C_DK + 512 + hi]
        kd_ref[0, h] = (half_norm(dk) * (dk * ckd + dkr * skd)).astype(jnp.bfloat16)
        vd_ref[0, h] = proj[:, _C_DV + lo:_C_DV + hi].astype(jnp.bfloat16)

    sg_ref[0] = jax.nn.sigmoid(proj[:, _C_GATE:_C_END]).astype(jnp.bfloat16)


def _const_spec(shape):
    nd = len(shape)
    return pl.BlockSpec(shape, lambda *_: (0,) * nd)


def _prep_call(x, gattn, w1, glat, wq, wkv, tabs):
    b, s, _ = x.shape
    tm = min(PREP_TM, s)
    grid = (b, s // tm)
    head_out = lambda nh: pl.BlockSpec((1, nh, tm, LANES), lambda bi, ti: (bi, 0, ti, 0))
    bf = jnp.bfloat16
    out_shape = (
        jax.ShapeDtypeStruct((b, MLA_HEADS, s, LANES), bf),
        jax.ShapeDtypeStruct((b, MLA_HEADS, s, LANES), bf),
        jax.ShapeDtypeStruct((b, MLA_HEADS // 2, s, LANES), bf),
        jax.ShapeDtypeStruct((b, 2 * DIFF_HEADS, s, LANES), bf),
        jax.ShapeDtypeStruct((b, DIFF_HEADS, s, LANES), bf),
        jax.ShapeDtypeStruct((b, DIFF_HEADS, s, LANES), bf),
        jax.ShapeDtypeStruct((b, s, 2 * D_MODEL), bf),
    )
    return pl.pallas_call(
        _prep_kernel,
        grid=grid,
        in_specs=[
            pl.BlockSpec((1, tm, D_MODEL), lambda bi, ti: (bi, ti, 0)),
            _const_spec(gattn.shape),
            _const_spec(w1.shape),
            _const_spec(glat.shape),
            _const_spec(wq.shape),
            _const_spec(wkv.shape),
            pl.BlockSpec((8, tm, LANES), lambda bi, ti: (0, ti, 0)),
        ],
        out_specs=(
            head_out(MLA_HEADS), head_out(MLA_HEADS), head_out(MLA_HEADS // 2),
            head_out(2 * DIFF_HEADS), head_out(DIFF_HEADS), head_out(DIFF_HEADS),
            pl.BlockSpec((1, tm, 2 * D_MODEL), lambda bi, ti: (bi, ti, 0)),
        ),
        out_shape=out_shape,
        compiler_params=pltpu.CompilerParams(
            dimension_semantics=("arbitrary", "arbitrary"), vmem_limit_bytes=V7X_VMEM_LIMIT),
        name="prep",
    )(x, gattn, w1, glat, wq, wkv, tabs)


def _attn_kernel(q_ref, k_ref, v_ref, lam_ref, subln_ref, o_ref, m_scr, l_scr, acc_scr, *, mode, tq, tk):
    qi = pl.program_id(2)
    shared_k = k_ref.shape[1] == 1
    m_scr[...] = jnp.full(m_scr.shape, NEG_BIG, jnp.float32)
    l_scr[...] = jnp.zeros(l_scr.shape, jnp.float32)
    acc_scr[...] = jnp.zeros(acc_scr.shape, jnp.float32)

    def step(k0, masked):
        v = v_ref[0, 0, pl.ds(k0, tk), :]
        for hh in range(2):
            k = k_ref[0, 0 if shared_k else hh, pl.ds(k0, tk), :]
            s = _dot_nt(q_ref[0, hh], k)
            if masked:
                row = lax.broadcasted_iota(jnp.int32, s.shape, 0) + qi * tq
                col = lax.broadcasted_iota(jnp.int32, s.shape, 1) + k0
                s = jnp.where(col <= row, s, NEG_BIG)
            m_prev = m_scr[hh]
            m_new = jnp.maximum(m_prev, jnp.max(s, axis=-1, keepdims=True))
            alpha = jnp.exp(m_prev - m_new)
            p = jnp.exp(s - m_new)
            l_scr[hh] = alpha * l_scr[hh] + jnp.sum(p, axis=-1, keepdims=True)
            acc_scr[hh] = alpha * acc_scr[hh] + _f32dot(p.astype(jnp.bfloat16), v)
            m_scr[hh] = m_new

    n_full = qi * (tq // tk)

    def body(j, carry):
        step(pl.multiple_of(j * tk, tk), False)
        return carry

    lax.fori_loop(0, n_full, body, 0)
    for d in range(tq // tk):
        step(pl.multiple_of(qi * tq + d * tk, tk), True)

    oa = acc_scr[0] / l_scr[0]
    ob = acc_scr[1] / l_scr[1]
    if mode == "mla":
        lane = _lane_iota(oa.shape)
        o_ref[0] = jnp.where(lane < MLA_V, oa, ob).astype(o_ref.dtype)
    else:
        lp = lam_ref[...]
        lam = (jnp.exp(jnp.sum(lp[0:1] * lp[1:2], axis=-1, keepdims=True))
               - jnp.exp(jnp.sum(lp[2:3] * lp[3:4], axis=-1, keepdims=True)) + LAMBDA_INIT)
        o = oa - lam * ob
        o = o * _rms_scale(o, DIFF_V) * subln_ref[...] * (1.0 - LAMBDA_INIT)
        o_ref[0] = o.astype(o_ref.dtype)


def _attn_call(q, k, v, lam, subln, *, mode):
    b, _, s, _ = q.shape
    n_groups = v.shape[1]
    kh = k.shape[1] // n_groups
    tq = min(ATT_TQ, s)
    tk = min(ATT_TK, tq)
    grid = (b, n_groups, s // tq)
    kernel = functools.partial(_attn_kernel, mode=mode, tq=tq, tk=tk)
    return pl.pallas_call(
        kernel,
        grid=grid,
        in_specs=[
            pl.BlockSpec((1, 2, tq, LANES), lambda bi, g, qi: (bi, g, qi, 0)),
            pl.BlockSpec((1, kh, s, LANES), lambda bi, g, qi: (bi, g, 0, 0)),
            pl.BlockSpec((1, 1, s, LANES), lambda bi, g, qi: (bi, g, 0, 0)),
            _const_spec(lam.shape),
            _const_spec(subln.shape),
        ],
        out_specs=pl.BlockSpec((1, tq, LANES), lambda bi, g, qi: (bi, qi, g)),
        out_shape=jax.ShapeDtypeStruct((b, s, n_groups * LANES), jnp.bfloat16),
        scratch_shapes=[
            pltpu.VMEM((2, tq, 1), jnp.float32),
            pltpu.VMEM((2, tq, 1), jnp.float32),
            pltpu.VMEM((2, tq, LANES), jnp.float32),
        ],
        compiler_params=pltpu.CompilerParams(
            dimension_semantics=("arbitrary", "arbitrary", "arbitrary"), vmem_limit_bytes=V7X_VMEM_LIMIT),
        name="attn_" + mode,
    )(q, k, v, lam, subln)


def _pack_bf16_pairs(v):
    n = v.shape[1] // 2
    hi = pltpu.bitcast(v[:, :n].astype(jnp.bfloat16).astype(jnp.float32), jnp.uint32)
    lo = pltpu.bitcast(v[:, n:].astype(jnp.bfloat16).astype(jnp.float32), jnp.uint32)
    return hi | (lo >> 16)


def _unpack_bf16_pairs(u):
    hi = pltpu.bitcast(u & jnp.uint32(0xFFFF0000), jnp.float32)
    lo = pltpu.bitcast(u << 16, jnp.float32)
    return jnp.concatenate([hi, lo], axis=1)


def _route_kernel(x_ref, oa_ref, ob_ref, sg_ref, wbm_ref, wbd_ref, wout_ref, gffn_ref, wrh_ref, wrl_ref, br_ref,
                  h_ref, hn_ref, idx_ref, w_ref, pos_ref, cnt_ref, carry_scr):
    @pl.when(pl.program_id(0) == 0)
    def _():
        carry_scr[...] = jnp.zeros(carry_scr.shape, jnp.float32)

    sg = sg_ref[...]
    ma = _f32dot(oa_ref[...], wbm_ref[...])
    mb = _f32dot(ob_ref[...], wbd_ref[...])
    merged = sg[:, :D_MODEL] * ma + sg[:, D_MODEL:] * mb
    h = x_ref[...] + _f32dot(merged.astype(jnp.bfloat16), wout_ref[...])
    h_ref[...] = h

    hn = h * _rms_scale(h, D_MODEL) * gffn_ref[...]
    hn_ref[...] = _pack_bf16_pairs(hn)
    hn_hi = hn.astype(jnp.bfloat16)
    hn_lo = (hn - hn_hi.astype(jnp.float32)).astype(jnp.bfloat16)
    wrh = wrh_ref[...]
    logits = _f32dot(hn_hi, wrh) + _f32dot(hn_lo, wrh) + _f32dot(hn_hi, wrl_ref[...]) + br_ref[...]

    tm = logits.shape[0]
    lane = _lane_iota(logits.shape)
    work = logits
    vals, onehots = [], []
    for _k in range(TOP_K):
        mx = jnp.max(work, axis=-1, keepdims=True)
        sel = jnp.min(jnp.where(work == mx, lane, LANES), axis=-1, keepdims=True)
        oh = lane == sel
        vals.append(mx)
        onehots.append(oh)
        work = jnp.where(oh, NEG_BIG * 2, work)
    exps = [jnp.exp(vv - vals[0]) for vv in vals]
    denom = exps[0] + exps[1] + exps[2] + exps[3]

    chosen = jnp.zeros(logits.shape, jnp.float32)
    for oh in onehots:
        chosen = chosen + oh.astype(jnp.float32)
    r_i = lax.broadcasted_iota(jnp.int32, (tm, tm), 0)
    c_i = lax.broadcasted_iota(jnp.int32, (tm, tm), 1)
    ltri = (c_i < r_i).astype(jnp.bfloat16)
    before = _f32dot(ltri, chosen.astype(jnp.bfloat16)) + carry_scr[...]
    carry_scr[...] = carry_scr[...] + jnp.sum(chosen, axis=0, keepdims=True)
    cnt_ref[...] = carry_scr[...]

    idx_out = jnp.zeros(logits.shape, jnp.int32)
    w_out = jnp.zeros(logits.shape, jnp.float32)
    pos_out = jnp.zeros(logits.shape, jnp.int32)
    for kk in range(TOP_K):
        oh = onehots[kk]
        e_k = jnp.sum(jnp.where(oh, lane, 0), axis=-1, keepdims=True)
        p_k = jnp.sum(jnp.where(oh, before, 0.0), axis=-1, keepdims=True).astype(jnp.int32)
        idx_out = jnp.where(lane == kk, e_k, idx_out)
        pos_out = jnp.where(lane == kk, p_k, pos_out)
        w_out = jnp.where(lane == kk, exps[kk] / denom, w_out)
    idx_ref[...] = idx_out
    w_ref[...] = w_out
    pos_ref[...] = pos_out


def _route_call(x2, oa, ob, sg, wbm, wbd, wout, gffn, wrh, wrl, br):
    n = x2.shape[0]
    tm = min(ROUTE_TM, n)
    row = lambda w: pl.BlockSpec((tm, w), lambda i: (i, 0))
    out_shape = (
        jax.ShapeDtypeStruct((n, D_MODEL), jnp.float32),
        jax.ShapeDtypeStruct((n, D_MODEL // 2), jnp.uint32),
        jax.ShapeDtypeStruct((n, LANES), jnp.int32),
        jax.ShapeDtypeStruct((n, LANES), jnp.float32),
        jax.ShapeDtypeStruct((n, LANES), jnp.int32),
        jax.ShapeDtypeStruct((1, LANES), jnp.float32),
    )
    return pl.pallas_call(
        _route_kernel,
        grid=(n // tm,),
        in_specs=[row(D_MODEL), row(512), row(512), row(2 * D_MODEL),
                  _const_spec(wbm.shape), _const_spec(wbd.shape), _const_spec(wout.shape), _const_spec(gffn.shape),
                  _const_spec(wrh.shape), _const_spec(wrl.shape), _const_spec(br.shape)],
        out_specs=(row(D_MODEL), row(D_MODEL // 2), row(LANES), row(LANES), row(LANES), _const_spec((1, LANES))),
        out_shape=out_shape,
        scratch_shapes=[pltpu.VMEM((1, LANES), jnp.float32)],
        compiler_params=pltpu.CompilerParams(dimension_semantics=("arbitrary",), vmem_limit_bytes=V7X_VMEM_LIMIT),
        name="route",
    )(x2, oa, ob, sg, wbm, wbd, wout, gffn, wrh, wrl, br)


def _dispatch_kernel(dest_ref, hn_ref, xs_ref, sem):
    tm = hn_ref.shape[0]

    def issue(i, carry):
        for kk in range(TOP_K):
            d = dest_ref[0, 0, i * TOP_K + kk]
            pltpu.make_async_copy(hn_ref.at[pl.ds(i, 1)], xs_ref.at[pl.ds(d, 1)], sem).start()
        return carry

    lax.fori_loop(0, tm, issue, 0)
    for _k in range(TOP_K):
        pltpu.make_async_copy(hn_ref, xs_ref.at[pl.ds(0, tm)], sem).wait()


def _dispatch_call(dest3, hn, n_rows):
    n = hn.shape[0]
    tm = dest3.shape[2] // TOP_K
    return pl.pallas_call(
        _dispatch_kernel,
        grid=(n // tm,),
        in_specs=[
            pl.BlockSpec((1, 1, tm * TOP_K), lambda i: (i, 0, 0), memory_space=pltpu.SMEM),
            pl.BlockSpec((tm, D_MODEL // 2), lambda i: (i, 0)),
        ],
        out_specs=pl.BlockSpec(memory_space=pl.ANY),
        out_shape=jax.ShapeDtypeStruct((n_rows, D_MODEL // 2), jnp.uint32),
        scratch_shapes=[pltpu.SemaphoreType.DMA(())],
        compiler_params=pltpu.CompilerParams(dimension_semantics=("arbitrary",), has_side_effects=True),
        name="dispatch",
    )(dest3, hn)


def _ffn_kernel(te_ref, tv_ref, nt_ref, xs_ref, w1_ref, b1_ref, w2_ref, b2_ref, y_ref):
    i = pl.program_id(0)

    @pl.when(i < nt_ref[0])
    def _():
        tm = xs_ref.shape[0]
        rows = lax.broadcasted_iota(jnp.int32, (tm, 1), 0)
        xt = jnp.where(rows < tv_ref[i], _unpack_bf16_pairs(xs_ref[...]), 0.0).astype(jnp.bfloat16)
        u = _f32dot(xt, w1_ref[0]) + b1_ref[0]
        glu = jnp.minimum(u[:, :D_EXPERT], SWIGLU_LIMIT)
        lin = jnp.clip(u[:, D_EXPERT:], -SWIGLU_LIMIT, SWIGLU_LIMIT)
        a = glu * jax.nn.sigmoid(SWIGLU_ALPHA * glu) * (lin + 1.0)
        y = _f32dot(a.astype(jnp.bfloat16), w2_ref[0]) + b2_ref[0]
        y_ref[...] = _pack_bf16_pairs(y)


def _ffn_call(tile_expert, tile_valid, n_tiles, xs, w1, b1, w2, b2):
    n_rows = xs.shape[0]
    tm = FFN_TM
    t_tiles = n_rows // tm

    def tile_map(i, te, tv, nt):
        return (jnp.minimum(i, nt[0] - 1), 0)

    def w_map(i, te, tv, nt):
        return (te[i], 0, 0)

    grid_spec = pltpu.PrefetchScalarGridSpec(
        num_scalar_prefetch=3,
        grid=(t_tiles,),
        in_specs=[
            pl.BlockSpec((tm, D_MODEL // 2), tile_map),
            pl.BlockSpec((1, D_MODEL, 2 * D_EXPERT), w_map),
            pl.BlockSpec((1, 1, 2 * D_EXPERT), w_map),
            pl.BlockSpec((1, D_EXPERT, D_MODEL), w_map),
            pl.BlockSpec((1, 1, D_MODEL), w_map),
        ],
        out_specs=pl.BlockSpec((tm, D_MODEL // 2), tile_map),
    )
    return pl.pallas_call(
        _ffn_kernel,
        grid_spec=grid_spec,
        out_shape=jax.ShapeDtypeStruct((n_rows, D_MODEL // 2), jnp.uint32),
        compiler_params=pltpu.CompilerParams(dimension_semantics=("arbitrary",), vmem_limit_bytes=V7X_VMEM_LIMIT),
        name="ffn",
    )(tile_expert, tile_valid, n_tiles, xs, w1, b1, w2, b2)


def _combine_kernel(dest_ref, w_ref, h_ref, y_ref, o_ref, gbuf, sem):
    tm = h_ref.shape[0]

    def issue(i, carry):
        for kk in range(TOP_K):
            d = dest_ref[0, 0, i * TOP_K + kk]
            pltpu.make_async_copy(y_ref.at[pl.ds(d, 1)], gbuf.at[kk, pl.ds(i, 1)], sem).start()
        return carry

    lax.fori_loop(0, tm, issue, 0)
    for kk in range(TOP_K):
        pltpu.make_async_copy(y_ref.at[pl.ds(0, tm)], gbuf.at[kk], sem).wait()

    w = w_ref[...]
    out = h_ref[...]
    for kk in range(TOP_K):
        out = out + w[:, kk:kk + 1] * _unpack_bf16_pairs(gbuf[kk])
    o_ref[...] = out


def _combine_call(dest3, w, h, y):
    n = h.shape[0]
    tm = dest3.shape[2] // TOP_K
    return pl.pallas_call(
        _combine_kernel,
        grid=(n // tm,),
        in_specs=[
            pl.BlockSpec((1, 1, tm * TOP_K), lambda i: (i, 0, 0), memory_space=pltpu.SMEM),
            pl.BlockSpec((tm, LANES), lambda i: (i, 0)),
            pl.BlockSpec((tm, D_MODEL), lambda i: (i, 0)),
            pl.BlockSpec(memory_space=pl.ANY),
        ],
        out_specs=pl.BlockSpec((tm, D_MODEL), lambda i: (i, 0)),
        out_shape=jax.ShapeDtypeStruct((n, D_MODEL), jnp.float32),
        scratch_shapes=[pltpu.VMEM((TOP_K, tm, D_MODEL // 2), jnp.uint32), pltpu.SemaphoreType.DMA(())],
        compiler_params=pltpu.CompilerParams(dimension_semantics=("arbitrary",), vmem_limit_bytes=V7X_VMEM_LIMIT),
        name="combine",
    )(dest3, w, h, y)


def _rot_cols(w, group):
    lead = w.shape[:-1]
    g = w.reshape(lead + (-1, group))
    half = group // 2
    return jnp.concatenate([-g[..., half:], g[..., :half]], axis=-1).reshape(w.shape)


def _rope_tables(s, dim):
    inv_freq = 1.0 / (ROPE_THETA ** (jnp.arange(0, dim, 2, dtype=jnp.float32) / dim))
    ang = jnp.arange(s, dtype=jnp.float32)[:, None] * inv_freq[None, :]
    return jnp.cos(ang), jnp.sin(ang)


def _mla_tables(s, gain, scale):
    cos, sin = _rope_tables(s, MLA_ROPE)
    cos2 = jnp.concatenate([cos, cos], axis=-1)
    sin2 = jnp.concatenate([sin, sin], axis=-1)
    g_nope, g_rope = gain[:MLA_NOPE], gain[MLA_NOPE:]
    g_perm = jnp.concatenate([g_rope[MLA_ROPE // 2:], g_rope[:MLA_ROPE // 2]])
    pad = jnp.zeros((s, LANES - MLA_QK), jnp.float32)
    c = jnp.concatenate([jnp.broadcast_to(g_nope, (s, MLA_NOPE)), cos2 * g_rope, pad], axis=-1) * scale
    sn = jnp.concatenate([jnp.zeros((s, MLA_NOPE), jnp.float32), sin2 * g_perm, pad], axis=-1) * scale
    return c, sn


def _diff_tables(s, gain, scale):
    cos, sin = _rope_tables(s, DIFF_D)
    cos2 = jnp.concatenate([cos, cos], axis=-1)
    sin2 = jnp.concatenate([sin, sin], axis=-1)
    g_perm = jnp.concatenate([gain[DIFF_D // 2:], gain[:DIFF_D // 2]])
    c = cos2 * gain * scale
    sn = sin2 * g_perm * scale
    return jnp.concatenate([c, c], axis=-1), jnp.concatenate([sn, sn], axis=-1)


def _pad_heads(w, width):
    kdim = w.shape[0]
    g = w.reshape(kdim, -1, width)
    g = jnp.pad(g, ((0, 0), (0, 0), (0, LANES - width)))
    return g.reshape(kdim, -1)


def kernel(x, attn_norm, w_in, mla_q_lat_norm, mla_kv_lat_norm, mla_w_uq, mla_w_ukv, mla_q_norm, mla_k_norm,
           diff_q_norm, diff_k_norm, diff_lambda_q1, diff_lambda_k1, diff_lambda_q2, diff_lambda_k2, diff_subln,
           w_branch_mla, w_branch_diff, w_out, ffn_norm, w_router, b_router, w_mlp1, b_mlp1, w_mlp2, b_mlp2):
    b, s, d = x.shape
    n = b * s
    bf = jnp.bfloat16
    f32 = jnp.float32
    i = 0

    wi = w_in[i]
    c0 = MLA_Q_LORA + MLA_KV_LORA
    w_kpe = wi[:, c0:c0 + MLA_ROPE]
    c1 = c0 + MLA_ROPE
    w_dq, w_dk, w_dv = wi[:, c1:c1 + 512], wi[:, c1 + 512:c1 + 1024], wi[:, c1 + 1024:c1 + 1536]
    w_gate = wi[:, c1 + 1536:]
    place = lambda w: jnp.pad(w, ((0, 0), (MLA_NOPE, LANES - MLA_QK)))
    w1 = jnp.concatenate([
        wi[:, :c0], place(w_kpe), place(_rot_cols(w_kpe, MLA_ROPE)),
        w_dq, _rot_cols(w_dq, DIFF_D), w_dk, _rot_cols(w_dk, DIFF_D), w_dv, w_gate], axis=1).astype(bf)

    wuq = mla_w_uq[i].reshape(MLA_Q_LORA, MLA_HEADS, MLA_QK)
    wuq_rot = jnp.concatenate(
        [jnp.zeros((MLA_Q_LORA, MLA_HEADS, MLA_NOPE), f32), _rot_cols(wuq[..., MLA_NOPE:], MLA_ROPE)], axis=-1)
    wq = jnp.concatenate([_pad_heads(wuq.reshape(MLA_Q_LORA, -1), MLA_QK),
                          _pad_heads(wuq_rot.reshape(MLA_Q_LORA, -1), MLA_QK)], axis=1).astype(bf)
    wukv = mla_w_ukv[i].reshape(MLA_KV_LORA, MLA_HEADS, MLA_NOPE + MLA_V)
    wkv = jnp.concatenate([_pad_heads(wukv[..., :MLA_NOPE].reshape(MLA_KV_LORA, -1), MLA_NOPE),
                           wukv[..., MLA_NOPE:].reshape(MLA_KV_LORA, -1)], axis=1).astype(bf)

    cq, sq = _mla_tables(s, mla_q_norm[i], 1.0 / math.sqrt(MLA_QK))
    ck, sk = _mla_tables(s, mla_k_norm[i], 1.0)
    cqd, sqd = _diff_tables(s, diff_q_norm[i], 1.0 / math.sqrt(DIFF_D))
    ckd, skd = _diff_tables(s, diff_k_norm[i], 1.0)
    tabs = jnp.stack([cq, sq, ck, sk, cqd, sqd, ckd, skd])

    glat = jnp.concatenate([mla_q_lat_norm[i], mla_kv_lat_norm[i]])[None, :]
    qa, ka, va, qd, kd, vd, sg = _prep_call(x, attn_norm[i][None, :], w1, glat, wq, wkv, tabs)

    lam = jnp.pad(jnp.stack([diff_lambda_q1[i], diff_lambda_k1[i], diff_lambda_q2[i], diff_lambda_k2[i]]),
                  ((0, 0), (0, LANES - DIFF_D)))
    subln = diff_subln[i][None, :]
    o_a = _attn_call(qa, ka, va, lam, subln, mode="mla")
    o_b = _attn_call(qd, kd, vd, lam, subln, mode="diff")

    wr = jnp.pad(w_router[i], ((0, 0), (0, LANES - N_EXPERTS)))
    wr_hi = wr.astype(bf)
    wr_lo = (wr - wr_hi.astype(f32)).astype(bf)
    br = jnp.concatenate([b_router[i], jnp.full((LANES - N_EXPERTS,), NEG_BIG, f32)])[None, :]
    h, hn, idx, gw, pos, cnt = _route_call(
        x.reshape(n, d), o_a.reshape(n, 512), o_b.reshape(n, 512), sg.reshape(n, 2 * d),
        w_branch_mla[i].astype(bf), w_branch_diff[i].astype(bf), w_out[i].astype(bf), ffn_norm[i][None, :],
        wr_hi, wr_lo, br)

    counts = cnt[0, :N_EXPERTS].astype(jnp.int32)
    tiles_per = (counts + FFN_TM - 1) // FFN_TM
    tile_end = jnp.cumsum(tiles_per)
    tile_start = tile_end - tiles_per
    n_tiles = tile_end[-1:]
    t_tiles = n * TOP_K // FFN_TM + N_EXPERTS
    tids = jnp.arange(t_tiles, dtype=jnp.int32)
    tile_expert = jnp.minimum(jnp.searchsorted(tile_end, tids, side="right"), N_EXPERTS - 1).astype(jnp.int32)
    tile_valid = jnp.clip(counts[tile_expert] - (tids - tile_start[tile_expert]) * FFN_TM, 0, FFN_TM).astype(jnp.int32)
    dest = (tile_start * FFN_TM)[idx[:, :TOP_K]] + pos[:, :TOP_K]
    tm_r = min(ROUTE_TM, n)
    dest3 = dest.reshape(n // tm_r, 1, tm_r * TOP_K)

    xs = _dispatch_call(dest3, hn, t_tiles * FFN_TM)
    y = _ffn_call(tile_expert, tile_valid, n_tiles.astype(jnp.int32), xs,
                  w_mlp1[i].astype(bf), b_mlp1[i][:, None, :], w_mlp2[i].astype(bf), b_mlp2[i][:, None, :])
    out = _combine_call(dest3, gw, h, y)
    return out.reshape(b, s, d)
```

```python
import functools
import math

import jax
import jax.numpy as jnp
import numpy as np
from jax import lax
from jax.experimental import pallas as pl
from jax.experimental.pallas import tpu as pltpu

D_MODEL = 1024
ROPE_THETA = 10000.0
NORM_EPS = 1e-6
MLA_HEADS = 8
MLA_NOPE = 64
MLA_ROPE = 32
MLA_QK = MLA_NOPE + MLA_ROPE
MLA_V = 64
MLA_Q_LORA = 768
MLA_KV_LORA = 256
DIFF_HEADS = 4
DIFF_D = 64
DIFF_V = 128
N_EXPERTS = 32
TOP_K = 4
D_EXPERT = 1024
SWIGLU_ALPHA = 1.702
SWIGLU_LIMIT = 7.0
LAMBDA_INIT = 0.8 - 0.6 * math.exp(-0.3 * 0)

LANES = 128
V7X_VMEM_LIMIT = 56 * 1024 * 1024

PREP_TM = 256
ATT_TQ = 512
ROUTE_TM = 256
FFN_TM = 512

NEG_BIG = -1e30
LOG2E = math.log2(math.e)

_C_LAT = 0
_C_KPE = 1024
_C_DQ = 1280
_C_DK = 2304
_C_DV = 3328
_C_GATE = 3840
_C_END = 5888


def _f32dot(a, b):
    return jnp.dot(a, b, preferred_element_type=jnp.float32)


def _dot_nt(a, b):
    return lax.dot_general(a, b, (((1,), (1,)), ((), ())), preferred_element_type=jnp.float32)


def _rms_scale(x, width):
    return lax.rsqrt(jnp.sum(x * x, axis=-1, keepdims=True) * (1.0 / width) + NORM_EPS)


def _lane_iota(shape):
    return lax.broadcasted_iota(jnp.int32, shape, len(shape) - 1)


def _prep_kernel(x_ref, gattn_ref, w1_ref, glat_ref, wq_ref, wkv_ref, tabs_ref,
                 qa_ref, ka_ref, va_ref, qd_ref, kd_ref, vd_ref, sg_ref):
    x = x_ref[0]
    xn = x * _rms_scale(x, D_MODEL) * gattn_ref[...]
    proj = _f32dot(xn.astype(jnp.bfloat16), w1_ref[...])

    glat = glat_ref[...]
    q_lat = proj[:, 0:MLA_Q_LORA]
    kv_lat = proj[:, MLA_Q_LORA:MLA_Q_LORA + MLA_KV_LORA]
    qn = q_lat * _rms_scale(q_lat, MLA_Q_LORA) * glat[:, 0:MLA_Q_LORA]
    kvn = kv_lat * _rms_scale(kv_lat, MLA_KV_LORA) * glat[:, MLA_Q_LORA:]
    qq = _f32dot(qn.astype(jnp.bfloat16), wq_ref[...])
    kk = _f32dot(kvn.astype(jnp.bfloat16), wkv_ref[...])

    kpe = proj[:, _C_KPE:_C_KPE + LANES]
    kper = proj[:, _C_KPE + LANES:_C_KPE + 2 * LANES]
    cq, sq = tabs_ref[0], tabs_ref[1]
    ck, sk = tabs_ref[2], tabs_ref[3]
    for h in range(MLA_HEADS):
        lo, hi = h * LANES, (h + 1) * LANES
        qh = qq[:, lo:hi]
        qa_ref[0, h] = (_rms_scale(qh, MLA_QK) * (qh * cq + qq[:, 1024 + lo:1024 + hi] * sq)).astype(jnp.bfloat16)
        kh = kk[:, lo:hi] + kpe
        ka_ref[0, h] = (_rms_scale(kh, MLA_QK) * (kh * ck + kper * sk)).astype(jnp.bfloat16)
    for j in range(MLA_HEADS // 2):
        va_ref[0, j, 0] = kk[:, 1024 + j * LANES:1024 + (j + 1) * LANES].T.astype(jnp.bfloat16)

    cqd, sqd = tabs_ref[4], tabs_ref[5]
    ckd, skd = tabs_ref[6], tabs_ref[7]
    lane = _lane_iota((x.shape[0], LANES))
    first = lane < DIFF_D

    def half_norm(v):
        sqv = v * v
        s_all = jnp.sum(sqv, axis=-1, keepdims=True)
        s_lo = jnp.sum(jnp.where(first, sqv, 0.0), axis=-1, keepdims=True)
        r_lo = lax.rsqrt(s_lo * (1.0 / DIFF_D) + NORM_EPS)
        r_hi = lax.rsqrt((s_all - s_lo) * (1.0 / DIFF_D) + NORM_EPS)
        return jnp.where(first, r_lo, r_hi)

    for h in range(DIFF_HEADS):
        lo, hi = h * LANES, (h + 1) * LANES
        dq = proj[:, _C_DQ + lo:_C_DQ + hi]
        dqr = proj[:, _C_DQ + 512 + lo:_C_DQ + 512 + hi]
        qv = half_norm(dq) * (dq * cqd + dqr * sqd)
        qd_ref[0, 2 * h] = jnp.where(first, qv, 0.0).astype(jnp.bfloat16)
        qd_ref[0, 2 * h + 1] = jnp.where(first, 0.0, qv).astype(jnp.bfloat16)
        dk = proj[:, _C_DK + lo:_C_DK + hi]
        dkr = proj[:, _C_DK + 512 + lo:_C_DK + 512 + hi]
        kd_ref[0, h] = (half_norm(dk) * (dk * ckd + dkr * skd)).astype(jnp.bfloat16)
        vd_ref[0, h, 0] = proj[:, _C_DV + lo:_C_DV + hi].T.astype(jnp.bfloat16)

    sg_ref[0] = jax.nn.sigmoid(proj[:, _C_GATE:_C_END]).astype(jnp.bfloat16)


def _att_tiles(s):
    tq = min(ATT_TQ, s)
    return tq, tq


def _const_spec(shape):
    nd = len(shape)
    return pl.BlockSpec(shape, lambda *_: (0,) * nd)


def _prep_call(x, gattn, w1, glat, wq, wkv, tabs):
    b, s, _ = x.shape
    tm = min(PREP_TM, s)
    grid = (b, s // tm)
    head_out = lambda nh: pl.BlockSpec((1, nh, tm, LANES), lambda bi, ti: (bi, 0, ti, 0))
    _, tk = _att_tiles(s)
    per = tk // tm
    vt_out = lambda nh: pl.BlockSpec((1, nh, 1, LANES, tm), lambda bi, ti: (bi, 0, ti // per, 0, ti % per))
    bf = jnp.bfloat16
    out_shape = (
        jax.ShapeDtypeStruct((b, MLA_HEADS, s, LANES), bf),
        jax.ShapeDtypeStruct((b, MLA_HEADS, s, LANES), bf),
        jax.ShapeDtypeStruct((b, MLA_HEADS // 2, s // tk, LANES, tk), bf),
        jax.ShapeDtypeStruct((b, 2 * DIFF_HEADS, s, LANES), bf),
        jax.ShapeDtypeStruct((b, DIFF_HEADS, s, LANES), bf),
        jax.ShapeDtypeStruct((b, DIFF_HEADS, s // tk, LANES, tk), bf),
        jax.ShapeDtypeStruct((b, s, 2 * D_MODEL), bf),
    )
    return pl.pallas_call(
        _prep_kernel,
        grid=grid,
        in_specs=[
            pl.BlockSpec((1, tm, D_MODEL), lambda bi, ti: (bi, ti, 0)),
            _const_spec(gattn.shape),
            _const_spec(w1.shape),
            _const_spec(glat.shape),
            _const_spec(wq.shape),
            _const_spec(wkv.shape),
            pl.BlockSpec((8, tm, LANES), lambda bi, ti: (0, ti, 0)),
        ],
        out_specs=(
            head_out(MLA_HEADS), head_out(MLA_HEADS), vt_out(MLA_HEADS // 2),
            head_out(2 * DIFF_HEADS), head_out(DIFF_HEADS), vt_out(DIFF_HEADS),
            pl.BlockSpec((1, tm, 2 * D_MODEL), lambda bi, ti: (bi, ti, 0)),
        ),
        out_shape=out_shape,
        compiler_params=pltpu.CompilerParams(
            dimension_semantics=("arbitrary", "arbitrary"), vmem_limit_bytes=V7X_VMEM_LIMIT),
        name="prep",
    )(x, gattn, w1, glat, wq, wkv, tabs)


def _attn_kernel(q_ref, k_ref, vt_ref, lam_ref, subln_ref, o_ref, st_buf, mx_buf, m_scr, l_scr, acc_scr, *, mode, tq):
    qi = pl.program_id(2)
    shared_k = k_ref.shape[1] == 1
    m_scr[...] = jnp.full(m_scr.shape, NEG_BIG, jnp.float32)
    l_scr[...] = jnp.zeros(l_scr.shape, jnp.float32)
    acc_scr[...] = jnp.zeros(acc_scr.shape, jnp.float32)

    def scores(kb, slot, masked):
        k0 = pl.multiple_of(kb * tq, tq)
        for hh in range(2):
            k = k_ref[0, 0 if shared_k else hh, pl.ds(k0, tq), :]
            st = _dot_nt(k, q_ref[0, hh])
            if masked:
                key = lax.broadcasted_iota(jnp.int32, st.shape, 0)
                qry = lax.broadcasted_iota(jnp.int32, st.shape, 1)
                st = jnp.where(key <= qry, st, NEG_BIG)
            st_buf[slot, hh] = st
            mx_buf[slot, hh] = jnp.max(st, axis=0, keepdims=True)

    def softmax_pv(kb, slot):
        vt = vt_ref[0, 0, kb]
        for hh in range(2):
            m_prev = m_scr[hh]
            m_new = jnp.maximum(m_prev, mx_buf[slot, hh])
            alpha = jnp.exp2(m_prev - m_new)
            p = jnp.exp2(st_buf[slot, hh] - m_new)
            l_scr[hh] = alpha * l_scr[hh] + jnp.sum(p, axis=0, keepdims=True)
            acc_scr[hh] = alpha * acc_scr[hh] + _f32dot(vt, p.astype(jnp.bfloat16))
            m_scr[hh] = m_new

    scores(qi, 0, True)

    def body(j, carry):
        scores(2 * j, 1, False)
        softmax_pv(jnp.where(j == 0, qi, 2 * j - 1), 0)
        scores(2 * j + 1, 0, False)
        softmax_pv(2 * j, 1)
        return carry

    lax.fori_loop(0, qi // 2, body, 0)
    last = qi - 1

    @pl.when(qi % 2 == 1)
    def _():
        scores(last, 1, False)
        softmax_pv(jnp.where(qi == 1, qi, last - 1), 0)
        softmax_pv(last, 1)

    @pl.when(qi % 2 == 0)
    def _():
        softmax_pv(jnp.where(qi == 0, qi, last), 0)

    oa = acc_scr[0] / l_scr[0]
    ob = acc_scr[1] / l_scr[1]
    if mode == "mla":
        sub = lax.broadcasted_iota(jnp.int32, oa.shape, 0)
        o_ref[0] = jnp.where(sub < MLA_V, oa, ob).T.astype(o_ref.dtype)
    else:
        lp = lam_ref[...]
        lam = (jnp.exp(jnp.sum(lp[0:1] * lp[1:2], axis=-1, keepdims=True))
               - jnp.exp(jnp.sum(lp[2:3] * lp[3:4], axis=-1, keepdims=True)) + LAMBDA_INIT)
        o = oa - lam * ob
        o = o * lax.rsqrt(jnp.sum(o * o, axis=0, keepdims=True) * (1.0 / DIFF_V) + NORM_EPS)
        o_ref[0] = (o.T * subln_ref[...] * (1.0 - LAMBDA_INIT)).astype(o_ref.dtype)


def _attn_call(q, k, vt, lam, subln, *, mode):
    b, _, s, _ = q.shape
    n_groups = vt.shape[1]
    kh = k.shape[1] // n_groups
    tq, tk = _att_tiles(s)
    grid = (b, n_groups, s // tq)
    assert tk == tq
    kernel = functools.partial(_attn_kernel, mode=mode, tq=tq)
    return pl.pallas_call(
        kernel,
        grid=grid,
        in_specs=[
            pl.BlockSpec((1, 2, tq, LANES), lambda bi, g, qi: (bi, g, qi, 0)),
            pl.BlockSpec((1, kh, s, LANES), lambda bi, g, qi: (bi, g, 0, 0)),
            pl.BlockSpec((1, 1, s // tk, LANES, tk), lambda bi, g, qi: (bi, g, 0, 0, 0)),
            _const_spec(lam.shape),
            _const_spec(subln.shape),
        ],
        out_specs=pl.BlockSpec((1, tq, LANES), lambda bi, g, qi: (bi, qi, g)),
        out_shape=jax.ShapeDtypeStruct((b, s, n_groups * LANES), jnp.bfloat16),
        scratch_shapes=[
            pltpu.VMEM((2, 2, tq, tq), jnp.float32),
            pltpu.VMEM((2, 2, 1, tq), jnp.float32),
            pltpu.VMEM((2, 1, tq), jnp.float32),
            pltpu.VMEM((2, 1, tq), jnp.float32),
            pltpu.VMEM((2, LANES, tq), jnp.float32),
        ],
        compiler_params=pltpu.CompilerParams(
            dimension_semantics=("arbitrary", "arbitrary", "arbitrary"), vmem_limit_bytes=V7X_VMEM_LIMIT),
        name="attn_" + mode,
    )(q, k, vt, lam, subln)


def _pack_bf16_pairs(v):
    n = v.shape[1] // 2
    hi = pltpu.bitcast(v[:, :n].astype(jnp.bfloat16).astype(jnp.float32), jnp.uint32)
    lo = pltpu.bitcast(v[:, n:].astype(jnp.bfloat16).astype(jnp.float32), jnp.uint32)
    return hi | (lo >> 16)


def _unpack_bf16_pairs(u):
    hi = pltpu.bitcast(u & jnp.uint32(0xFFFF0000), jnp.float32)
    lo = pltpu.bitcast(u << 16, jnp.float32)
    return jnp.concatenate([hi, lo], axis=1)


def _route_kernel(x_ref, oa_ref, ob_ref, sg_ref, wbm_ref, wbd_ref, wout_ref, gffn_ref, wrh_ref, wrl_ref, br_ref,
                  h_ref, hn_ref, idx_ref, w_ref, pos_ref, cnt_ref, carry_scr):
    @pl.when(pl.program_id(0) == 0)
    def _():
        carry_scr[...] = jnp.zeros(carry_scr.shape, jnp.float32)

    sg = sg_ref[...]
    ma = _f32dot(oa_ref[...], wbm_ref[...])
    mb = _f32dot(ob_ref[...], wbd_ref[...])
    merged = sg[:, :D_MODEL] * ma + sg[:, D_MODEL:] * mb
    h = x_ref[...] + _f32dot(merged.astype(jnp.bfloat16), wout_ref[...])
    h_ref[...] = h

    hn = h * _rms_scale(h, D_MODEL) * gffn_ref[...]
    hn_ref[...] = _pack_bf16_pairs(hn)
    hn_hi = hn.astype(jnp.bfloat16)
    hn_lo = (hn - hn_hi.astype(jnp.float32)).astype(jnp.bfloat16)
    wrh = wrh_ref[...]
    logits = _f32dot(hn_hi, wrh) + _f32dot(hn_lo, wrh) + _f32dot(hn_hi, wrl_ref[...]) + br_ref[...]

    tm = logits.shape[0]
    lane = _lane_iota(logits.shape)
    work = logits
    vals, onehots = [], []
    for _k in range(TOP_K):
        mx = jnp.max(work, axis=-1, keepdims=True)
        sel = jnp.min(jnp.where(work == mx, lane, LANES), axis=-1, keepdims=True)
        oh = lane == sel
        vals.append(mx)
        onehots.append(oh)
        work = jnp.where(oh, NEG_BIG * 2, work)
    exps = [jnp.exp(vv - vals[0]) for vv in vals]
    denom = exps[0] + exps[1] + exps[2] + exps[3]

    chosen = jnp.zeros(logits.shape, jnp.float32)
    for oh in onehots:
        chosen = chosen + oh.astype(jnp.float32)
    r_i = lax.broadcasted_iota(jnp.int32, (tm, tm), 0)
    c_i = lax.broadcasted_iota(jnp.int32, (tm, tm), 1)
    ltri = (c_i < r_i).astype(jnp.bfloat16)
    before = _f32dot(ltri, chosen.astype(jnp.bfloat16)) + carry_scr[...]
    carry_scr[...] = carry_scr[...] + jnp.sum(chosen, axis=0, keepdims=True)
    cnt_ref[...] = carry_scr[...]

    idx_out = jnp.zeros(logits.shape, jnp.int32)
    w_out = jnp.zeros(logits.shape, jnp.float32)
    pos_out = jnp.zeros(logits.shape, jnp.int32)
    for kk in range(TOP_K):
        oh = onehots[kk]
        e_k = jnp.sum(jnp.where(oh, lane, 0), axis=-1, keepdims=True)
        p_k = jnp.sum(jnp.where(oh, before, 0.0), axis=-1, keepdims=True).astype(jnp.int32)
        idx_out = jnp.where(lane == kk, e_k, idx_out)
        pos_out = jnp.where(lane == kk, p_k, pos_out)
        w_out = jnp.where(lane == kk, exps[kk] / denom, w_out)
    idx_ref[...] = idx_out
    w_ref[...] = w_out
    pos_ref[...] = pos_out


def _route_call(x2, oa, ob, sg, wbm, wbd, wout, gffn, wrh, wrl, br):
    n = x2.shape[0]
    tm = min(ROUTE_TM, n)
    row = lambda w: pl.BlockSpec((tm, w), lambda i: (i, 0))
    out_shape = (
        jax.ShapeDtypeStruct((n, D_MODEL), jnp.float32),
        jax.ShapeDtypeStruct((n, D_MODEL // 2), jnp.uint32),
        jax.ShapeDtypeStruct((n, LANES), jnp.int32),
        jax.ShapeDtypeStruct((n, LANES), jnp.float32),
        jax.ShapeDtypeStruct((n, LANES), jnp.int32),
        jax.ShapeDtypeStruct((1, LANES), jnp.float32),
    )
    return pl.pallas_call(
        _route_kernel,
        grid=(n // tm,),
        in_specs=[row(D_MODEL), row(512), row(512), row(2 * D_MODEL),
                  _const_spec(wbm.shape), _const_spec(wbd.shape), _const_spec(wout.shape), _const_spec(gffn.shape),
                  _const_spec(wrh.shape), _const_spec(wrl.shape), _const_spec(br.shape)],
        out_specs=(row(D_MODEL), row(D_MODEL // 2), row(LANES), row(LANES), row(LANES), _const_spec((1, LANES))),
        out_shape=out_shape,
        scratch_shapes=[pltpu.VMEM((1, LANES), jnp.float32)],
        compiler_params=pltpu.CompilerParams(dimension_semantics=("arbitrary",), vmem_limit_bytes=V7X_VMEM_LIMIT),
        name="route",
    )(x2, oa, ob, sg, wbm, wbd, wout, gffn, wrh, wrl, br)


def _dispatch_kernel(dest_ref, pad_ref, hn_ref, xs_ref, zrow, sem, zsem):
    tm = hn_ref.shape[0]
    n_pad = pad_ref.shape[2]
    zrow[...] = jnp.zeros(zrow.shape, zrow.dtype)

    def issue(i, carry):
        for kk in range(TOP_K):
            d = dest_ref[0, 0, i * TOP_K + kk]
            pltpu.make_async_copy(hn_ref.at[pl.ds(i, 1)], xs_ref.at[pl.ds(d, 1)], sem).start()
        return carry

    def issue_pad(j, carry):
        pltpu.make_async_copy(zrow.at[pl.ds(0, 1)], xs_ref.at[pl.ds(pad_ref[0, 0, j], 1)], zsem).start()
        return carry

    lax.fori_loop(0, tm, issue, 0)
    lax.fori_loop(0, n_pad, issue_pad, 0)
    for _k in range(TOP_K):
        pltpu.make_async_copy(hn_ref, xs_ref.at[pl.ds(0, tm)], sem).wait()

    def wait_pad(j, carry):
        pltpu.make_async_copy(zrow.at[pl.ds(0, 1)], xs_ref.at[pl.ds(0, 1)], zsem).wait()
        return carry

    lax.fori_loop(0, n_pad, wait_pad, 0)


def _dispatch_call(dest3, pad3, hn, n_rows):
    n = hn.shape[0]
    tm = dest3.shape[2] // TOP_K
    return pl.pallas_call(
        _dispatch_kernel,
        grid=(n // tm,),
        in_specs=[
            pl.BlockSpec((1, 1, tm * TOP_K), lambda i: (i, 0, 0), memory_space=pltpu.SMEM),
            pl.BlockSpec((1, 1, pad3.shape[2]), lambda i: (i, 0, 0), memory_space=pltpu.SMEM),
            pl.BlockSpec((tm, D_MODEL // 2), lambda i: (i, 0)),
        ],
        out_specs=pl.BlockSpec(memory_space=pl.ANY),
        out_shape=jax.ShapeDtypeStruct((n_rows, D_MODEL // 2), jnp.uint32),
        scratch_shapes=[pltpu.VMEM((8, D_MODEL // 2), jnp.uint32), pltpu.SemaphoreType.DMA(()),
                        pltpu.SemaphoreType.DMA(())],
        compiler_params=pltpu.CompilerParams(dimension_semantics=("arbitrary",)),
        name="dispatch",
    )(dest3, pad3, hn)


def _ffn_kernel(te_ref, tv_ref, nt_ref, xs_ref, w1_ref, b1_ref, w2_ref, b2_ref, y_ref):
    i = pl.program_id(0)

    @pl.when(i < nt_ref[0])
    def _():
        tm = xs_ref.shape[0]
        rows = lax.broadcasted_iota(jnp.int32, (tm, 1), 0)
        xt = jnp.where(rows < tv_ref[i], _unpack_bf16_pairs(xs_ref[...]), 0.0).astype(jnp.bfloat16)
        u = _f32dot(xt, w1_ref[0]) + b1_ref[0]
        glu = jnp.minimum(u[:, :D_EXPERT], SWIGLU_LIMIT)
        lin = jnp.clip(u[:, D_EXPERT:], -SWIGLU_LIMIT, SWIGLU_LIMIT)
        a = glu * jax.nn.sigmoid(SWIGLU_ALPHA * glu) * (lin + 1.0)
        y = _f32dot(a.astype(jnp.bfloat16), w2_ref[0]) + b2_ref[0]
        y_ref[...] = _pack_bf16_pairs(y)

    @pl.when(i >= nt_ref[0])
    def _():
        y_ref[...] = jnp.zeros(y_ref.shape, y_ref.dtype)


def _ffn_call(tile_expert, tile_valid, n_tiles, xs, w1, b1, w2, b2):
    n_rows = xs.shape[0]
    tm = FFN_TM
    t_tiles = n_rows // tm

    def tile_map(i, te, tv, nt):
        return (jnp.minimum(i, nt[0] - 1), 0)

    def out_map(i, te, tv, nt):
        return (i, 0)

    def w_map(i, te, tv, nt):
        return (te[i], 0, 0)

    grid_spec = pltpu.PrefetchScalarGridSpec(
        num_scalar_prefetch=3,
        grid=(t_tiles,),
        in_specs=[
            pl.BlockSpec((tm, D_MODEL // 2), tile_map),
            pl.BlockSpec((1, D_MODEL, 2 * D_EXPERT), w_map),
            pl.BlockSpec((1, 1, 2 * D_EXPERT), w_map),
            pl.BlockSpec((1, D_EXPERT, D_MODEL), w_map),
            pl.BlockSpec((1, 1, D_MODEL), w_map),
        ],
        out_specs=pl.BlockSpec((tm, D_MODEL // 2), out_map),
    )
    return pl.pallas_call(
        _ffn_kernel,
        grid_spec=grid_spec,
        out_shape=jax.ShapeDtypeStruct((n_rows, D_MODEL // 2), jnp.uint32),
        compiler_params=pltpu.CompilerParams(dimension_semantics=("arbitrary",), vmem_limit_bytes=V7X_VMEM_LIMIT),
        name="ffn",
    )(tile_expert, tile_valid, n_tiles, xs, w1, b1, w2, b2)


def _combine_kernel(dest_ref, w_ref, h_ref, y_ref, o_ref, gbuf, sem):
    tm = h_ref.shape[0]

    def issue(i, carry):
        for kk in range(TOP_K):
            d = dest_ref[0, 0, i * TOP_K + kk]
            pltpu.make_async_copy(y_ref.at[pl.ds(d, 1)], gbuf.at[kk, pl.ds(i, 1)], sem).start()
        return carry

    lax.fori_loop(0, tm, issue, 0)
    for kk in range(TOP_K):
        pltpu.make_async_copy(y_ref.at[pl.ds(0, tm)], gbuf.at[kk], sem).wait()

    w = w_ref[...]
    out = h_ref[...]
    for kk in range(TOP_K):
        out = out + w[:, kk:kk + 1] * _unpack_bf16_pairs(gbuf[kk])
    o_ref[...] = out


def _combine_call(dest3, w, h, y):
    n = h.shape[0]
    tm = dest3.shape[2] // TOP_K
    return pl.pallas_call(
        _combine_kernel,
        grid=(n // tm,),
        in_specs=[
            pl.BlockSpec((1, 1, tm * TOP_K), lambda i: (i, 0, 0), memory_space=pltpu.SMEM),
            pl.BlockSpec((tm, LANES), lambda i: (i, 0)),
            pl.BlockSpec((tm, D_MODEL), lambda i: (i, 0)),
            pl.BlockSpec(memory_space=pl.ANY),
        ],
        out_specs=pl.BlockSpec((tm, D_MODEL), lambda i: (i, 0)),
        out_shape=jax.ShapeDtypeStruct((n, D_MODEL), jnp.float32),
        scratch_shapes=[pltpu.VMEM((TOP_K, tm, D_MODEL // 2), jnp.uint32), pltpu.SemaphoreType.DMA(())],
        compiler_params=pltpu.CompilerParams(dimension_semantics=("arbitrary",), vmem_limit_bytes=V7X_VMEM_LIMIT),
        name="combine",
    )(dest3, w, h, y)


def _rot_cols(w, group):
    lead = w.shape[:-1]
    g = w.reshape(lead + (-1, group))
    half = group // 2
    return jnp.concatenate([-g[..., half:], g[..., :half]], axis=-1).reshape(w.shape)


def _rope_tables(s, dim):
    inv_freq = 1.0 / (ROPE_THETA ** (jnp.arange(0, dim, 2, dtype=jnp.float32) / dim))
    ang = jnp.arange(s, dtype=jnp.float32)[:, None] * inv_freq[None, :]
    return jnp.cos(ang), jnp.sin(ang)


def _mla_tables(s, gain, scale):
    cos, sin = _rope_tables(s, MLA_ROPE)
    cos2 = jnp.concatenate([cos, cos], axis=-1)
    sin2 = jnp.concatenate([sin, sin], axis=-1)
    g_nope, g_rope = gain[:MLA_NOPE], gain[MLA_NOPE:]
    g_perm = jnp.concatenate([g_rope[MLA_ROPE // 2:], g_rope[:MLA_ROPE // 2]])
    pad = jnp.zeros((s, LANES - MLA_QK), jnp.float32)
    c = jnp.concatenate([jnp.broadcast_to(g_nope, (s, MLA_NOPE)), cos2 * g_rope, pad], axis=-1) * scale
    sn = jnp.concatenate([jnp.zeros((s, MLA_NOPE), jnp.float32), sin2 * g_perm, pad], axis=-1) * scale
    return c, sn


def _diff_tables(s, gain, scale):
    cos, sin = _rope_tables(s, DIFF_D)
    cos2 = jnp.concatenate([cos, cos], axis=-1)
    sin2 = jnp.concatenate([sin, sin], axis=-1)
    g_perm = jnp.concatenate([gain[DIFF_D // 2:], gain[:DIFF_D // 2]])
    c = cos2 * gain * scale
    sn = sin2 * g_perm * scale
    return jnp.concatenate([c, c], axis=-1), jnp.concatenate([sn, sn], axis=-1)


def _pad_heads(w, width):
    kdim = w.shape[0]
    g = w.reshape(kdim, -1, width)
    g = jnp.pad(g, ((0, 0), (0, 0), (0, LANES - width)))
    return g.reshape(kdim, -1)


def kernel(x, attn_norm, w_in, mla_q_lat_norm, mla_kv_lat_norm, mla_w_uq, mla_w_ukv, mla_q_norm, mla_k_norm,
           diff_q_norm, diff_k_norm, diff_lambda_q1, diff_lambda_k1, diff_lambda_q2, diff_lambda_k2, diff_subln,
           w_branch_mla, w_branch_diff, w_out, ffn_norm, w_router, b_router, w_mlp1, b_mlp1, w_mlp2, b_mlp2):
    b, s, d = x.shape
    n = b * s
    bf = jnp.bfloat16
    f32 = jnp.float32
    i = 0

    wi = w_in[i]
    c0 = MLA_Q_LORA + MLA_KV_LORA
    w_kpe = wi[:, c0:c0 + MLA_ROPE]
    c1 = c0 + MLA_ROPE
    w_dq, w_dk, w_dv = wi[:, c1:c1 + 512], wi[:, c1 + 512:c1 + 1024], wi[:, c1 + 1024:c1 + 1536]
    w_gate = wi[:, c1 + 1536:]
    place = lambda w: jnp.pad(w, ((0, 0), (MLA_NOPE, LANES - MLA_QK)))
    w1 = jnp.concatenate([
        wi[:, :c0], place(w_kpe), place(_rot_cols(w_kpe, MLA_ROPE)),
        w_dq, _rot_cols(w_dq, DIFF_D), w_dk, _rot_cols(w_dk, DIFF_D), w_dv, w_gate], axis=1).astype(bf)

    wuq = mla_w_uq[i].reshape(MLA_Q_LORA, MLA_HEADS, MLA_QK)
    wuq_rot = jnp.concatenate(
        [jnp.zeros((MLA_Q_LORA, MLA_HEADS, MLA_NOPE), f32), _rot_cols(wuq[..., MLA_NOPE:], MLA_ROPE)], axis=-1)
    wq = jnp.concatenate([_pad_heads(wuq.reshape(MLA_Q_LORA, -1), MLA_QK),
                          _pad_heads(wuq_rot.reshape(MLA_Q_LORA, -1), MLA_QK)], axis=1).astype(bf)
    wukv = mla_w_ukv[i].reshape(MLA_KV_LORA, MLA_HEADS, MLA_NOPE + MLA_V)
    wkv = jnp.concatenate([_pad_heads(wukv[..., :MLA_NOPE].reshape(MLA_KV_LORA, -1), MLA_NOPE),
                           wukv[..., MLA_NOPE:].reshape(MLA_KV_LORA, -1)], axis=1).astype(bf)

    cq, sq = _mla_tables(s, mla_q_norm[i], LOG2E / math.sqrt(MLA_QK))
    ck, sk = _mla_tables(s, mla_k_norm[i], 1.0)
    cqd, sqd = _diff_tables(s, diff_q_norm[i], LOG2E / math.sqrt(DIFF_D))
    ckd, skd = _diff_tables(s, diff_k_norm[i], 1.0)
    tabs = jnp.stack([cq, sq, ck, sk, cqd, sqd, ckd, skd])

    glat = jnp.concatenate([mla_q_lat_norm[i], mla_kv_lat_norm[i]])[None, :]
    qa, ka, va, qd, kd, vd, sg = _prep_call(x, attn_norm[i][None, :], w1, glat, wq, wkv, tabs)

    lam = jnp.pad(jnp.stack([diff_lambda_q1[i], diff_lambda_k1[i], diff_lambda_q2[i], diff_lambda_k2[i]]),
                  ((0, 0), (0, LANES - DIFF_D)))
    subln = diff_subln[i][None, :]
    o_a = _attn_call(qa, ka, va, lam, subln, mode="mla")
    o_b = _attn_call(qd, kd, vd, lam, subln, mode="diff")

    wr = jnp.pad(w_router[i], ((0, 0), (0, LANES - N_EXPERTS)))
    wr_hi = wr.astype(bf)
    wr_lo = (wr - wr_hi.astype(f32)).astype(bf)
    br = jnp.concatenate([b_router[i], jnp.full((LANES - N_EXPERTS,), NEG_BIG, f32)])[None, :]
    h, hn, idx, gw, pos, cnt = _route_call(
        x.reshape(n, d), o_a.reshape(n, 512), o_b.reshape(n, 512), sg.reshape(n, 2 * d),
        w_branch_mla[i].astype(bf), w_branch_diff[i].astype(bf), w_out[i].astype(bf), ffn_norm[i][None, :],
        wr_hi, wr_lo, br)

    counts = cnt[0, :N_EXPERTS].astype(jnp.int32)
    tiles_per = (counts + FFN_TM - 1) // FFN_TM
    tile_end = jnp.cumsum(tiles_per)
    tile_start = tile_end - tiles_per
    n_tiles = tile_end[-1:]
    t_tiles = n * TOP_K // FFN_TM + N_EXPERTS
    tids = jnp.arange(t_tiles, dtype=jnp.int32)
    tile_expert = jnp.minimum(jnp.sum(tids[:, None] >= tile_end[None, :], axis=1), N_EXPERTS - 1).astype(jnp.int32)
    tile_valid = jnp.clip(counts[tile_expert] - (tids - tile_start[tile_expert]) * FFN_TM, 0, FFN_TM).astype(jnp.int32)
    dest = (tile_start * FFN_TM)[idx[:, :TOP_K]] + pos[:, :TOP_K]
    tm_r = min(ROUTE_TM, n)
    n_steps = n // tm_r
    dest3 = dest.reshape(n_steps, 1, tm_r * TOP_K)

    n_pad = N_EXPERTS * FFN_TM
    pad_len = tiles_per * FFN_TM - counts
    pad_cum = jnp.cumsum(pad_len)
    pad_beg = pad_cum - pad_len
    slot = jnp.arange(n_pad, dtype=jnp.int32)
    pe = jnp.minimum(jnp.sum(slot[:, None] >= pad_cum[None, :], axis=1), N_EXPERTS - 1)
    in_group = tile_start[pe] * FFN_TM + counts[pe] + (slot - pad_beg[pe])
    trailing = n_tiles[0] * FFN_TM + (slot - pad_cum[-1])
    pad_rows = jnp.where(slot < pad_cum[-1], in_group, trailing).astype(jnp.int32)
    pad3 = pad_rows.reshape(n_steps, 1, n_pad // n_steps)

    xs = _dispatch_call(dest3, pad3, hn, t_tiles * FFN_TM)
    y = _ffn_call(tile_expert, tile_valid, n_tiles.astype(jnp.int32), xs,
                  w_mlp1[i].astype(bf), b_mlp1[i][:, None, :], w_mlp2[i].astype(bf), b_mlp2[i][:, None, :])
    out = _combine_call(dest3, gw, h, y)
    return out.reshape(b, s, d)
```

```python
import functools
import math

import jax
import jax.numpy as jnp
import numpy as np
from jax import lax
from jax.experimental import pallas as pl
from jax.experimental.pallas import tpu as pltpu

D_MODEL = 1024
ROPE_THETA = 10000.0
NORM_EPS = 1e-6
MLA_HEADS = 8
MLA_NOPE = 64
MLA_ROPE = 32
MLA_QK = MLA_NOPE + MLA_ROPE
MLA_V = 64
MLA_Q_LORA = 768
MLA_KV_LORA = 256
DIFF_HEADS = 4
DIFF_D = 64
DIFF_V = 128
N_EXPERTS = 32
TOP_K = 4
D_EXPERT = 1024
SWIGLU_ALPHA = 1.702
SWIGLU_LIMIT = 7.0
LAMBDA_INIT = 0.8 - 0.6 * math.exp(-0.3 * 0)

LANES = 128
V7X_VMEM_LIMIT = 56 * 1024 * 1024

PREP_TM = 256
ATT_TQ = 512
ROUTE_TM = 512
DISPATCH_TM = 1024
COMBINE_TM = 256
DMA_UNROLL = 4
FFN_TM = 512

NEG_BIG = -1e30
LOG2E = math.log2(math.e)

_C_LAT = 0
_C_KPE = 1024
_C_DQ = 1280
_C_DK = 2304
_C_DV = 3328
_C_GATE = 3840
_C_END = 5888


def _f32dot(a, b):
    return jnp.dot(a, b, preferred_element_type=jnp.float32)


def _dot_nt(a, b):
    return lax.dot_general(a, b, (((1,), (1,)), ((), ())), preferred_element_type=jnp.float32)


def _rms_scale(x, width):
    return lax.rsqrt(jnp.sum(x * x, axis=-1, keepdims=True) * (1.0 / width) + NORM_EPS)


def _lane_iota(shape):
    return lax.broadcasted_iota(jnp.int32, shape, len(shape) - 1)


def _prep_kernel(x_ref, gattn_ref, w1_ref, glat_ref, wq_ref, wkv_ref, tabs_ref,
                 qa_ref, ka_ref, va_ref, qd_ref, kd_ref, vd_ref, sg_ref):
    x = x_ref[0]
    xn = x * _rms_scale(x, D_MODEL) * gattn_ref[...]
    proj = _f32dot(xn.astype(jnp.bfloat16), w1_ref[...])

    glat = glat_ref[...]
    q_lat = proj[:, 0:MLA_Q_LORA]
    kv_lat = proj[:, MLA_Q_LORA:MLA_Q_LORA + MLA_KV_LORA]
    qn = q_lat * _rms_scale(q_lat, MLA_Q_LORA) * glat[:, 0:MLA_Q_LORA]
    kvn = kv_lat * _rms_scale(kv_lat, MLA_KV_LORA) * glat[:, MLA_Q_LORA:]
    qq = _f32dot(qn.astype(jnp.bfloat16), wq_ref[...])
    kk = _f32dot(kvn.astype(jnp.bfloat16), wkv_ref[...])

    kpe = proj[:, _C_KPE:_C_KPE + LANES]
    kper = proj[:, _C_KPE + LANES:_C_KPE + 2 * LANES]
    cq, sq = tabs_ref[0], tabs_ref[1]
    ck, sk = tabs_ref[2], tabs_ref[3]
    for h in range(MLA_HEADS):
        lo, hi = h * LANES, (h + 1) * LANES
        qh = qq[:, lo:hi]
        qa_ref[0, h] = (_rms_scale(qh, MLA_QK) * (qh * cq + qq[:, 1024 + lo:1024 + hi] * sq)).astype(jnp.bfloat16)
        kh = kk[:, lo:hi] + kpe
        ka_ref[0, h] = (_rms_scale(kh, MLA_QK) * (kh * ck + kper * sk)).astype(jnp.bfloat16)
    for j in range(MLA_HEADS // 2):
        va_ref[0, j, 0] = kk[:, 1024 + j * LANES:1024 + (j + 1) * LANES].T.astype(jnp.bfloat16)

    cqd, sqd = tabs_ref[4], tabs_ref[5]
    ckd, skd = tabs_ref[6], tabs_ref[7]
    lane = _lane_iota((x.shape[0], LANES))
    first = lane < DIFF_D

    def half_norm(v):
        sqv = v * v
        s_all = jnp.sum(sqv, axis=-1, keepdims=True)
        s_lo = jnp.sum(jnp.where(first, sqv, 0.0), axis=-1, keepdims=True)
        r_lo = lax.rsqrt(s_lo * (1.0 / DIFF_D) + NORM_EPS)
        r_hi = lax.rsqrt((s_all - s_lo) * (1.0 / DIFF_D) + NORM_EPS)
        return jnp.where(first, r_lo, r_hi)

    for h in range(DIFF_HEADS):
        lo, hi = h * LANES, (h + 1) * LANES
        dq = proj[:, _C_DQ + lo:_C_DQ + hi]
        dqr = proj[:, _C_DQ + 512 + lo:_C_DQ + 512 + hi]
        qv = half_norm(dq) * (dq * cqd + dqr * sqd)
        qd_ref[0, 2 * h] = jnp.where(first, qv, 0.0).astype(jnp.bfloat16)
        qd_ref[0, 2 * h + 1] = jnp.where(first, 0.0, qv).astype(jnp.bfloat16)
        dk = proj[:, _C_DK + lo:_C_DK + hi]
        dkr = proj[:, _C_DK + 512 + lo:_C_DK + 512 + hi]
        kd_ref[0, h] = (half_norm(dk) * (dk * ckd + dkr * skd)).astype(jnp.bfloat16)
        vd_ref[0, h, 0] = proj[:, _C_DV + lo:_C_DV + hi].T.astype(jnp.bfloat16)

    sg_ref[0] = jax.nn.sigmoid(proj[:, _C_GATE:_C_END]).astype(jnp.bfloat16)


def _att_tiles(s):
    tq = min(ATT_TQ, s)
    return tq, tq


def _const_spec(shape):
    nd = len(shape)
    return pl.BlockSpec(shape, lambda *_: (0,) * nd)


def _prep_call(x, gattn, w1, glat, wq, wkv, tabs):
    b, s, _ = x.shape
    tm = min(PREP_TM, s)
    grid = (b, s // tm)
    head_out = lambda nh: pl.BlockSpec((1, nh, tm, LANES), lambda bi, ti: (bi, 0, ti, 0))
    _, tk = _att_tiles(s)
    per = tk // tm
    vt_out = lambda nh: pl.BlockSpec((1, nh, 1, LANES, tm), lambda bi, ti: (bi, 0, ti // per, 0, ti % per))
    bf = jnp.bfloat16
    out_shape = (
        jax.ShapeDtypeStruct((b, MLA_HEADS, s, LANES), bf),
        jax.ShapeDtypeStruct((b, MLA_HEADS, s, LANES), bf),
        jax.ShapeDtypeStruct((b, MLA_HEADS // 2, s // tk, LANES, tk), bf),
        jax.ShapeDtypeStruct((b, 2 * DIFF_HEADS, s, LANES), bf),
        jax.ShapeDtypeStruct((b, DIFF_HEADS, s, LANES), bf),
        jax.ShapeDtypeStruct((b, DIFF_HEADS, s // tk, LANES, tk), bf),
        jax.ShapeDtypeStruct((b, s, 2 * D_MODEL), bf),
    )
    return pl.pallas_call(
        _prep_kernel,
        grid=grid,
        in_specs=[
            pl.BlockSpec((1, tm, D_MODEL), lambda bi, ti: (bi, ti, 0)),
            _const_spec(gattn.shape),
            _const_spec(w1.shape),
            _const_spec(glat.shape),
            _const_spec(wq.shape),
            _const_spec(wkv.shape),
            pl.BlockSpec((8, tm, LANES), lambda bi, ti: (0, ti, 0)),
        ],
        out_specs=(
            head_out(MLA_HEADS), head_out(MLA_HEADS), vt_out(MLA_HEADS // 2),
            head_out(2 * DIFF_HEADS), head_out(DIFF_HEADS), vt_out(DIFF_HEADS),
            pl.BlockSpec((1, tm, 2 * D_MODEL), lambda bi, ti: (bi, ti, 0)),
        ),
        out_shape=out_shape,
        compiler_params=pltpu.CompilerParams(
            dimension_semantics=("arbitrary", "arbitrary"), vmem_limit_bytes=V7X_VMEM_LIMIT),
        name="prep",
    )(x, gattn, w1, glat, wq, wkv, tabs)


def _attn_kernel(q_ref, k_ref, vt_ref, lam_ref, subln_ref, o_ref, st_buf, mx_buf, m_scr, l_scr, acc_scr, *, mode, tq):
    qi = pl.program_id(2)
    shared_k = k_ref.shape[1] == 1
    m_scr[...] = jnp.full(m_scr.shape, NEG_BIG, jnp.float32)
    l_scr[...] = jnp.zeros(l_scr.shape, jnp.float32)
    acc_scr[...] = jnp.zeros(acc_scr.shape, jnp.float32)

    def scores(kb, slot, masked):
        k0 = pl.multiple_of(kb * tq, tq)
        for hh in range(2):
            k = k_ref[0, 0 if shared_k else hh, pl.ds(k0, tq), :]
            st = _dot_nt(k, q_ref[0, hh])
            if masked:
                key = lax.broadcasted_iota(jnp.int32, st.shape, 0)
                qry = lax.broadcasted_iota(jnp.int32, st.shape, 1)
                st = jnp.where(key <= qry, st, NEG_BIG)
            st_buf[slot, hh] = st
            mx_buf[slot, hh] = jnp.max(st, axis=0, keepdims=True)

    def softmax_pv(kb, slot):
        vt = vt_ref[0, 0, kb]
        for hh in range(2):
            m_prev = m_scr[hh]
            m_new = jnp.maximum(m_prev, mx_buf[slot, hh])
            alpha = jnp.exp2(m_prev - m_new)
            p = jnp.exp2(st_buf[slot, hh] - m_new)
            l_scr[hh] = alpha * l_scr[hh] + jnp.sum(p, axis=0, keepdims=True)
            acc_scr[hh] = alpha * acc_scr[hh] + _f32dot(vt, p.astype(jnp.bfloat16))
            m_scr[hh] = m_new

    scores(qi, 0, True)

    def body(j, carry):
        scores(2 * j, 1, False)
        softmax_pv(jnp.where(j == 0, qi, 2 * j - 1), 0)
        scores(2 * j + 1, 0, False)
        softmax_pv(2 * j, 1)
        return carry

    lax.fori_loop(0, qi // 2, body, 0)
    last = qi - 1

    @pl.when(qi % 2 == 1)
    def _():
        scores(last, 1, False)
        softmax_pv(jnp.where(qi == 1, qi, last - 1), 0)
        softmax_pv(last, 1)

    @pl.when(qi % 2 == 0)
    def _():
        softmax_pv(jnp.where(qi == 0, qi, last), 0)

    oa = acc_scr[0] / l_scr[0]
    ob = acc_scr[1] / l_scr[1]
    if mode == "mla":
        sub = lax.broadcasted_iota(jnp.int32, oa.shape, 0)
        o_ref[0] = jnp.where(sub < MLA_V, oa, ob).T.astype(o_ref.dtype)
    else:
        lp = lam_ref[...]
        lam = (jnp.exp(jnp.sum(lp[0:1] * lp[1:2], axis=-1, keepdims=True))
               - jnp.exp(jnp.sum(lp[2:3] * lp[3:4], axis=-1, keepdims=True)) + LAMBDA_INIT)
        o = oa - lam * ob
        o = o * lax.rsqrt(jnp.sum(o * o, axis=0, keepdims=True) * (1.0 / DIFF_V) + NORM_EPS)
        o_ref[0] = (o.T * subln_ref[...] * (1.0 - LAMBDA_INIT)).astype(o_ref.dtype)


def _attn_call(q, k, vt, lam, subln, *, mode):
    b, _, s, _ = q.shape
    n_groups = vt.shape[1]
    kh = k.shape[1] // n_groups
    tq, tk = _att_tiles(s)
    grid = (b, n_groups, s // tq)
    assert tk == tq
    kernel = functools.partial(_attn_kernel, mode=mode, tq=tq)
    return pl.pallas_call(
        kernel,
        grid=grid,
        in_specs=[
            pl.BlockSpec((1, 2, tq, LANES), lambda bi, g, qi: (bi, g, qi, 0)),
            pl.BlockSpec((1, kh, s, LANES), lambda bi, g, qi: (bi, g, 0, 0)),
            pl.BlockSpec((1, 1, s // tk, LANES, tk), lambda bi, g, qi: (bi, g, 0, 0, 0)),
            _const_spec(lam.shape),
            _const_spec(subln.shape),
        ],
        out_specs=pl.BlockSpec((1, tq, LANES), lambda bi, g, qi: (bi, qi, g)),
        out_shape=jax.ShapeDtypeStruct((b, s, n_groups * LANES), jnp.bfloat16),
        scratch_shapes=[
            pltpu.VMEM((2, 2, tq, tq), jnp.float32),
            pltpu.VMEM((2, 2, 1, tq), jnp.float32),
            pltpu.VMEM((2, 1, tq), jnp.float32),
            pltpu.VMEM((2, 1, tq), jnp.float32),
            pltpu.VMEM((2, LANES, tq), jnp.float32),
        ],
        compiler_params=pltpu.CompilerParams(
            dimension_semantics=("arbitrary", "arbitrary", "arbitrary"), vmem_limit_bytes=V7X_VMEM_LIMIT),
        name="attn_" + mode,
    )(q, k, vt, lam, subln)


def _pack_bf16_pairs(v):
    n = v.shape[1] // 2
    hi = pltpu.bitcast(v[:, :n].astype(jnp.bfloat16).astype(jnp.float32), jnp.uint32)
    lo = pltpu.bitcast(v[:, n:].astype(jnp.bfloat16).astype(jnp.float32), jnp.uint32)
    return hi | (lo >> 16)


def _unpack_bf16_pairs(u):
    hi = pltpu.bitcast(u & jnp.uint32(0xFFFF0000), jnp.float32)
    lo = pltpu.bitcast(u << 16, jnp.float32)
    return jnp.concatenate([hi, lo], axis=1)


def _route_kernel(x_ref, oa_ref, ob_ref, sg_ref, wbm_ref, wbd_ref, wout_ref, gffn_ref, wrh_ref, wrl_ref, br_ref,
                  h_ref, hn_ref, idx_ref, w_ref, pos_ref, cnt_ref, carry_scr):
    @pl.when(pl.program_id(0) == 0)
    def _():
        carry_scr[...] = jnp.zeros(carry_scr.shape, jnp.float32)

    sg = sg_ref[...]
    ma = _f32dot(oa_ref[...], wbm_ref[...])
    mb = _f32dot(ob_ref[...], wbd_ref[...])
    merged = sg[:, :D_MODEL] * ma + sg[:, D_MODEL:] * mb
    h = x_ref[...] + _f32dot(merged.astype(jnp.bfloat16), wout_ref[...])
    h_ref[...] = h

    hn = h * _rms_scale(h, D_MODEL) * gffn_ref[...]
    hn_ref[...] = _pack_bf16_pairs(hn)
    hn_hi = hn.astype(jnp.bfloat16)
    hn_lo = (hn - hn_hi.astype(jnp.float32)).astype(jnp.bfloat16)
    wrh = wrh_ref[...]
    logits = _f32dot(hn_hi, wrh) + _f32dot(hn_lo, wrh) + _f32dot(hn_hi, wrl_ref[...]) + br_ref[...]

    tm = logits.shape[0]
    lane = _lane_iota(logits.shape)
    work = logits
    vals, onehots = [], []
    for _k in range(TOP_K):
        mx = jnp.max(work, axis=-1, keepdims=True)
        sel = jnp.min(jnp.where(work == mx, lane, LANES), axis=-1, keepdims=True)
        oh = lane == sel
        vals.append(mx)
        onehots.append(oh)
        work = jnp.where(oh, NEG_BIG * 2, work)
    exps = [jnp.exp(vv - vals[0]) for vv in vals]
    denom = exps[0] + exps[1] + exps[2] + exps[3]

    chosen = jnp.zeros(logits.shape, jnp.float32)
    for oh in onehots:
        chosen = chosen + oh.astype(jnp.float32)
    r_i = lax.broadcasted_iota(jnp.int32, (tm, tm), 0)
    c_i = lax.broadcasted_iota(jnp.int32, (tm, tm), 1)
    ltri = (c_i < r_i).astype(jnp.bfloat16)
    before = _f32dot(ltri, chosen.astype(jnp.bfloat16)) + carry_scr[...]
    carry_scr[...] = carry_scr[...] + jnp.sum(chosen, axis=0, keepdims=True)
    cnt_ref[...] = carry_scr[...]

    idx_out = jnp.zeros(logits.shape, jnp.int32)
    w_out = jnp.zeros(logits.shape, jnp.float32)
    pos_out = jnp.zeros(logits.shape, jnp.int32)
    for kk in range(TOP_K):
        oh = onehots[kk]
        e_k = jnp.sum(jnp.where(oh, lane, 0), axis=-1, keepdims=True)
        p_k = jnp.sum(jnp.where(oh, before, 0.0), axis=-1, keepdims=True).astype(jnp.int32)
        idx_out = jnp.where(lane == kk, e_k, idx_out)
        pos_out = jnp.where(lane == kk, p_k, pos_out)
        w_out = jnp.where(lane == kk, exps[kk] / denom, w_out)
    idx_ref[...] = idx_out
    w_ref[...] = w_out
    pos_ref[...] = pos_out


def _route_call(x2, oa, ob, sg, wbm, wbd, wout, gffn, wrh, wrl, br):
    n = x2.shape[0]
    tm = min(ROUTE_TM, n)
    row = lambda w: pl.BlockSpec((tm, w), lambda i: (i, 0))
    out_shape = (
        jax.ShapeDtypeStruct((n, D_MODEL), jnp.float32),
        jax.ShapeDtypeStruct((n, D_MODEL // 2), jnp.uint32),
        jax.ShapeDtypeStruct((n, LANES), jnp.int32),
        jax.ShapeDtypeStruct((n, LANES), jnp.float32),
        jax.ShapeDtypeStruct((n, LANES), jnp.int32),
        jax.ShapeDtypeStruct((1, LANES), jnp.float32),
    )
    return pl.pallas_call(
        _route_kernel,
        grid=(n // tm,),
        in_specs=[row(D_MODEL), row(512), row(512), row(2 * D_MODEL),
                  _const_spec(wbm.shape), _const_spec(wbd.shape), _const_spec(wout.shape), _const_spec(gffn.shape),
                  _const_spec(wrh.shape), _const_spec(wrl.shape), _const_spec(br.shape)],
        out_specs=(row(D_MODEL), row(D_MODEL // 2), row(LANES), row(LANES), row(LANES), _const_spec((1, LANES))),
        out_shape=out_shape,
        scratch_shapes=[pltpu.VMEM((1, LANES), jnp.float32)],
        compiler_params=pltpu.CompilerParams(dimension_semantics=("arbitrary",), vmem_limit_bytes=V7X_VMEM_LIMIT),
        name="route",
    )(x2, oa, ob, sg, wbm, wbd, wout, gffn, wrh, wrl, br)


def _dispatch_kernel(dest_ref, pad_ref, hn_ref, xs_ref, zrow, sem, zsem):
    tm = hn_ref.shape[0]
    n_pad = pad_ref.shape[2]

    @pl.when(pl.program_id(0) == 0)
    def _():
        zrow[...] = jnp.zeros(zrow.shape, zrow.dtype)

    def issue(i, carry):
        for kk in range(TOP_K):
            d = dest_ref[0, 0, i * TOP_K + kk]
            pltpu.make_async_copy(hn_ref.at[pl.ds(i, 1)], xs_ref.at[pl.ds(d, 1)], sem).start()
        return carry

    def issue_pad(j, carry):
        pltpu.make_async_copy(zrow.at[pl.ds(0, 1)], xs_ref.at[pl.ds(pad_ref[0, 0, j], 1)], zsem).start()
        return carry

    lax.fori_loop(0, tm, issue, 0, unroll=DMA_UNROLL)
    lax.fori_loop(0, n_pad, issue_pad, 0, unroll=DMA_UNROLL)
    for _k in range(TOP_K):
        pltpu.make_async_copy(hn_ref, xs_ref.at[pl.ds(0, tm)], sem).wait()

    def wait_pad(j, carry):
        pltpu.make_async_copy(zrow.at[pl.ds(0, 1)], xs_ref.at[pl.ds(0, 1)], zsem).wait()
        return carry

    lax.fori_loop(0, n_pad, wait_pad, 0, unroll=DMA_UNROLL)


def _dispatch_call(dest3, pad3, hn, n_rows):
    n = hn.shape[0]
    n_steps = dest3.shape[0]
    tm = n // n_steps
    return pl.pallas_call(
        _dispatch_kernel,
        grid=(n_steps,),
        in_specs=[
            pl.BlockSpec((1, 1, dest3.shape[2]), lambda i: (i, 0, 0), memory_space=pltpu.SMEM),
            pl.BlockSpec((1, 1, pad3.shape[2]), lambda i: (i, 0, 0), memory_space=pltpu.SMEM),
            pl.BlockSpec((tm, D_MODEL // 2), lambda i: (i, 0)),
        ],
        out_specs=pl.BlockSpec(memory_space=pl.ANY),
        out_shape=jax.ShapeDtypeStruct((n_rows, D_MODEL // 2), jnp.uint32),
        scratch_shapes=[pltpu.VMEM((8, D_MODEL // 2), jnp.uint32), pltpu.SemaphoreType.DMA(()),
                        pltpu.SemaphoreType.DMA(())],
        compiler_params=pltpu.CompilerParams(dimension_semantics=("arbitrary",)),
        name="dispatch",
    )(dest3, pad3, hn)


def _ffn_kernel(te_ref, tv_ref, nt_ref, xs_ref, w1_ref, b1_ref, w2_ref, b2_ref, y_ref):
    i = pl.program_id(0)

    @pl.when(i < nt_ref[0])
    def _():
        tm = xs_ref.shape[0]
        rows = lax.broadcasted_iota(jnp.int32, (tm, 1), 0)
        xt = jnp.where(rows < tv_ref[i], _unpack_bf16_pairs(xs_ref[...]), 0.0).astype(jnp.bfloat16)
        u = _f32dot(xt, w1_ref[0]) + b1_ref[0]
        glu = jnp.minimum(u[:, :D_EXPERT], SWIGLU_LIMIT)
        lin = jnp.clip(u[:, D_EXPERT:], -SWIGLU_LIMIT, SWIGLU_LIMIT)
        a = glu * jax.nn.sigmoid(SWIGLU_ALPHA * glu) * (lin + 1.0)
        y = _f32dot(a.astype(jnp.bfloat16), w2_ref[0]) + b2_ref[0]
        y_ref[...] = _pack_bf16_pairs(y)

    @pl.when(i >= nt_ref[0])
    def _():
        y_ref[...] = jnp.zeros(y_ref.shape, y_ref.dtype)


def _ffn_call(tile_expert, tile_valid, n_tiles, xs, w1, b1, w2, b2):
    n_rows = xs.shape[0]
    tm = FFN_TM
    t_tiles = n_rows // tm

    def tile_map(i, te, tv, nt):
        return (jnp.minimum(i, nt[0] - 1), 0)

    def out_map(i, te, tv, nt):
        return (i, 0)

    def w_map(i, te, tv, nt):
        return (te[i], 0, 0)

    grid_spec = pltpu.PrefetchScalarGridSpec(
        num_scalar_prefetch=3,
        grid=(t_tiles,),
        in_specs=[
            pl.BlockSpec((tm, D_MODEL // 2), tile_map),
            pl.BlockSpec((1, D_MODEL, 2 * D_EXPERT), w_map),
            pl.BlockSpec((1, 1, 2 * D_EXPERT), w_map),
            pl.BlockSpec((1, D_EXPERT, D_MODEL), w_map),
            pl.BlockSpec((1, 1, D_MODEL), w_map),
        ],
        out_specs=pl.BlockSpec((tm, D_MODEL // 2), out_map),
    )
    return pl.pallas_call(
        _ffn_kernel,
        grid_spec=grid_spec,
        out_shape=jax.ShapeDtypeStruct((n_rows, D_MODEL // 2), jnp.uint32),
        compiler_params=pltpu.CompilerParams(dimension_semantics=("arbitrary",), vmem_limit_bytes=V7X_VMEM_LIMIT),
        name="ffn",
    )(tile_expert, tile_valid, n_tiles, xs, w1, b1, w2, b2)


def _combine_kernel(dest_ref, dnext_ref, w_ref, h_ref, y_ref, o_ref, gbuf, sems):
    step = pl.program_id(0)
    last = pl.num_programs(0) - 1
    tm = h_ref.shape[0]
    par = step % 2

    def gather(d_ref, slot):
        def issue(i, carry):
            for kk in range(TOP_K):
                d = d_ref[0, 0, i * TOP_K + kk]
                pltpu.make_async_copy(y_ref.at[pl.ds(d, 1)], gbuf.at[slot, kk, pl.ds(i, 1)], sems.at[slot]).start()
            return carry

        lax.fori_loop(0, tm, issue, 0, unroll=DMA_UNROLL)

    @pl.when(step == 0)
    def _():
        gather(dest_ref, 0)

    @pl.when(step < last)
    def _():
        gather(dnext_ref, 1 - par)

    for kk in range(TOP_K):
        pltpu.make_async_copy(y_ref.at[pl.ds(0, tm)], gbuf.at[par, kk], sems.at[par]).wait()

    w = w_ref[...]
    out = h_ref[...]
    for kk in range(TOP_K):
        out = out + w[:, kk:kk + 1] * _unpack_bf16_pairs(gbuf[par, kk])
    o_ref[...] = out


def _combine_call(dest3, w, h, y):
    n = h.shape[0]
    n_steps = dest3.shape[0]
    tm = n // n_steps
    dspec = lambda f: pl.BlockSpec((1, 1, dest3.shape[2]), f, memory_space=pltpu.SMEM)
    return pl.pallas_call(
        _combine_kernel,
        grid=(n_steps,),
        in_specs=[
            dspec(lambda i: (i, 0, 0)),
            dspec(lambda i: (jnp.minimum(i + 1, n_steps - 1), 0, 0)),
            pl.BlockSpec((tm, LANES), lambda i: (i, 0)),
            pl.BlockSpec((tm, D_MODEL), lambda i: (i, 0)),
            pl.BlockSpec(memory_space=pl.ANY),
        ],
        out_specs=pl.BlockSpec((tm, D_MODEL), lambda i: (i, 0)),
        out_shape=jax.ShapeDtypeStruct((n, D_MODEL), jnp.float32),
        scratch_shapes=[pltpu.VMEM((2, TOP_K, tm, D_MODEL // 2), jnp.uint32), pltpu.SemaphoreType.DMA((2,))],
        compiler_params=pltpu.CompilerParams(dimension_semantics=("arbitrary",), vmem_limit_bytes=V7X_VMEM_LIMIT),
        name="combine",
    )(dest3, dest3, w, h, y)


def _rot_cols(w, group):
    lead = w.shape[:-1]
    g = w.reshape(lead + (-1, group))
    half = group // 2
    return jnp.concatenate([-g[..., half:], g[..., :half]], axis=-1).reshape(w.shape)


def _rope_tables(s, dim):
    inv_freq = 1.0 / (ROPE_THETA ** (jnp.arange(0, dim, 2, dtype=jnp.float32) / dim))
    ang = jnp.arange(s, dtype=jnp.float32)[:, None] * inv_freq[None, :]
    return jnp.cos(ang), jnp.sin(ang)


def _mla_tables(s, gain, scale):
    cos, sin = _rope_tables(s, MLA_ROPE)
    cos2 = jnp.concatenate([cos, cos], axis=-1)
    sin2 = jnp.concatenate([sin, sin], axis=-1)
    g_nope, g_rope = gain[:MLA_NOPE], gain[MLA_NOPE:]
    g_perm = jnp.concatenate([g_rope[MLA_ROPE // 2:], g_rope[:MLA_ROPE // 2]])
    pad = jnp.zeros((s, LANES - MLA_QK), jnp.float32)
    c = jnp.concatenate([jnp.broadcast_to(g_nope, (s, MLA_NOPE)), cos2 * g_rope, pad], axis=-1) * scale
    sn = jnp.concatenate([jnp.zeros((s, MLA_NOPE), jnp.float32), sin2 * g_perm, pad], axis=-1) * scale
    return c, sn


def _diff_tables(s, gain, scale):
    cos, sin = _rope_tables(s, DIFF_D)
    cos2 = jnp.concatenate([cos, cos], axis=-1)
    sin2 = jnp.concatenate([sin, sin], axis=-1)
    g_perm = jnp.concatenate([gain[DIFF_D // 2:], gain[:DIFF_D // 2]])
    c = cos2 * gain * scale
    sn = sin2 * g_perm * scale
    return jnp.concatenate([c, c], axis=-1), jnp.concatenate([sn, sn], axis=-1)


def _pad_heads(w, width):
    kdim = w.shape[0]
    g = w.reshape(kdim, -1, width)
    g = jnp.pad(g, ((0, 0), (0, 0), (0, LANES - width)))
    return g.reshape(kdim, -1)


def kernel(x, attn_norm, w_in, mla_q_lat_norm, mla_kv_lat_norm, mla_w_uq, mla_w_ukv, mla_q_norm, mla_k_norm,
           diff_q_norm, diff_k_norm, diff_lambda_q1, diff_lambda_k1, diff_lambda_q2, diff_lambda_k2, diff_subln,
           w_branch_mla, w_branch_diff, w_out, ffn_norm, w_router, b_router, w_mlp1, b_mlp1, w_mlp2, b_mlp2):
    b, s, d = x.shape
    n = b * s
    bf = jnp.bfloat16
    f32 = jnp.float32
    i = 0

    wi = w_in[i]
    c0 = MLA_Q_LORA + MLA_KV_LORA
    w_kpe = wi[:, c0:c0 + MLA_ROPE]
    c1 = c0 + MLA_ROPE
    w_dq, w_dk, w_dv = wi[:, c1:c1 + 512], wi[:, c1 + 512:c1 + 1024], wi[:, c1 + 1024:c1 + 1536]
    w_gate = wi[:, c1 + 1536:]
    place = lambda w: jnp.pad(w, ((0, 0), (MLA_NOPE, LANES - MLA_QK)))
    w1 = jnp.concatenate([
        wi[:, :c0], place(w_kpe), place(_rot_cols(w_kpe, MLA_ROPE)),
        w_dq, _rot_cols(w_dq, DIFF_D), w_dk, _rot_cols(w_dk, DIFF_D), w_dv, w_gate], axis=1).astype(bf)

    wuq = mla_w_uq[i].reshape(MLA_Q_LORA, MLA_HEADS, MLA_QK)
    wuq_rot = jnp.concatenate(
        [jnp.zeros((MLA_Q_LORA, MLA_HEADS, MLA_NOPE), f32), _rot_cols(wuq[..., MLA_NOPE:], MLA_ROPE)], axis=-1)
    wq = jnp.concatenate([_pad_heads(wuq.reshape(MLA_Q_LORA, -1), MLA_QK),
                          _pad_heads(wuq_rot.reshape(MLA_Q_LORA, -1), MLA_QK)], axis=1).astype(bf)
    wukv = mla_w_ukv[i].reshape(MLA_KV_LORA, MLA_HEADS, MLA_NOPE + MLA_V)
    wkv = jnp.concatenate([_pad_heads(wukv[..., :MLA_NOPE].reshape(MLA_KV_LORA, -1), MLA_NOPE),
                           wukv[..., MLA_NOPE:].reshape(MLA_KV_LORA, -1)], axis=1).astype(bf)

    cq, sq = _mla_tables(s, mla_q_norm[i], LOG2E / math.sqrt(MLA_QK))
    ck, sk = _mla_tables(s, mla_k_norm[i], 1.0)
    cqd, sqd = _diff_tables(s, diff_q_norm[i], LOG2E / math.sqrt(DIFF_D))
    ckd, skd = _diff_tables(s, diff_k_norm[i], 1.0)
    tabs = jnp.stack([cq, sq, ck, sk, cqd, sqd, ckd, skd])

    glat = jnp.concatenate([mla_q_lat_norm[i], mla_kv_lat_norm[i]])[None, :]
    qa, ka, va, qd, kd, vd, sg = _prep_call(x, attn_norm[i][None, :], w1, glat, wq, wkv, tabs)

    lam = jnp.pad(jnp.stack([diff_lambda_q1[i], diff_lambda_k1[i], diff_lambda_q2[i], diff_lambda_k2[i]]),
                  ((0, 0), (0, LANES - DIFF_D)))
    subln = diff_subln[i][None, :]
    o_a = _attn_call(qa, ka, va, lam, subln, mode="mla")
    o_b = _attn_call(qd, kd, vd, lam, subln, mode="diff")

    wr = jnp.pad(w_router[i], ((0, 0), (0, LANES - N_EXPERTS)))
    wr_hi = wr.astype(bf)
    wr_lo = (wr - wr_hi.astype(f32)).astype(bf)
    br = jnp.concatenate([b_router[i], jnp.full((LANES - N_EXPERTS,), NEG_BIG, f32)])[None, :]
    h, hn, idx, gw, pos, cnt = _route_call(
        x.reshape(n, d), o_a.reshape(n, 512), o_b.reshape(n, 512), sg.reshape(n, 2 * d),
        w_branch_mla[i].astype(bf), w_branch_diff[i].astype(bf), w_out[i].astype(bf), ffn_norm[i][None, :],
        wr_hi, wr_lo, br)

    counts = cnt[0, :N_EXPERTS].astype(jnp.int32)
    tiles_per = (counts + FFN_TM - 1) // FFN_TM
    tile_end = jnp.cumsum(tiles_per)
    tile_start = tile_end - tiles_per
    n_tiles = tile_end[-1:]
    t_tiles = n * TOP_K // FFN_TM + N_EXPERTS
    tids = jnp.arange(t_tiles, dtype=jnp.int32)
    tile_expert = jnp.minimum(jnp.sum(tids[:, None] >= tile_end[None, :], axis=1), N_EXPERTS - 1).astype(jnp.int32)
    tile_valid = jnp.clip(counts[tile_expert] - (tids - tile_start[tile_expert]) * FFN_TM, 0, FFN_TM).astype(jnp.int32)
    dest = (tile_start * FFN_TM)[idx[:, :TOP_K]] + pos[:, :TOP_K]
    n_steps = n // min(DISPATCH_TM, n)
    dest_d = dest.reshape(n_steps, 1, -1)
    dest_c = dest.reshape(n // min(COMBINE_TM, n), 1, -1)

    n_pad = N_EXPERTS * FFN_TM
    pad_len = tiles_per * FFN_TM - counts
    pad_cum = jnp.cumsum(pad_len)
    pad_beg = pad_cum - pad_len
    slot = jnp.arange(n_pad, dtype=jnp.int32)
    pe = jnp.minimum(jnp.sum(slot[:, None] >= pad_cum[None, :], axis=1), N_EXPERTS - 1)
    in_group = tile_start[pe] * FFN_TM + counts[pe] + (slot - pad_beg[pe])
    trailing = n_tiles[0] * FFN_TM + (slot - pad_cum[-1])
    pad_rows = jnp.where(slot < pad_cum[-1], in_group, trailing).astype(jnp.int32)
    pad3 = pad_rows.reshape(n_steps, 1, n_pad // n_steps)

    xs = _dispatch_call(dest_d, pad3, hn, t_tiles * FFN_TM)
    y = _ffn_call(tile_expert, tile_valid, n_tiles.astype(jnp.int32), xs,
                  w_mlp1[i].astype(bf), b_mlp1[i][:, None, :], w_mlp2[i].astype(bf), b_mlp2[i][:, None, :])
    out = _combine_call(dest_c, gw, h, y)
    return out.reshape(b, s, d)
```

```python
import functools
import math

import jax
import jax.numpy as jnp
import numpy as np
from jax import lax
from jax.experimental import pallas as pl
from jax.experimental.pallas import tpu as pltpu

D_MODEL = 1024
ROPE_THETA = 10000.0
NORM_EPS = 1e-6
MLA_HEADS = 8
MLA_NOPE = 64
MLA_ROPE = 32
MLA_QK = MLA_NOPE + MLA_ROPE
MLA_V = 64
MLA_Q_LORA = 768
MLA_KV_LORA = 256
DIFF_HEADS = 4
DIFF_D = 64
DIFF_V = 128
N_EXPERTS = 32
TOP_K = 4
D_EXPERT = 1024
SWIGLU_ALPHA = 1.702
SWIGLU_LIMIT = 7.0
LAMBDA_INIT = 0.8 - 0.6 * math.exp(-0.3 * 0)

LANES = 128
V7X_VMEM_LIMIT = 56 * 1024 * 1024

PREP_TM = 256
ATT_TQ = 512
ROUTE_TM = 512
TOK_ROWS = 8
TOK_UNROLL = 8
FFN_TM = 512

NEG_BIG = -1e30
LOG2E = math.log2(math.e)

_C_LAT = 0
_C_KPE = 1024
_C_DQ = 1280
_C_DK = 2304
_C_DV = 3328
_C_GATE = 3840
_C_END = 5888


def _f32dot(a, b):
    return jnp.dot(a, b, preferred_element_type=jnp.float32)


def _dot_nt(a, b):
    return lax.dot_general(a, b, (((1,), (1,)), ((), ())), preferred_element_type=jnp.float32)


def _rms_scale(x, width):
    return lax.rsqrt(jnp.sum(x * x, axis=-1, keepdims=True) * (1.0 / width) + NORM_EPS)


def _lane_iota(shape):
    return lax.broadcasted_iota(jnp.int32, shape, len(shape) - 1)


def _prep_kernel(x_ref, gattn_ref, w1_ref, glat_ref, wq_ref, wkv_ref, tabs_ref,
                 qa_ref, ka_ref, va_ref, qd_ref, kd_ref, vd_ref, sg_ref):
    x = x_ref[0]
    xn = x * _rms_scale(x, D_MODEL) * gattn_ref[...]
    proj = _f32dot(xn.astype(jnp.bfloat16), w1_ref[...])

    glat = glat_ref[...]
    q_lat = proj[:, 0:MLA_Q_LORA]
    kv_lat = proj[:, MLA_Q_LORA:MLA_Q_LORA + MLA_KV_LORA]
    qn = q_lat * _rms_scale(q_lat, MLA_Q_LORA) * glat[:, 0:MLA_Q_LORA]
    kvn = kv_lat * _rms_scale(kv_lat, MLA_KV_LORA) * glat[:, MLA_Q_LORA:]
    qq = _f32dot(qn.astype(jnp.bfloat16), wq_ref[...])
    kk = _f32dot(kvn.astype(jnp.bfloat16), wkv_ref[...])

    kpe = proj[:, _C_KPE:_C_KPE + LANES]
    kper = proj[:, _C_KPE + LANES:_C_KPE + 2 * LANES]
    cq, sq = tabs_ref[0], tabs_ref[1]
    ck, sk = tabs_ref[2], tabs_ref[3]
    for h in range(MLA_HEADS):
        lo, hi = h * LANES, (h + 1) * LANES
        qh = qq[:, lo:hi]
        qa_ref[0, h] = (_rms_scale(qh, MLA_QK) * (qh * cq + qq[:, 1024 + lo:1024 + hi] * sq)).astype(jnp.bfloat16)
        kh = kk[:, lo:hi] + kpe
        ka_ref[0, h] = (_rms_scale(kh, MLA_QK) * (kh * ck + kper * sk)).astype(jnp.bfloat16)
    for j in range(MLA_HEADS // 2):
        va_ref[0, j, 0] = kk[:, 1024 + j * LANES:1024 + (j + 1) * LANES].T.astype(jnp.bfloat16)

    cqd, sqd = tabs_ref[4], tabs_ref[5]
    ckd, skd = tabs_ref[6], tabs_ref[7]
    lane = _lane_iota((x.shape[0], LANES))
    first = lane < DIFF_D

    def half_norm(v):
        sqv = v * v
        s_all = jnp.sum(sqv, axis=-1, keepdims=True)
        s_lo = jnp.sum(jnp.where(first, sqv, 0.0), axis=-1, keepdims=True)
        r_lo = lax.rsqrt(s_lo * (1.0 / DIFF_D) + NORM_EPS)
        r_hi = lax.rsqrt((s_all - s_lo) * (1.0 / DIFF_D) + NORM_EPS)
        return jnp.where(first, r_lo, r_hi)

    for h in range(DIFF_HEADS):
        lo, hi = h * LANES, (h + 1) * LANES
        dq = proj[:, _C_DQ + lo:_C_DQ + hi]
        dqr = proj[:, _C_DQ + 512 + lo:_C_DQ + 512 + hi]
        qv = half_norm(dq) * (dq * cqd + dqr * sqd)
        qd_ref[0, 2 * h] = jnp.where(first, qv, 0.0).astype(jnp.bfloat16)
        qd_ref[0, 2 * h + 1] = jnp.where(first, 0.0, qv).astype(jnp.bfloat16)
        dk = proj[:, _C_DK + lo:_C_DK + hi]
        dkr = proj[:, _C_DK + 512 + lo:_C_DK + 512 + hi]
        kd_ref[0, h] = (half_norm(dk) * (dk * ckd + dkr * skd)).astype(jnp.bfloat16)
        vd_ref[0, h, 0] = proj[:, _C_DV + lo:_C_DV + hi].T.astype(jnp.bfloat16)

    sg_ref[0] = jax.nn.sigmoid(proj[:, _C_GATE:_C_END]).astype(jnp.bfloat16)


def _att_tiles(s):
    tq = min(ATT_TQ, s)
    return tq, tq


def _const_spec(shape):
    nd = len(shape)
    return pl.BlockSpec(shape, lambda *_: (0,) * nd)


def _prep_call(x, gattn, w1, glat, wq, wkv, tabs):
    b, s, _ = x.shape
    tm = min(PREP_TM, s)
    grid = (b, s // tm)
    head_out = lambda nh: pl.BlockSpec((1, nh, tm, LANES), lambda bi, ti: (bi, 0, ti, 0))
    _, tk = _att_tiles(s)
    per = tk // tm
    vt_out = lambda nh: pl.BlockSpec((1, nh, 1, LANES, tm), lambda bi, ti: (bi, 0, ti // per, 0, ti % per))
    bf = jnp.bfloat16
    out_shape = (
        jax.ShapeDtypeStruct((b, MLA_HEADS, s, LANES), bf),
        jax.ShapeDtypeStruct((b, MLA_HEADS, s, LANES), bf),
        jax.ShapeDtypeStruct((b, MLA_HEADS // 2, s // tk, LANES, tk), bf),
        jax.ShapeDtypeStruct((b, 2 * DIFF_HEADS, s, LANES), bf),
        jax.ShapeDtypeStruct((b, DIFF_HEADS, s, LANES), bf),
        jax.ShapeDtypeStruct((b, DIFF_HEADS, s // tk, LANES, tk), bf),
        jax.ShapeDtypeStruct((b, s, 2 * D_MODEL), bf),
    )
    return pl.pallas_call(
        _prep_kernel,
        grid=grid,
        in_specs=[
            pl.BlockSpec((1, tm, D_MODEL), lambda bi, ti: (bi, ti, 0)),
            _const_spec(gattn.shape),
            _const_spec(w1.shape),
            _const_spec(glat.shape),
            _const_spec(wq.shape),
            _const_spec(wkv.shape),
            pl.BlockSpec((8, tm, LANES), lambda bi, ti: (0, ti, 0)),
        ],
        out_specs=(
            head_out(MLA_HEADS), head_out(MLA_HEADS), vt_out(MLA_HEADS // 2),
            head_out(2 * DIFF_HEADS), head_out(DIFF_HEADS), vt_out(DIFF_HEADS),
            pl.BlockSpec((1, tm, 2 * D_MODEL), lambda bi, ti: (bi, ti, 0)),
        ),
        out_shape=out_shape,
        compiler_params=pltpu.CompilerParams(
            dimension_semantics=("arbitrary", "arbitrary"), vmem_limit_bytes=V7X_VMEM_LIMIT),
        name="prep",
    )(x, gattn, w1, glat, wq, wkv, tabs)


def _attn_kernel(q_ref, k_ref, vt_ref, lam_ref, subln_ref, o_ref, st_buf, mx_buf, m_scr, l_scr, acc_scr, *, mode, tq):
    qi = pl.program_id(2)
    shared_k = k_ref.shape[1] == 1
    m_scr[...] = jnp.full(m_scr.shape, NEG_BIG, jnp.float32)
    l_scr[...] = jnp.zeros(l_scr.shape, jnp.float32)
    acc_scr[...] = jnp.zeros(acc_scr.shape, jnp.float32)

    def scores(kb, slot, masked):
        k0 = pl.multiple_of(kb * tq, tq)
        for hh in range(2):
            k = k_ref[0, 0 if shared_k else hh, pl.ds(k0, tq), :]
            st = _dot_nt(k, q_ref[0, hh])
            if masked:
                key = lax.broadcasted_iota(jnp.int32, st.shape, 0)
                qry = lax.broadcasted_iota(jnp.int32, st.shape, 1)
                st = jnp.where(key <= qry, st, NEG_BIG)
            st_buf[slot, hh] = st
            mx_buf[slot, hh] = jnp.max(st, axis=0, keepdims=True)

    def softmax_pv(kb, slot):
        vt = vt_ref[0, 0, kb]
        for hh in range(2):
            m_prev = m_scr[hh]
            m_new = jnp.maximum(m_prev, mx_buf[slot, hh])
            alpha = jnp.exp2(m_prev - m_new)
            p = jnp.exp2(st_buf[slot, hh] - m_new)
            l_scr[hh] = alpha * l_scr[hh] + jnp.sum(p, axis=0, keepdims=True)
            acc_scr[hh] = alpha * acc_scr[hh] + _f32dot(vt, p.astype(jnp.bfloat16))
            m_scr[hh] = m_new

    scores(qi, 0, True)

    def body(j, carry):
        scores(2 * j, 1, False)
        softmax_pv(jnp.where(j == 0, qi, 2 * j - 1), 0)
        scores(2 * j + 1, 0, False)
        softmax_pv(2 * j, 1)
        return carry

    lax.fori_loop(0, qi // 2, body, 0)
    last = qi - 1

    @pl.when(qi % 2 == 1)
    def _():
        scores(last, 1, False)
        softmax_pv(jnp.where(qi == 1, qi, last - 1), 0)
        softmax_pv(last, 1)

    @pl.when(qi % 2 == 0)
    def _():
        softmax_pv(jnp.where(qi == 0, qi, last), 0)

    oa = acc_scr[0] / l_scr[0]
    ob = acc_scr[1] / l_scr[1]
    if mode == "mla":
        sub = lax.broadcasted_iota(jnp.int32, oa.shape, 0)
        o_ref[0] = jnp.where(sub < MLA_V, oa, ob).T.astype(o_ref.dtype)
    else:
        lp = lam_ref[...]
        lam = (jnp.exp(jnp.sum(lp[0:1] * lp[1:2], axis=-1, keepdims=True))
               - jnp.exp(jnp.sum(lp[2:3] * lp[3:4], axis=-1, keepdims=True)) + LAMBDA_INIT)
        o = oa - lam * ob
        o = o * lax.rsqrt(jnp.sum(o * o, axis=0, keepdims=True) * (1.0 / DIFF_V) + NORM_EPS)
        o_ref[0] = (o.T * subln_ref[...] * (1.0 - LAMBDA_INIT)).astype(o_ref.dtype)


def _attn_call(q, k, vt, lam, subln, *, mode):
    b, _, s, _ = q.shape
    n_groups = vt.shape[1]
    kh = k.shape[1] // n_groups
    tq, tk = _att_tiles(s)
    grid = (b, n_groups, s // tq)
    assert tk == tq
    kernel = functools.partial(_attn_kernel, mode=mode, tq=tq)
    return pl.pallas_call(
        kernel,
        grid=grid,
        in_specs=[
            pl.BlockSpec((1, 2, tq, LANES), lambda bi, g, qi: (bi, g, qi, 0)),
            pl.BlockSpec((1, kh, s, LANES), lambda bi, g, qi: (bi, g, 0, 0)),
            pl.BlockSpec((1, 1, s // tk, LANES, tk), lambda bi, g, qi: (bi, g, 0, 0, 0)),
            _const_spec(lam.shape),
            _const_spec(subln.shape),
        ],
        out_specs=pl.BlockSpec((1, tq, LANES), lambda bi, g, qi: (bi, qi, g)),
        out_shape=jax.ShapeDtypeStruct((b, s, n_groups * LANES), jnp.bfloat16),
        scratch_shapes=[
            pltpu.VMEM((2, 2, tq, tq), jnp.float32),
            pltpu.VMEM((2, 2, 1, tq), jnp.float32),
            pltpu.VMEM((2, 1, tq), jnp.float32),
            pltpu.VMEM((2, 1, tq), jnp.float32),
            pltpu.VMEM((2, LANES, tq), jnp.float32),
        ],
        compiler_params=pltpu.CompilerParams(
            dimension_semantics=("arbitrary", "arbitrary", "arbitrary"), vmem_limit_bytes=V7X_VMEM_LIMIT),
        name="attn_" + mode,
    )(q, k, vt, lam, subln)


def _to_token_tiles(ref, mat):
    n = mat.shape[0]
    for j in range(TOK_ROWS):
        ref[pl.ds(j, n, stride=TOK_ROWS), :] = mat[:, j * LANES:(j + 1) * LANES]


def _from_token_tiles(ref, n):
    return jnp.concatenate([ref[pl.ds(j, n, stride=TOK_ROWS), :] for j in range(TOK_ROWS)], axis=1)


def _tok(ref, t):
    return ref.at[pl.ds(pl.multiple_of(t * TOK_ROWS, TOK_ROWS), TOK_ROWS)]


def _copy_token_run(src, dst, src_tok, dst_tok, count, sem, top_bit):
    bit = top_bit
    while bit >= 1:
        done = count & ~(2 * bit - 1)

        @pl.when((count & bit) != 0)
        def _(bit=bit, done=done):
            s = pl.multiple_of((src_tok + done) * TOK_ROWS, TOK_ROWS)
            d = pl.multiple_of((dst_tok + done) * TOK_ROWS, TOK_ROWS)
            pltpu.make_async_copy(src.at[pl.ds(s, bit * TOK_ROWS)], dst.at[pl.ds(d, bit * TOK_ROWS)], sem).start()

        bit //= 2


def _route_kernel(x_ref, oa_ref, ob_ref, sg_ref, wbm_ref, wbd_ref, wout_ref, gffn_ref, wrh_ref, wrl_ref, br_ref,
                  h_ref, hn_ref, idx_ref, w_ref, rank_ref, tcnt_ref, tbase_ref, cnt_ref, carry_scr):
    @pl.when(pl.program_id(0) == 0)
    def _():
        carry_scr[...] = jnp.zeros(carry_scr.shape, jnp.float32)

    sg = sg_ref[...]
    ma = _f32dot(oa_ref[...], wbm_ref[...])
    mb = _f32dot(ob_ref[...], wbd_ref[...])
    merged = sg[:, :D_MODEL] * ma + sg[:, D_MODEL:] * mb
    h = x_ref[...] + _f32dot(merged.astype(jnp.bfloat16), wout_ref[...])
    h_ref[...] = h

    hn = h * _rms_scale(h, D_MODEL) * gffn_ref[...]
    _to_token_tiles(hn_ref, hn)
    hn_hi = hn.astype(jnp.bfloat16)
    hn_lo = (hn - hn_hi.astype(jnp.float32)).astype(jnp.bfloat16)
    wrh = wrh_ref[...]
    logits = _f32dot(hn_hi, wrh) + _f32dot(hn_lo, wrh) + _f32dot(hn_hi, wrl_ref[...]) + br_ref[...]

    tm = logits.shape[0]
    lane = _lane_iota(logits.shape)
    work = logits
    vals, onehots = [], []
    for _k in range(TOP_K):
        mx = jnp.max(work, axis=-1, keepdims=True)
        sel = jnp.min(jnp.where(work == mx, lane, LANES), axis=-1, keepdims=True)
        oh = lane == sel
        vals.append(mx)
        onehots.append(oh)
        work = jnp.where(oh, NEG_BIG * 2, work)
    exps = [jnp.exp(vv - vals[0]) for vv in vals]
    denom = exps[0] + exps[1] + exps[2] + exps[3]

    chosen = jnp.zeros(logits.shape, jnp.float32)
    for oh in onehots:
        chosen = chosen + oh.astype(jnp.float32)
    r_i = lax.broadcasted_iota(jnp.int32, (tm, tm), 0)
    c_i = lax.broadcasted_iota(jnp.int32, (tm, tm), 1)
    ltri = (c_i < r_i).astype(jnp.bfloat16)
    before = _f32dot(ltri, chosen.astype(jnp.bfloat16))
    tile_counts = jnp.sum(chosen, axis=0, keepdims=True)
    tbase_ref[0] = carry_scr[...]
    tcnt_ref[0] = tile_counts
    carry_scr[...] = carry_scr[...] + tile_counts
    cnt_ref[...] = carry_scr[...]

    idx_out = jnp.zeros(logits.shape, jnp.int32)
    w_out = jnp.zeros(logits.shape, jnp.float32)
    rank_out = jnp.zeros(logits.shape, jnp.int32)
    for kk in range(TOP_K):
        oh = onehots[kk]
        e_k = jnp.sum(jnp.where(oh, lane, 0), axis=-1, keepdims=True)
        r_k = jnp.sum(jnp.where(oh, before, 0.0), axis=-1, keepdims=True).astype(jnp.int32)
        idx_out = jnp.where(lane == kk, e_k, idx_out)
        rank_out = jnp.where(lane == kk, r_k, rank_out)
        w_out = jnp.where(lane == kk, exps[kk] / denom, w_out)
    idx_ref[...] = idx_out
    w_ref[...] = w_out
    rank_ref[...] = rank_out


def _route_call(x2, oa, ob, sg, wbm, wbd, wout, gffn, wrh, wrl, br):
    n = x2.shape[0]
    tm = min(ROUTE_TM, n)
    nt = n // tm
    row = lambda w: pl.BlockSpec((tm, w), lambda i: (i, 0))
    per_tile = pl.BlockSpec((1, 1, LANES), lambda i: (i, 0, 0))
    out_shape = (
        jax.ShapeDtypeStruct((n, D_MODEL), jnp.float32),
        jax.ShapeDtypeStruct((n * TOK_ROWS, LANES), jnp.float32),
        jax.ShapeDtypeStruct((n, LANES), jnp.int32),
        jax.ShapeDtypeStruct((n, LANES), jnp.float32),
        jax.ShapeDtypeStruct((n, LANES), jnp.int32),
        jax.ShapeDtypeStruct((nt, 1, LANES), jnp.float32),
        jax.ShapeDtypeStruct((nt, 1, LANES), jnp.float32),
        jax.ShapeDtypeStruct((1, LANES), jnp.float32),
    )
    return pl.pallas_call(
        _route_kernel,
        grid=(nt,),
        in_specs=[row(D_MODEL), row(512), row(512), row(2 * D_MODEL),
                  _const_spec(wbm.shape), _const_spec(wbd.shape), _const_spec(wout.shape), _const_spec(gffn.shape),
                  _const_spec(wrh.shape), _const_spec(wrl.shape), _const_spec(br.shape)],
        out_specs=(row(D_MODEL), pl.BlockSpec((tm * TOK_ROWS, LANES), lambda i: (i, 0)), row(LANES), row(LANES),
                   row(LANES), per_tile, per_tile, _const_spec((1, LANES))),
        out_shape=out_shape,
        scratch_shapes=[pltpu.VMEM((1, LANES), jnp.float32)],
        compiler_params=pltpu.CompilerParams(dimension_semantics=("arbitrary",), vmem_limit_bytes=V7X_VMEM_LIMIT),
        name="route",
    )(x2, oa, ob, sg, wbm, wbd, wout, gffn, wrh, wrl, br)


def _dispatch_kernel(seg_ref, lp_ref, pad_ref, hn_ref, xs_ref, scr, zscr, sems, zsem):
    step = pl.program_id(0)
    last = pl.num_programs(0) - 1
    tm = hn_ref.shape[0] // TOK_ROWS
    par = step % 2
    buf = scr.at[par]

    def place(t, carry):
        v = _tok(hn_ref, t)[...]
        for kk in range(TOP_K):
            _tok(buf, lp_ref[0, 0, t * TOP_K + kk])[...] = v
        return carry

    lax.fori_loop(0, tm, place, 0, unroll=TOK_UNROLL)

    def run(e, carry):
        _copy_token_run(buf, xs_ref, seg_ref[0, 0, N_EXPERTS + e], seg_ref[0, 0, 2 * N_EXPERTS + e],
                        seg_ref[0, 0, e], sems.at[par], tm)
        return carry

    lax.fori_loop(0, N_EXPERTS, run, 0)

    def drain(p):
        pltpu.make_async_copy(scr.at[p], xs_ref.at[pl.ds(0, scr.shape[1])], sems.at[p]).wait()

    @pl.when(step == 0)
    def _():
        zscr[...] = jnp.zeros(zscr.shape, zscr.dtype)

        def pad_run(e, carry):
            _copy_token_run(zscr, xs_ref, 0, pad_ref[e], pad_ref[N_EXPERTS + e], zsem, FFN_TM // 2)
            return carry

        lax.fori_loop(0, N_EXPERTS, pad_run, 0)

        def pad_tile(j, carry):
            d = pl.multiple_of((pad_ref[2 * N_EXPERTS] + j * FFN_TM) * TOK_ROWS, TOK_ROWS)
            pltpu.make_async_copy(zscr, xs_ref.at[pl.ds(d, FFN_TM * TOK_ROWS)], zsem).start()
            return carry

        lax.fori_loop(0, pad_ref[2 * N_EXPERTS + 1], pad_tile, 0)

    @pl.when(step > 0)
    def _():
        drain(1 - par)

    @pl.when(step == last)
    def _():
        drain(par)
        for _e in range(N_EXPERTS):
            pltpu.make_async_copy(zscr, xs_ref.at[pl.ds(0, FFN_TM * TOK_ROWS)], zsem).wait()


def _dispatch_call(seg3, lp3, pad, hn, n_rows):
    nt = seg3.shape[0]
    tm = hn.shape[0] // TOK_ROWS // nt
    smem = lambda a: pl.BlockSpec((1, 1, a.shape[2]), lambda i: (i, 0, 0), memory_space=pltpu.SMEM)
    return pl.pallas_call(
        _dispatch_kernel,
        grid=(nt,),
        in_specs=[
            smem(seg3), smem(lp3),
            pl.BlockSpec(memory_space=pltpu.SMEM),
            pl.BlockSpec((tm * TOK_ROWS, LANES), lambda i: (i, 0)),
        ],
        out_specs=pl.BlockSpec(memory_space=pl.ANY),
        out_shape=jax.ShapeDtypeStruct((n_rows * TOK_ROWS, LANES), jnp.float32),
        scratch_shapes=[pltpu.VMEM((2, TOP_K * tm * TOK_ROWS, LANES), jnp.float32),
                        pltpu.VMEM((FFN_TM * TOK_ROWS, LANES), jnp.float32),
                        pltpu.SemaphoreType.DMA((2,)), pltpu.SemaphoreType.DMA(())],
        compiler_params=pltpu.CompilerParams(dimension_semantics=("arbitrary",), vmem_limit_bytes=V7X_VMEM_LIMIT),
        name="dispatch",
    )(seg3, lp3, pad, hn)


def _ffn_kernel(te_ref, tv_ref, nt_ref, xs_ref, w1_ref, b1_ref, w2_ref, b2_ref, y_ref):
    i = pl.program_id(0)
    tm = xs_ref.shape[0] // TOK_ROWS

    @pl.when(i < nt_ref[0])
    def _():
        rows = lax.broadcasted_iota(jnp.int32, (tm, 1), 0)
        xt = jnp.where(rows < tv_ref[i], _from_token_tiles(xs_ref, tm), 0.0).astype(jnp.bfloat16)
        u = _f32dot(xt, w1_ref[0]) + b1_ref[0]
        glu = jnp.minimum(u[:, :D_EXPERT], SWIGLU_LIMIT)
        lin = jnp.clip(u[:, D_EXPERT:], -SWIGLU_LIMIT, SWIGLU_LIMIT)
        a = glu * jax.nn.sigmoid(SWIGLU_ALPHA * glu) * (lin + 1.0)
        y = _f32dot(a.astype(jnp.bfloat16), w2_ref[0]) + b2_ref[0]
        _to_token_tiles(y_ref, y)

    @pl.when(i >= nt_ref[0])
    def _():
        y_ref[...] = jnp.zeros(y_ref.shape, y_ref.dtype)


def _ffn_call(tile_expert, tile_valid, n_tiles, xs, w1, b1, w2, b2):
    n_rows = xs.shape[0] // TOK_ROWS
    tm = FFN_TM
    t_tiles = n_rows // tm

    def tile_map(i, te, tv, nt):
        return (jnp.minimum(i, nt[0] - 1), 0)

    def out_map(i, te, tv, nt):
        return (i, 0)

    def w_map(i, te, tv, nt):
        return (te[i], 0, 0)

    grid_spec = pltpu.PrefetchScalarGridSpec(
        num_scalar_prefetch=3,
        grid=(t_tiles,),
        in_specs=[
            pl.BlockSpec((tm * TOK_ROWS, LANES), tile_map),
            pl.BlockSpec((1, D_MODEL, 2 * D_EXPERT), w_map),
            pl.BlockSpec((1, 1, 2 * D_EXPERT), w_map),
            pl.BlockSpec((1, D_EXPERT, D_MODEL), w_map),
            pl.BlockSpec((1, 1, D_MODEL), w_map),
        ],
        out_specs=pl.BlockSpec((tm * TOK_ROWS, LANES), out_map),
    )
    return pl.pallas_call(
        _ffn_kernel,
        grid_spec=grid_spec,
        out_shape=jax.ShapeDtypeStruct(xs.shape, jnp.float32),
        compiler_params=pltpu.CompilerParams(dimension_semantics=("arbitrary",), vmem_limit_bytes=V7X_VMEM_LIMIT),
        name="ffn",
    )(tile_expert, tile_valid, n_tiles, xs, w1, b1, w2, b2)


def _combine_kernel(seg_ref, snext_ref, lp_ref, w_ref, h_ref, y_ref, o_ref, ybuf, tok_scr, sems):
    step = pl.program_id(0)
    last = pl.num_programs(0) - 1
    tm = h_ref.shape[0]
    par = step % 2

    def fetch(s_ref, slot):
        def run(e, carry):
            _copy_token_run(y_ref, ybuf.at[slot], s_ref[0, 0, 2 * N_EXPERTS + e], s_ref[0, 0, N_EXPERTS + e],
                            s_ref[0, 0, e], sems.at[slot], tm)
            return carry

        lax.fori_loop(0, N_EXPERTS, run, 0)

    @pl.when(step == 0)
    def _():
        fetch(seg_ref, 0)

    @pl.when(step < last)
    def _():
        fetch(snext_ref, 1 - par)

    pltpu.make_async_copy(y_ref.at[pl.ds(0, ybuf.shape[1])], ybuf.at[par], sems.at[par]).wait()
    buf = ybuf.at[par]

    def token(t, carry):
        acc = None
        for kk in range(TOP_K):
            term = w_ref[0, 0, t * TOP_K + kk] * _tok(buf, lp_ref[0, 0, t * TOP_K + kk])[...]
            acc = term if acc is None else acc + term
        _tok(tok_scr, t)[...] = acc
        return carry

    lax.fori_loop(0, tm, token, 0, unroll=TOK_UNROLL)
    o_ref[...] = h_ref[...] + _from_token_tiles(tok_scr, tm)


def _combine_call(seg3, lp3, w3, h, y):
    n = h.shape[0]
    nt = seg3.shape[0]
    tm = n // nt
    smem = lambda a, f: pl.BlockSpec((1, 1, a.shape[2]), f, memory_space=pltpu.SMEM)
    here = lambda i: (i, 0, 0)
    return pl.pallas_call(
        _combine_kernel,
        grid=(nt,),
        in_specs=[
            smem(seg3, here), smem(seg3, lambda i: (jnp.minimum(i + 1, nt - 1), 0, 0)), smem(lp3, here), smem(w3, here),
            pl.BlockSpec((tm, D_MODEL), lambda i: (i, 0)),
            pl.BlockSpec(memory_space=pl.ANY),
        ],
        out_specs=pl.BlockSpec((tm, D_MODEL), lambda i: (i, 0)),
        out_shape=jax.ShapeDtypeStruct((n, D_MODEL), jnp.float32),
        scratch_shapes=[pltpu.VMEM((2, TOP_K * tm * TOK_ROWS, LANES), jnp.float32),
                        pltpu.VMEM((tm * TOK_ROWS, LANES), jnp.float32), pltpu.SemaphoreType.DMA((2,))],
        compiler_params=pltpu.CompilerParams(dimension_semantics=("arbitrary",), vmem_limit_bytes=V7X_VMEM_LIMIT),
        name="combine",
    )(seg3, seg3, lp3, w3, h, y)


def _rot_cols(w, group):
    lead = w.shape[:-1]
    g = w.reshape(lead + (-1, group))
    half = group // 2
    return jnp.concatenate([-g[..., half:], g[..., :half]], axis=-1).reshape(w.shape)


def _rope_tables(s, dim):
    inv_freq = 1.0 / (ROPE_THETA ** (jnp.arange(0, dim, 2, dtype=jnp.float32) / dim))
    ang = jnp.arange(s, dtype=jnp.float32)[:, None] * inv_freq[None, :]
    return jnp.cos(ang), jnp.sin(ang)


def _mla_tables(s, gain, scale):
    cos, sin = _rope_tables(s, MLA_ROPE)
    cos2 = jnp.concatenate([cos, cos], axis=-1)
    sin2 = jnp.concatenate([sin, sin], axis=-1)
    g_nope, g_rope = gain[:MLA_NOPE], gain[MLA_NOPE:]
    g_perm = jnp.concatenate([g_rope[MLA_ROPE // 2:], g_rope[:MLA_ROPE // 2]])
    pad = jnp.zeros((s, LANES - MLA_QK), jnp.float32)
    c = jnp.concatenate([jnp.broadcast_to(g_nope, (s, MLA_NOPE)), cos2 * g_rope, pad], axis=-1) * scale
    sn = jnp.concatenate([jnp.zeros((s, MLA_NOPE), jnp.float32), sin2 * g_perm, pad], axis=-1) * scale
    return c, sn


def _diff_tables(s, gain, scale):
    cos, sin = _rope_tables(s, DIFF_D)
    cos2 = jnp.concatenate([cos, cos], axis=-1)
    sin2 = jnp.concatenate([sin, sin], axis=-1)
    g_perm = jnp.concatenate([gain[DIFF_D // 2:], gain[:DIFF_D // 2]])
    c = cos2 * gain * scale
    sn = sin2 * g_perm * scale
    return jnp.concatenate([c, c], axis=-1), jnp.concatenate([sn, sn], axis=-1)


def _pad_heads(w, width):
    kdim = w.shape[0]
    g = w.reshape(kdim, -1, width)
    g = jnp.pad(g, ((0, 0), (0, 0), (0, LANES - width)))
    return g.reshape(kdim, -1)


def kernel(x, attn_norm, w_in, mla_q_lat_norm, mla_kv_lat_norm, mla_w_uq, mla_w_ukv, mla_q_norm, mla_k_norm,
           diff_q_norm, diff_k_norm, diff_lambda_q1, diff_lambda_k1, diff_lambda_q2, diff_lambda_k2, diff_subln,
           w_branch_mla, w_branch_diff, w_out, ffn_norm, w_router, b_router, w_mlp1, b_mlp1, w_mlp2, b_mlp2):
    b, s, d = x.shape
    n = b * s
    bf = jnp.bfloat16
    f32 = jnp.float32
    i = 0

    wi = w_in[i]
    c0 = MLA_Q_LORA + MLA_KV_LORA
    w_kpe = wi[:, c0:c0 + MLA_ROPE]
    c1 = c0 + MLA_ROPE
    w_dq, w_dk, w_dv = wi[:, c1:c1 + 512], wi[:, c1 + 512:c1 + 1024], wi[:, c1 + 1024:c1 + 1536]
    w_gate = wi[:, c1 + 1536:]
    place = lambda w: jnp.pad(w, ((0, 0), (MLA_NOPE, LANES - MLA_QK)))
    w1 = jnp.concatenate([
        wi[:, :c0], place(w_kpe), place(_rot_cols(w_kpe, MLA_ROPE)),
        w_dq, _rot_cols(w_dq, DIFF_D), w_dk, _rot_cols(w_dk, DIFF_D), w_dv, w_gate], axis=1).astype(bf)

    wuq = mla_w_uq[i].reshape(MLA_Q_LORA, MLA_HEADS, MLA_QK)
    wuq_rot = jnp.concatenate(
        [jnp.zeros((MLA_Q_LORA, MLA_HEADS, MLA_NOPE), f32), _rot_cols(wuq[..., MLA_NOPE:], MLA_ROPE)], axis=-1)
    wq = jnp.concatenate([_pad_heads(wuq.reshape(MLA_Q_LORA, -1), MLA_QK),
                          _pad_heads(wuq_rot.reshape(MLA_Q_LORA, -1), MLA_QK)], axis=1).astype(bf)
    wukv = mla_w_ukv[i].reshape(MLA_KV_LORA, MLA_HEADS, MLA_NOPE + MLA_V)
    wkv = jnp.concatenate([_pad_heads(wukv[..., :MLA_NOPE].reshape(MLA_KV_LORA, -1), MLA_NOPE),
                           wukv[..., MLA_NOPE:].reshape(MLA_KV_LORA, -1)], axis=1).astype(bf)

    cq, sq = _mla_tables(s, mla_q_norm[i], LOG2E / math.sqrt(MLA_QK))
    ck, sk = _mla_tables(s, mla_k_norm[i], 1.0)
    cqd, sqd = _diff_tables(s, diff_q_norm[i], LOG2E / math.sqrt(DIFF_D))
    ckd, skd = _diff_tables(s, diff_k_norm[i], 1.0)
    tabs = jnp.stack([cq, sq, ck, sk, cqd, sqd, ckd, skd])

    glat = jnp.concatenate([mla_q_lat_norm[i], mla_kv_lat_norm[i]])[None, :]
    qa, ka, va, qd, kd, vd, sg = _prep_call(x, attn_norm[i][None, :], w1, glat, wq, wkv, tabs)

    lam = jnp.pad(jnp.stack([diff_lambda_q1[i], diff_lambda_k1[i], diff_lambda_q2[i], diff_lambda_k2[i]]),
                  ((0, 0), (0, LANES - DIFF_D)))
    subln = diff_subln[i][None, :]
    o_a = _attn_call(qa, ka, va, lam, subln, mode="mla")
    o_b = _attn_call(qd, kd, vd, lam, subln, mode="diff")

    wr = jnp.pad(w_router[i], ((0, 0), (0, LANES - N_EXPERTS)))
    wr_hi = wr.astype(bf)
    wr_lo = (wr - wr_hi.astype(f32)).astype(bf)
    br = jnp.concatenate([b_router[i], jnp.full((LANES - N_EXPERTS,), NEG_BIG, f32)])[None, :]
    h, hn, idx, gw, rank, tcnt, tbase, cnt = _route_call(
        x.reshape(n, d), o_a.reshape(n, 512), o_b.reshape(n, 512), sg.reshape(n, 2 * d),
        w_branch_mla[i].astype(bf), w_branch_diff[i].astype(bf), w_out[i].astype(bf), ffn_norm[i][None, :],
        wr_hi, wr_lo, br)

    i32 = jnp.int32
    counts = cnt[0, :N_EXPERTS].astype(i32)
    tiles_per = (counts + FFN_TM - 1) // FFN_TM
    tile_end = jnp.cumsum(tiles_per)
    tile_start = tile_end - tiles_per
    n_tiles = tile_end[-1:]
    t_tiles = n * TOP_K // FFN_TM + N_EXPERTS
    tids = jnp.arange(t_tiles, dtype=i32)
    tile_expert = jnp.minimum(jnp.sum(tids[:, None] >= tile_end[None, :], axis=1), N_EXPERTS - 1).astype(i32)
    tile_valid = jnp.clip(counts[tile_expert] - (tids - tile_start[tile_expert]) * FFN_TM, 0, FFN_TM).astype(i32)

    nt = tcnt.shape[0]
    tc = tcnt[:, 0, :N_EXPERTS].astype(i32)
    tb = tbase[:, 0, :N_EXPERTS].astype(i32)
    off = jnp.cumsum(tc, axis=1) - tc
    first = (tile_start * FFN_TM)[None, :] + tb
    seg3 = jnp.concatenate([tc, off, first], axis=1).reshape(nt, 1, 3 * N_EXPERTS)
    per = n // nt * TOP_K
    lp3 = (jnp.take_along_axis(off, idx[:, :TOP_K].reshape(nt, per), axis=1)
           + rank[:, :TOP_K].reshape(nt, per)).reshape(nt, 1, per)
    w3 = gw[:, :TOP_K].reshape(nt, 1, per)
    pad = jnp.concatenate([tile_start * FFN_TM + counts, tiles_per * FFN_TM - counts,
                           n_tiles * FFN_TM, t_tiles - n_tiles]).astype(i32)

    xs = _dispatch_call(seg3, lp3, pad, hn, t_tiles * FFN_TM)
    y = _ffn_call(tile_expert, tile_valid, n_tiles.astype(i32), xs,
                  w_mlp1[i].astype(bf), b_mlp1[i][:, None, :], w_mlp2[i].astype(bf), b_mlp2[i][:, None, :])
    out = _combine_call(seg3, lp3, w3, h, y)
    return out.reshape(b, s, d)
```

```python
import functools
import math

import jax
import jax.numpy as jnp
import numpy as np
from jax import lax
from jax.experimental import pallas as pl
from jax.experimental.pallas import tpu as pltpu

D_MODEL = 1024
ROPE_THETA = 10000.0
NORM_EPS = 1e-6
MLA_HEADS = 8
MLA_NOPE = 64
MLA_ROPE = 32
MLA_QK = MLA_NOPE + MLA_ROPE
MLA_V = 64
MLA_Q_LORA = 768
MLA_KV_LORA = 256
DIFF_HEADS = 4
DIFF_D = 64
DIFF_V = 128
N_EXPERTS = 32
TOP_K = 4
D_EXPERT = 1024
SWIGLU_ALPHA = 1.702
SWIGLU_LIMIT = 7.0
LAMBDA_INIT = 0.8 - 0.6 * math.exp(-0.3 * 0)

LANES = 128
V7X_VMEM_LIMIT = 56 * 1024 * 1024

PREP_TM = 256
ATT_TQ = 512
ROUTE_TM = 512
TOK_ROWS = 8
TOK_UNROLL = 8
FFN_TM = 512

NEG_BIG = -1e30
LOG2E = math.log2(math.e)

_C_LAT = 0
_C_KPE = 1024
_C_DQ = 1280
_C_DK = 2304
_C_DV = 3328
_C_GATE = 3840
_C_END = 5888


def _f32dot(a, b):
    return jnp.dot(a, b, preferred_element_type=jnp.float32)


def _dot_nt(a, b):
    return lax.dot_general(a, b, (((1,), (1,)), ((), ())), preferred_element_type=jnp.float32)


def _rms_scale(x, width):
    return lax.rsqrt(jnp.sum(x * x, axis=-1, keepdims=True) * (1.0 / width) + NORM_EPS)


def _lane_iota(shape):
    return lax.broadcasted_iota(jnp.int32, shape, len(shape) - 1)


def _prep_kernel(x_ref, gattn_ref, w1_ref, glat_ref, wq_ref, wkv_ref, tabs_ref,
                 qa_ref, ka_ref, va_ref, qd_ref, kd_ref, vd_ref, sg_ref):
    x = x_ref[0]
    xn = x * _rms_scale(x, D_MODEL) * gattn_ref[...]
    proj = _f32dot(xn.astype(jnp.bfloat16), w1_ref[...])

    glat = glat_ref[...]
    q_lat = proj[:, 0:MLA_Q_LORA]
    kv_lat = proj[:, MLA_Q_LORA:MLA_Q_LORA + MLA_KV_LORA]
    qn = q_lat * _rms_scale(q_lat, MLA_Q_LORA) * glat[:, 0:MLA_Q_LORA]
    kvn = kv_lat * _rms_scale(kv_lat, MLA_KV_LORA) * glat[:, MLA_Q_LORA:]
    qq = _f32dot(qn.astype(jnp.bfloat16), wq_ref[...])
    kk = _f32dot(kvn.astype(jnp.bfloat16), wkv_ref[...])

    kpe = proj[:, _C_KPE:_C_KPE + LANES]
    kper = proj[:, _C_KPE + LANES:_C_KPE + 2 * LANES]
    cq, sq = tabs_ref[0], tabs_ref[1]
    ck, sk = tabs_ref[2], tabs_ref[3]
    for h in range(MLA_HEADS):
        lo, hi = h * LANES, (h + 1) * LANES
        qh = qq[:, lo:hi]
        qa_ref[0, h] = (_rms_scale(qh, MLA_QK) * (qh * cq + qq[:, 1024 + lo:1024 + hi] * sq)).astype(jnp.bfloat16)
        kh = kk[:, lo:hi] + kpe
        ka_ref[0, h] = (_rms_scale(kh, MLA_QK) * (kh * ck + kper * sk)).astype(jnp.bfloat16)
    for j in range(MLA_HEADS // 2):
        va_ref[0, j, 0] = kk[:, 1024 + j * LANES:1024 + (j + 1) * LANES].T.astype(jnp.bfloat16)

    cqd, sqd = tabs_ref[4], tabs_ref[5]
    ckd, skd = tabs_ref[6], tabs_ref[7]
    lane = _lane_iota((x.shape[0], LANES))
    first = lane < DIFF_D

    def half_norm(v):
        sqv = v * v
        s_all = jnp.sum(sqv, axis=-1, keepdims=True)
        s_lo = jnp.sum(jnp.where(first, sqv, 0.0), axis=-1, keepdims=True)
        r_lo = lax.rsqrt(s_lo * (1.0 / DIFF_D) + NORM_EPS)
        r_hi = lax.rsqrt((s_all - s_lo) * (1.0 / DIFF_D) + NORM_EPS)
        return jnp.where(first, r_lo, r_hi)

    for h in range(DIFF_HEADS):
        lo, hi = h * LANES, (h + 1) * LANES
        dq = proj[:, _C_DQ + lo:_C_DQ + hi]
        dqr = proj[:, _C_DQ + 512 + lo:_C_DQ + 512 + hi]
        qv = half_norm(dq) * (dq * cqd + dqr * sqd)
        qd_ref[0, 2 * h] = jnp.where(first, qv, 0.0).astype(jnp.bfloat16)
        qd_ref[0, 2 * h + 1] = jnp.where(first, 0.0, qv).astype(jnp.bfloat16)
        dk = proj[:, _C_DK + lo:_C_DK + hi]
        dkr = proj[:, _C_DK + 512 + lo:_C_DK + 512 + hi]
        kd_ref[0, h] = (half_norm(dk) * (dk * ckd + dkr * skd)).astype(jnp.bfloat16)
        vd_ref[0, h, 0] = proj[:, _C_DV + lo:_C_DV + hi].T.astype(jnp.bfloat16)

    sg_ref[0] = jax.nn.sigmoid(proj[:, _C_GATE:_C_END]).astype(jnp.bfloat16)


def _att_tiles(s):
    tq = min(ATT_TQ, s)
    return tq, tq


def _const_spec(shape):
    nd = len(shape)
    return pl.BlockSpec(shape, lambda *_: (0,) * nd)


def _prep_call(x, gattn, w1, glat, wq, wkv, tabs):
    b, s, _ = x.shape
    tm = min(PREP_TM, s)
    grid = (b, s // tm)
    head_out = lambda nh: pl.BlockSpec((1, nh, tm, LANES), lambda bi, ti: (bi, 0, ti, 0))
    _, tk = _att_tiles(s)
    per = tk // tm
    vt_out = lambda nh: pl.BlockSpec((1, nh, 1, LANES, tm), lambda bi, ti: (bi, 0, ti // per, 0, ti % per))
    bf = jnp.bfloat16
    out_shape = (
        jax.ShapeDtypeStruct((b, MLA_HEADS, s, LANES), bf),
        jax.ShapeDtypeStruct((b, MLA_HEADS, s, LANES), bf),
        jax.ShapeDtypeStruct((b, MLA_HEADS // 2, s // tk, LANES, tk), bf),
        jax.ShapeDtypeStruct((b, 2 * DIFF_HEADS, s, LANES), bf),
        jax.ShapeDtypeStruct((b, DIFF_HEADS, s, LANES), bf),
        jax.ShapeDtypeStruct((b, DIFF_HEADS, s // tk, LANES, tk), bf),
        jax.ShapeDtypeStruct((b, s, 2 * D_MODEL), bf),
    )
    return pl.pallas_call(
        _prep_kernel,
        grid=grid,
        in_specs=[
            pl.BlockSpec((1, tm, D_MODEL), lambda bi, ti: (bi, ti, 0)),
            _const_spec(gattn.shape),
            _const_spec(w1.shape),
            _const_spec(glat.shape),
            _const_spec(wq.shape),
            _const_spec(wkv.shape),
            pl.BlockSpec((8, tm, LANES), lambda bi, ti: (0, ti, 0)),
        ],
        out_specs=(
            head_out(MLA_HEADS), head_out(MLA_HEADS), vt_out(MLA_HEADS // 2),
            head_out(2 * DIFF_HEADS), head_out(DIFF_HEADS), vt_out(DIFF_HEADS),
            pl.BlockSpec((1, tm, 2 * D_MODEL), lambda bi, ti: (bi, ti, 0)),
        ),
        out_shape=out_shape,
        compiler_params=pltpu.CompilerParams(
            dimension_semantics=("arbitrary", "arbitrary"), vmem_limit_bytes=V7X_VMEM_LIMIT),
        name="prep",
    )(x, gattn, w1, glat, wq, wkv, tabs)


def _attn_kernel(q_ref, k_ref, vt_ref, lam_ref, subln_ref, o_ref, st_buf, mx_buf, m_scr, l_scr, acc_scr, *, mode, tq):
    qi = pl.program_id(2)
    shared_k = k_ref.shape[1] == 1
    m_scr[...] = jnp.full(m_scr.shape, NEG_BIG, jnp.float32)
    l_scr[...] = jnp.zeros(l_scr.shape, jnp.float32)
    acc_scr[...] = jnp.zeros(acc_scr.shape, jnp.float32)

    def scores(kb, slot, masked):
        k0 = pl.multiple_of(kb * tq, tq)
        for hh in range(2):
            k = k_ref[0, 0 if shared_k else hh, pl.ds(k0, tq), :]
            st = _dot_nt(k, q_ref[0, hh])
            if masked:
                key = lax.broadcasted_iota(jnp.int32, st.shape, 0)
                qry = lax.broadcasted_iota(jnp.int32, st.shape, 1)
                st = jnp.where(key <= qry, st, NEG_BIG)
            st_buf[slot, hh] = st
            mx_buf[slot, hh] = jnp.max(st, axis=0, keepdims=True)

    def softmax_pv(kb, slot):
        vt = vt_ref[0, 0, kb]
        for hh in range(2):
            m_prev = m_scr[hh]
            m_new = jnp.maximum(m_prev, mx_buf[slot, hh])
            alpha = jnp.exp2(m_prev - m_new)
            p = jnp.exp2(st_buf[slot, hh] - m_new)
            l_scr[hh] = alpha * l_scr[hh] + jnp.sum(p, axis=0, keepdims=True)
            acc_scr[hh] = alpha * acc_scr[hh] + _f32dot(vt, p.astype(jnp.bfloat16))
            m_scr[hh] = m_new

    scores(qi, 0, True)

    def body(j, carry):
        scores(2 * j, 1, False)
        softmax_pv(jnp.where(j == 0, qi, 2 * j - 1), 0)
        scores(2 * j + 1, 0, False)
        softmax_pv(2 * j, 1)
        return carry

    lax.fori_loop(0, qi // 2, body, 0)
    last = qi - 1

    @pl.when(qi % 2 == 1)
    def _():
        scores(last, 1, False)
        softmax_pv(jnp.where(qi == 1, qi, last - 1), 0)
        softmax_pv(last, 1)

    @pl.when(qi % 2 == 0)
    def _():
        softmax_pv(jnp.where(qi == 0, qi, last), 0)

    oa = acc_scr[0] / l_scr[0]
    ob = acc_scr[1] / l_scr[1]
    if mode == "mla":
        sub = lax.broadcasted_iota(jnp.int32, oa.shape, 0)
        o_ref[0] = jnp.where(sub < MLA_V, oa, ob).T.astype(o_ref.dtype)
    else:
        lp = lam_ref[...]
        lam = (jnp.exp(jnp.sum(lp[0:1] * lp[1:2], axis=-1, keepdims=True))
               - jnp.exp(jnp.sum(lp[2:3] * lp[3:4], axis=-1, keepdims=True)) + LAMBDA_INIT)
        o = oa - lam * ob
        o = o * lax.rsqrt(jnp.sum(o * o, axis=0, keepdims=True) * (1.0 / DIFF_V) + NORM_EPS)
        o_ref[0] = (o.T * subln_ref[...] * (1.0 - LAMBDA_INIT)).astype(o_ref.dtype)


def _attn_call(q, k, vt, lam, subln, *, mode):
    b, _, s, _ = q.shape
    n_groups = vt.shape[1]
    kh = k.shape[1] // n_groups
    tq, tk = _att_tiles(s)
    grid = (b, n_groups, s // tq)
    assert tk == tq
    kernel = functools.partial(_attn_kernel, mode=mode, tq=tq)
    return pl.pallas_call(
        kernel,
        grid=grid,
        in_specs=[
            pl.BlockSpec((1, 2, tq, LANES), lambda bi, g, qi: (bi, g, qi, 0)),
            pl.BlockSpec((1, kh, s, LANES), lambda bi, g, qi: (bi, g, 0, 0)),
            pl.BlockSpec((1, 1, s // tk, LANES, tk), lambda bi, g, qi: (bi, g, 0, 0, 0)),
            _const_spec(lam.shape),
            _const_spec(subln.shape),
        ],
        out_specs=pl.BlockSpec((1, tq, LANES), lambda bi, g, qi: (bi, qi, g)),
        out_shape=jax.ShapeDtypeStruct((b, s, n_groups * LANES), jnp.bfloat16),
        scratch_shapes=[
            pltpu.VMEM((2, 2, tq, tq), jnp.float32),
            pltpu.VMEM((2, 2, 1, tq), jnp.float32),
            pltpu.VMEM((2, 1, tq), jnp.float32),
            pltpu.VMEM((2, 1, tq), jnp.float32),
            pltpu.VMEM((2, LANES, tq), jnp.float32),
        ],
        compiler_params=pltpu.CompilerParams(
            dimension_semantics=("arbitrary", "arbitrary", "arbitrary"), vmem_limit_bytes=V7X_VMEM_LIMIT),
        name="attn_" + mode,
    )(q, k, vt, lam, subln)


def _to_token_tiles(ref, mat):
    n = mat.shape[0]
    for j in range(TOK_ROWS):
        ref[pl.ds(j, n, stride=TOK_ROWS), :] = mat[:, j * LANES:(j + 1) * LANES]


def _from_token_tiles(ref, n):
    return jnp.concatenate([ref[pl.ds(j, n, stride=TOK_ROWS), :] for j in range(TOK_ROWS)], axis=1)


def _tok_at(ref, row):
    return ref.at[pl.ds(pl.multiple_of(row, TOK_ROWS), TOK_ROWS)]


def _tok(ref, t):
    return _tok_at(ref, t * TOK_ROWS)


def _copy_token_run(src, dst, src_tok, dst_tok, count, sem, top_bit):
    bit = top_bit
    while bit >= 1:
        done = count & ~(2 * bit - 1)

        @pl.when((count & bit) != 0)
        def _(bit=bit, done=done):
            s = pl.multiple_of((src_tok + done) * TOK_ROWS, TOK_ROWS)
            d = pl.multiple_of((dst_tok + done) * TOK_ROWS, TOK_ROWS)
            pltpu.make_async_copy(src.at[pl.ds(s, bit * TOK_ROWS)], dst.at[pl.ds(d, bit * TOK_ROWS)], sem).start()

        bit //= 2


def _route_kernel(x_ref, oa_ref, ob_ref, sg_ref, wbm_ref, wbd_ref, wout_ref, gffn_ref, wrh_ref, wrl_ref, br_ref,
                  h_ref, hn_ref, idx_ref, w_ref, rank_ref, tcnt_ref, tbase_ref, cnt_ref, carry_scr):
    @pl.when(pl.program_id(0) == 0)
    def _():
        carry_scr[...] = jnp.zeros(carry_scr.shape, jnp.float32)

    sg = sg_ref[...]
    ma = _f32dot(oa_ref[...], wbm_ref[...])
    mb = _f32dot(ob_ref[...], wbd_ref[...])
    merged = sg[:, :D_MODEL] * ma + sg[:, D_MODEL:] * mb
    h = x_ref[...] + _f32dot(merged.astype(jnp.bfloat16), wout_ref[...])
    h_ref[...] = h

    hn = h * _rms_scale(h, D_MODEL) * gffn_ref[...]
    _to_token_tiles(hn_ref, hn)
    hn_hi = hn.astype(jnp.bfloat16)
    hn_lo = (hn - hn_hi.astype(jnp.float32)).astype(jnp.bfloat16)
    wrh = wrh_ref[...]
    logits = _f32dot(hn_hi, wrh) + _f32dot(hn_lo, wrh) + _f32dot(hn_hi, wrl_ref[...]) + br_ref[...]

    tm = logits.shape[0]
    lane = _lane_iota(logits.shape)
    work = logits
    vals, onehots = [], []
    for _k in range(TOP_K):
        mx = jnp.max(work, axis=-1, keepdims=True)
        sel = jnp.min(jnp.where(work == mx, lane, LANES), axis=-1, keepdims=True)
        oh = lane == sel
        vals.append(mx)
        onehots.append(oh)
        work = jnp.where(oh, NEG_BIG * 2, work)
    exps = [jnp.exp(vv - vals[0]) for vv in vals]
    denom = exps[0] + exps[1] + exps[2] + exps[3]

    chosen = jnp.zeros(logits.shape, jnp.float32)
    for oh in onehots:
        chosen = chosen + oh.astype(jnp.float32)
    r_i = lax.broadcasted_iota(jnp.int32, (tm, tm), 0)
    c_i = lax.broadcasted_iota(jnp.int32, (tm, tm), 1)
    ltri = (c_i < r_i).astype(jnp.bfloat16)
    before = _f32dot(ltri, chosen.astype(jnp.bfloat16))
    tile_counts = jnp.sum(chosen, axis=0, keepdims=True)
    tbase_ref[0] = carry_scr[...]
    tcnt_ref[0] = tile_counts
    carry_scr[...] = carry_scr[...] + tile_counts
    cnt_ref[...] = carry_scr[...]

    idx_out = jnp.zeros(logits.shape, jnp.int32)
    w_out = jnp.zeros(logits.shape, jnp.float32)
    rank_out = jnp.zeros(logits.shape, jnp.int32)
    for kk in range(TOP_K):
        oh = onehots[kk]
        e_k = jnp.sum(jnp.where(oh, lane, 0), axis=-1, keepdims=True)
        r_k = jnp.sum(jnp.where(oh, before, 0.0), axis=-1, keepdims=True).astype(jnp.int32)
        idx_out = jnp.where(lane == kk, e_k, idx_out)
        rank_out = jnp.where(lane == kk, r_k, rank_out)
        w_out = jnp.where(lane == kk, exps[kk] / denom, w_out)
    idx_ref[...] = idx_out
    w_ref[...] = w_out
    rank_ref[...] = rank_out


def _route_call(x2, oa, ob, sg, wbm, wbd, wout, gffn, wrh, wrl, br):
    n = x2.shape[0]
    tm = min(ROUTE_TM, n)
    nt = n // tm
    row = lambda w: pl.BlockSpec((tm, w), lambda i: (i, 0))
    per_tile = pl.BlockSpec((1, 1, LANES), lambda i: (i, 0, 0))
    out_shape = (
        jax.ShapeDtypeStruct((n, D_MODEL), jnp.float32),
        jax.ShapeDtypeStruct((n * TOK_ROWS, LANES), jnp.float32),
        jax.ShapeDtypeStruct((n, LANES), jnp.int32),
        jax.ShapeDtypeStruct((n, LANES), jnp.float32),
        jax.ShapeDtypeStruct((n, LANES), jnp.int32),
        jax.ShapeDtypeStruct((nt, 1, LANES), jnp.float32),
        jax.ShapeDtypeStruct((nt, 1, LANES), jnp.float32),
        jax.ShapeDtypeStruct((1, LANES), jnp.float32),
    )
    return pl.pallas_call(
        _route_kernel,
        grid=(nt,),
        in_specs=[row(D_MODEL), row(512), row(512), row(2 * D_MODEL),
                  _const_spec(wbm.shape), _const_spec(wbd.shape), _const_spec(wout.shape), _const_spec(gffn.shape),
                  _const_spec(wrh.shape), _const_spec(wrl.shape), _const_spec(br.shape)],
        out_specs=(row(D_MODEL), pl.BlockSpec((tm * TOK_ROWS, LANES), lambda i: (i, 0)), row(LANES), row(LANES),
                   row(LANES), per_tile, per_tile, _const_spec((1, LANES))),
        out_shape=out_shape,
        scratch_shapes=[pltpu.VMEM((1, LANES), jnp.float32)],
        compiler_params=pltpu.CompilerParams(dimension_semantics=("arbitrary",), vmem_limit_bytes=V7X_VMEM_LIMIT),
        name="route",
    )(x2, oa, ob, sg, wbm, wbd, wout, gffn, wrh, wrl, br)


def _dispatch_kernel(seg_ref, lp_ref, pad_ref, hn_ref, xs_ref, scr, zscr, sems, zsem):
    step = pl.program_id(0)
    last = pl.num_programs(0) - 1
    tm = hn_ref.shape[0] // TOK_ROWS
    par = step % 2
    buf = scr.at[par]

    def place(t, carry):
        v = _tok(hn_ref, t)[...]
        for kk in range(TOP_K):
            _tok_at(buf, lp_ref[0, 0, t * TOP_K + kk])[...] = v
        return carry

    lax.fori_loop(0, tm, place, 0, unroll=TOK_UNROLL)

    def run(e, carry):
        _copy_token_run(buf, xs_ref, seg_ref[0, 0, N_EXPERTS + e], seg_ref[0, 0, 2 * N_EXPERTS + e],
                        seg_ref[0, 0, e], sems.at[par], tm)
        return carry

    lax.fori_loop(0, N_EXPERTS, run, 0)

    def drain(p):
        pltpu.make_async_copy(scr.at[p], xs_ref.at[pl.ds(0, scr.shape[1])], sems.at[p]).wait()

    @pl.when(step == 0)
    def _():
        zscr[...] = jnp.zeros(zscr.shape, zscr.dtype)

        def pad_run(e, carry):
            _copy_token_run(zscr, xs_ref, 0, pad_ref[e], pad_ref[N_EXPERTS + e], zsem, FFN_TM // 2)
            return carry

        lax.fori_loop(0, N_EXPERTS, pad_run, 0)

        def pad_tile(j, carry):
            d = pl.multiple_of((pad_ref[2 * N_EXPERTS] + j * FFN_TM) * TOK_ROWS, TOK_ROWS)
            pltpu.make_async_copy(zscr, xs_ref.at[pl.ds(d, FFN_TM * TOK_ROWS)], zsem).start()
            return carry

        lax.fori_loop(0, pad_ref[2 * N_EXPERTS + 1], pad_tile, 0)

    @pl.when(step > 0)
    def _():
        drain(1 - par)

    @pl.when(step == last)
    def _():
        drain(par)
        for _e in range(N_EXPERTS):
            pltpu.make_async_copy(zscr, xs_ref.at[pl.ds(0, FFN_TM * TOK_ROWS)], zsem).wait()


def _dispatch_call(seg3, lp3, pad, hn, n_rows):
    nt = seg3.shape[0]
    tm = hn.shape[0] // TOK_ROWS // nt
    smem = lambda a: pl.BlockSpec((1, 1, a.shape[2]), lambda i: (i, 0, 0), memory_space=pltpu.SMEM)
    return pl.pallas_call(
        _dispatch_kernel,
        grid=(nt,),
        in_specs=[
            smem(seg3), smem(lp3),
            pl.BlockSpec(memory_space=pltpu.SMEM),
            pl.BlockSpec((tm * TOK_ROWS, LANES), lambda i: (i, 0)),
        ],
        out_specs=pl.BlockSpec(memory_space=pl.ANY),
        out_shape=jax.ShapeDtypeStruct((n_rows * TOK_ROWS, LANES), jnp.float32),
        scratch_shapes=[pltpu.VMEM((2, TOP_K * tm * TOK_ROWS, LANES), jnp.float32),
                        pltpu.VMEM((FFN_TM * TOK_ROWS, LANES), jnp.float32),
                        pltpu.SemaphoreType.DMA((2,)), pltpu.SemaphoreType.DMA(())],
        compiler_params=pltpu.CompilerParams(dimension_semantics=("arbitrary",), vmem_limit_bytes=V7X_VMEM_LIMIT),
        name="dispatch",
    )(seg3, lp3, pad, hn)


def _ffn_kernel(te_ref, tv_ref, nt_ref, xs_ref, w1_ref, b1_ref, w2_ref, b2_ref, y_ref):
    i = pl.program_id(0)
    tm = xs_ref.shape[0] // TOK_ROWS

    @pl.when(i < nt_ref[0])
    def _():
        rows = lax.broadcasted_iota(jnp.int32, (tm, 1), 0)
        xt = jnp.where(rows < tv_ref[i], _from_token_tiles(xs_ref, tm), 0.0).astype(jnp.bfloat16)
        u = _f32dot(xt, w1_ref[0]) + b1_ref[0]
        glu = jnp.minimum(u[:, :D_EXPERT], SWIGLU_LIMIT)
        lin = jnp.clip(u[:, D_EXPERT:], -SWIGLU_LIMIT, SWIGLU_LIMIT)
        a = glu * jax.nn.sigmoid(SWIGLU_ALPHA * glu) * (lin + 1.0)
        y = _f32dot(a.astype(jnp.bfloat16), w2_ref[0]) + b2_ref[0]
        _to_token_tiles(y_ref, y)

    @pl.when(i >= nt_ref[0])
    def _():
        y_ref[...] = jnp.zeros(y_ref.shape, y_ref.dtype)


def _ffn_call(tile_expert, tile_valid, n_tiles, xs, w1, b1, w2, b2):
    n_rows = xs.shape[0] // TOK_ROWS
    tm = FFN_TM
    t_tiles = n_rows // tm

    def tile_map(i, te, tv, nt):
        return (jnp.minimum(i, nt[0] - 1), 0)

    def out_map(i, te, tv, nt):
        return (i, 0)

    def w_map(i, te, tv, nt):
        return (te[i], 0, 0)

    grid_spec = pltpu.PrefetchScalarGridSpec(
        num_scalar_prefetch=3,
        grid=(t_tiles,),
        in_specs=[
            pl.BlockSpec((tm * TOK_ROWS, LANES), tile_map),
            pl.BlockSpec((1, D_MODEL, 2 * D_EXPERT), w_map),
            pl.BlockSpec((1, 1, 2 * D_EXPERT), w_map),
            pl.BlockSpec((1, D_EXPERT, D_MODEL), w_map),
            pl.BlockSpec((1, 1, D_MODEL), w_map),
        ],
        out_specs=pl.BlockSpec((tm * TOK_ROWS, LANES), out_map),
    )
    return pl.pallas_call(
        _ffn_kernel,
        grid_spec=grid_spec,
        out_shape=jax.ShapeDtypeStruct(xs.shape, jnp.float32),
        compiler_params=pltpu.CompilerParams(dimension_semantics=("arbitrary",), vmem_limit_bytes=V7X_VMEM_LIMIT),
        name="ffn",
    )(tile_expert, tile_valid, n_tiles, xs, w1, b1, w2, b2)


def _combine_kernel(seg_ref, snext_ref, lp_ref, w_ref, h_ref, y_ref, o_ref, ybuf, tok_scr, sems):
    step = pl.program_id(0)
    last = pl.num_programs(0) - 1
    tm = h_ref.shape[0]
    par = step % 2

    def fetch(s_ref, slot):
        def run(e, carry):
            _copy_token_run(y_ref, ybuf.at[slot], s_ref[0, 0, 2 * N_EXPERTS + e], s_ref[0, 0, N_EXPERTS + e],
                            s_ref[0, 0, e], sems.at[slot], tm)
            return carry

        lax.fori_loop(0, N_EXPERTS, run, 0)

    @pl.when(step == 0)
    def _():
        fetch(seg_ref, 0)

    @pl.when(step < last)
    def _():
        fetch(snext_ref, 1 - par)

    pltpu.make_async_copy(y_ref.at[pl.ds(0, ybuf.shape[1])], ybuf.at[par], sems.at[par]).wait()
    buf = ybuf.at[par]

    def token(t, carry):
        acc = None
        for kk in range(TOP_K):
            term = w_ref[0, 0, t * TOP_K + kk] * _tok_at(buf, lp_ref[0, 0, t * TOP_K + kk])[...]
            acc = term if acc is None else acc + term
        _tok(tok_scr, t)[...] = acc
        return carry

    lax.fori_loop(0, tm, token, 0, unroll=TOK_UNROLL)
    o_ref[...] = h_ref[...] + _from_token_tiles(tok_scr, tm)


def _combine_call(seg3, lp3, w3, h, y):
    n = h.shape[0]
    nt = seg3.shape[0]
    tm = n // nt
    smem = lambda a, f: pl.BlockSpec((1, 1, a.shape[2]), f, memory_space=pltpu.SMEM)
    here = lambda i: (i, 0, 0)
    return pl.pallas_call(
        _combine_kernel,
        grid=(nt,),
        in_specs=[
            smem(seg3, here), smem(seg3, lambda i: (jnp.minimum(i + 1, nt - 1), 0, 0)), smem(lp3, here), smem(w3, here),
            pl.BlockSpec((tm, D_MODEL), lambda i: (i, 0)),
            pl.BlockSpec(memory_space=pl.ANY),
        ],
        out_specs=pl.BlockSpec((tm, D_MODEL), lambda i: (i, 0)),
        out_shape=jax.ShapeDtypeStruct((n, D_MODEL), jnp.float32),
        scratch_shapes=[pltpu.VMEM((2, TOP_K * tm * TOK_ROWS, LANES), jnp.float32),
                        pltpu.VMEM((tm * TOK_ROWS, LANES), jnp.float32), pltpu.SemaphoreType.DMA((2,))],
        compiler_params=pltpu.CompilerParams(dimension_semantics=("arbitrary",), vmem_limit_bytes=V7X_VMEM_LIMIT),
        name="combine",
    )(seg3, seg3, lp3, w3, h, y)


def _rot_cols(w, group):
    lead = w.shape[:-1]
    g = w.reshape(lead + (-1, group))
    half = group // 2
    return jnp.concatenate([-g[..., half:], g[..., :half]], axis=-1).reshape(w.shape)


def _rope_tables(s, dim):
    inv_freq = 1.0 / (ROPE_THETA ** (jnp.arange(0, dim, 2, dtype=jnp.float32) / dim))
    ang = jnp.arange(s, dtype=jnp.float32)[:, None] * inv_freq[None, :]
    return jnp.cos(ang), jnp.sin(ang)


def _mla_tables(s, gain, scale):
    cos, sin = _rope_tables(s, MLA_ROPE)
    cos2 = jnp.concatenate([cos, cos], axis=-1)
    sin2 = jnp.concatenate([sin, sin], axis=-1)
    g_nope, g_rope = gain[:MLA_NOPE], gain[MLA_NOPE:]
    g_perm = jnp.concatenate([g_rope[MLA_ROPE // 2:], g_rope[:MLA_ROPE // 2]])
    pad = jnp.zeros((s, LANES - MLA_QK), jnp.float32)
    c = jnp.concatenate([jnp.broadcast_to(g_nope, (s, MLA_NOPE)), cos2 * g_rope, pad], axis=-1) * scale
    sn = jnp.concatenate([jnp.zeros((s, MLA_NOPE), jnp.float32), sin2 * g_perm, pad], axis=-1) * scale
    return c, sn


def _diff_tables(s, gain, scale):
    cos, sin = _rope_tables(s, DIFF_D)
    cos2 = jnp.concatenate([cos, cos], axis=-1)
    sin2 = jnp.concatenate([sin, sin], axis=-1)
    g_perm = jnp.concatenate([gain[DIFF_D // 2:], gain[:DIFF_D // 2]])
    c = cos2 * gain * scale
    sn = sin2 * g_perm * scale
    return jnp.concatenate([c, c], axis=-1), jnp.concatenate([sn, sn], axis=-1)


def _pad_heads(w, width):
    kdim = w.shape[0]
    g = w.reshape(kdim, -1, width)
    g = jnp.pad(g, ((0, 0), (0, 0), (0, LANES - width)))
    return g.reshape(kdim, -1)


def kernel(x, attn_norm, w_in, mla_q_lat_norm, mla_kv_lat_norm, mla_w_uq, mla_w_ukv, mla_q_norm, mla_k_norm,
           diff_q_norm, diff_k_norm, diff_lambda_q1, diff_lambda_k1, diff_lambda_q2, diff_lambda_k2, diff_subln,
           w_branch_mla, w_branch_diff, w_out, ffn_norm, w_router, b_router, w_mlp1, b_mlp1, w_mlp2, b_mlp2):
    b, s, d = x.shape
    n = b * s
    bf = jnp.bfloat16
    f32 = jnp.float32
    i = 0

    wi = w_in[i]
    c0 = MLA_Q_LORA + MLA_KV_LORA
    w_kpe = wi[:, c0:c0 + MLA_ROPE]
    c1 = c0 + MLA_ROPE
    w_dq, w_dk, w_dv = wi[:, c1:c1 + 512], wi[:, c1 + 512:c1 + 1024], wi[:, c1 + 1024:c1 + 1536]
    w_gate = wi[:, c1 + 1536:]
    place = lambda w: jnp.pad(w, ((0, 0), (MLA_NOPE, LANES - MLA_QK)))
    w1 = jnp.concatenate([
        wi[:, :c0], place(w_kpe), place(_rot_cols(w_kpe, MLA_ROPE)),
        w_dq, _rot_cols(w_dq, DIFF_D), w_dk, _rot_cols(w_dk, DIFF_D), w_dv, w_gate], axis=1).astype(bf)

    wuq = mla_w_uq[i].reshape(MLA_Q_LORA, MLA_HEADS, MLA_QK)
    wuq_rot = jnp.concatenate(
        [jnp.zeros((MLA_Q_LORA, MLA_HEADS, MLA_NOPE), f32), _rot_cols(wuq[..., MLA_NOPE:], MLA_ROPE)], axis=-1)
    wq = jnp.concatenate([_pad_heads(wuq.reshape(MLA_Q_LORA, -1), MLA_QK),
                          _pad_heads(wuq_rot.reshape(MLA_Q_LORA, -1), MLA_QK)], axis=1).astype(bf)
    wukv = mla_w_ukv[i].reshape(MLA_KV_LORA, MLA_HEADS, MLA_NOPE + MLA_V)
    wkv = jnp.concatenate([_pad_heads(wukv[..., :MLA_NOPE].reshape(MLA_KV_LORA, -1), MLA_NOPE),
                           wukv[..., MLA_NOPE:].reshape(MLA_KV_LORA, -1)], axis=1).astype(bf)

    cq, sq = _mla_tables(s, mla_q_norm[i], LOG2E / math.sqrt(MLA_QK))
    ck, sk = _mla_tables(s, mla_k_norm[i], 1.0)
    cqd, sqd = _diff_tables(s, diff_q_norm[i], LOG2E / math.sqrt(DIFF_D))
    ckd, skd = _diff_tables(s, diff_k_norm[i], 1.0)
    tabs = jnp.stack([cq, sq, ck, sk, cqd, sqd, ckd, skd])

    glat = jnp.concatenate([mla_q_lat_norm[i], mla_kv_lat_norm[i]])[None, :]
    qa, ka, va, qd, kd, vd, sg = _prep_call(x, attn_norm[i][None, :], w1, glat, wq, wkv, tabs)

    lam = jnp.pad(jnp.stack([diff_lambda_q1[i], diff_lambda_k1[i], diff_lambda_q2[i], diff_lambda_k2[i]]),
                  ((0, 0), (0, LANES - DIFF_D)))
    subln = diff_subln[i][None, :]
    o_a = _attn_call(qa, ka, va, lam, subln, mode="mla")
    o_b = _attn_call(qd, kd, vd, lam, subln, mode="diff")

    wr = jnp.pad(w_router[i], ((0, 0), (0, LANES - N_EXPERTS)))
    wr_hi = wr.astype(bf)
    wr_lo = (wr - wr_hi.astype(f32)).astype(bf)
    br = jnp.concatenate([b_router[i], jnp.full((LANES - N_EXPERTS,), NEG_BIG, f32)])[None, :]
    h, hn, idx, gw, rank, tcnt, tbase, cnt = _route_call(
        x.reshape(n, d), o_a.reshape(n, 512), o_b.reshape(n, 512), sg.reshape(n, 2 * d),
        w_branch_mla[i].astype(bf), w_branch_diff[i].astype(bf), w_out[i].astype(bf), ffn_norm[i][None, :],
        wr_hi, wr_lo, br)

    i32 = jnp.int32
    counts = cnt[0, :N_EXPERTS].astype(i32)
    tiles_per = (counts + FFN_TM - 1) // FFN_TM
    tile_end = jnp.cumsum(tiles_per)
    tile_start = tile_end - tiles_per
    n_tiles = tile_end[-1:]
    t_tiles = n * TOP_K // FFN_TM + N_EXPERTS
    tids = jnp.arange(t_tiles, dtype=i32)
    tile_expert = jnp.minimum(jnp.sum(tids[:, None] >= tile_end[None, :], axis=1), N_EXPERTS - 1).astype(i32)
    tile_valid = jnp.clip(counts[tile_expert] - (tids - tile_start[tile_expert]) * FFN_TM, 0, FFN_TM).astype(i32)

    nt = tcnt.shape[0]
    tc = tcnt[:, 0, :N_EXPERTS].astype(i32)
    tb = tbase[:, 0, :N_EXPERTS].astype(i32)
    off = jnp.cumsum(tc, axis=1) - tc
    first = (tile_start * FFN_TM)[None, :] + tb
    seg3 = jnp.concatenate([tc, off, first], axis=1).reshape(nt, 1, 3 * N_EXPERTS)
    per = n // nt * TOP_K
    picked = idx[:, :TOP_K].reshape(nt, per, 1) == jnp.arange(N_EXPERTS, dtype=i32)
    lp3 = ((jnp.sum(jnp.where(picked, off[:, None, :], 0), axis=-1)
            + rank[:, :TOP_K].reshape(nt, per)) * TOK_ROWS).reshape(nt, 1, per)
    w3 = gw[:, :TOP_K].reshape(nt, 1, per)
    pad = jnp.concatenate([tile_start * FFN_TM + counts, tiles_per * FFN_TM - counts,
                           n_tiles * FFN_TM, t_tiles - n_tiles]).astype(i32)

    xs = _dispatch_call(seg3, lp3, pad, hn, t_tiles * FFN_TM)
    y = _ffn_call(tile_expert, tile_valid, n_tiles.astype(i32), xs,
                  w_mlp1[i].astype(bf), b_mlp1[i][:, None, :], w_mlp2[i].astype(bf), b_mlp2[i][:, None, :])
    out = _combine_call(seg3, lp3, w3, h, y)
    return out.reshape(b, s, d)
```

```python
import functools
import math

import jax
import jax.numpy as jnp
import numpy as np
from jax import lax
from jax.experimental import pallas as pl
from jax.experimental.pallas import tpu as pltpu

D_MODEL = 1024
ROPE_THETA = 10000.0
NORM_EPS = 1e-6
MLA_HEADS = 8
MLA_NOPE = 64
MLA_ROPE = 32
MLA_QK = MLA_NOPE + MLA_ROPE
MLA_V = 64
MLA_Q_LORA = 768
MLA_KV_LORA = 256
DIFF_HEADS = 4
DIFF_D = 64
DIFF_V = 128
N_EXPERTS = 32
TOP_K = 4
D_EXPERT = 1024
SWIGLU_ALPHA = 1.702
SWIGLU_LIMIT = 7.0
LAMBDA_INIT = 0.8 - 0.6 * math.exp(-0.3 * 0)

LANES = 128
V7X_VMEM_LIMIT = 56 * 1024 * 1024

PREP_TM = 256
ATT_TQ = 512
ROUTE_TM = 512
TOK_ROWS = 8
TOK_UNROLL = 8
FFN_TM = 512

NEG_BIG = -1e30
LOG2E = math.log2(math.e)

_C_LAT = 0
_C_KPE = 1024
_C_DQ = 1280
_C_DK = 2304
_C_DV = 3328
_C_GATE = 3840
_C_END = 5888


def _f32dot(a, b):
    return jnp.dot(a, b, preferred_element_type=jnp.float32)


def _dot_nt(a, b):
    return lax.dot_general(a, b, (((1,), (1,)), ((), ())), preferred_element_type=jnp.float32)


def _rms_scale(x, width):
    return lax.rsqrt(jnp.sum(x * x, axis=-1, keepdims=True) * (1.0 / width) + NORM_EPS)


def _lane_iota(shape):
    return lax.broadcasted_iota(jnp.int32, shape, len(shape) - 1)


def _prep_kernel(x_ref, gattn_ref, w1_ref, glat_ref, wq_ref, wkv_ref, tabs_ref,
                 qa_ref, ka_ref, va_ref, qd_ref, kd_ref, vd_ref, sg_ref):
    x = x_ref[0]
    xn = x * _rms_scale(x, D_MODEL) * gattn_ref[...]
    proj = _f32dot(xn.astype(jnp.bfloat16), w1_ref[...])

    glat = glat_ref[...]
    q_lat = proj[:, 0:MLA_Q_LORA]
    kv_lat = proj[:, MLA_Q_LORA:MLA_Q_LORA + MLA_KV_LORA]
    qn = q_lat * _rms_scale(q_lat, MLA_Q_LORA) * glat[:, 0:MLA_Q_LORA]
    kvn = kv_lat * _rms_scale(kv_lat, MLA_KV_LORA) * glat[:, MLA_Q_LORA:]
    qq = _f32dot(qn.astype(jnp.bfloat16), wq_ref[...])
    kk = _f32dot(kvn.astype(jnp.bfloat16), wkv_ref[...])

    kpe = proj[:, _C_KPE:_C_KPE + LANES]
    kper = proj[:, _C_KPE + LANES:_C_KPE + 2 * LANES]
    cq, sq = tabs_ref[0], tabs_ref[1]
    ck, sk = tabs_ref[2], tabs_ref[3]
    for h in range(MLA_HEADS):
        lo, hi = h * LANES, (h + 1) * LANES
        qh = qq[:, lo:hi]
        qa_ref[0, h] = (_rms_scale(qh, MLA_QK) * (qh * cq + qq[:, 1024 + lo:1024 + hi] * sq)).astype(jnp.bfloat16)
        kh = kk[:, lo:hi] + kpe
        ka_ref[0, h] = (_rms_scale(kh, MLA_QK) * (kh * ck + kper * sk)).astype(jnp.bfloat16)
    for j in range(MLA_HEADS // 2):
        va_ref[0, j, 0] = kk[:, 1024 + j * LANES:1024 + (j + 1) * LANES].T.astype(jnp.bfloat16)

    cqd, sqd = tabs_ref[4], tabs_ref[5]
    ckd, skd = tabs_ref[6], tabs_ref[7]
    lane = _lane_iota((x.shape[0], LANES))
    first = lane < DIFF_D

    def half_norm(v):
        sqv = v * v
        s_all = jnp.sum(sqv, axis=-1, keepdims=True)
        s_lo = jnp.sum(jnp.where(first, sqv, 0.0), axis=-1, keepdims=True)
        r_lo = lax.rsqrt(s_lo * (1.0 / DIFF_D) + NORM_EPS)
        r_hi = lax.rsqrt((s_all - s_lo) * (1.0 / DIFF_D) + NORM_EPS)
        return jnp.where(first, r_lo, r_hi)

    for h in range(DIFF_HEADS):
        lo, hi = h * LANES, (h + 1) * LANES
        dq = proj[:, _C_DQ + lo:_C_DQ + hi]
        dqr = proj[:, _C_DQ + 512 + lo:_C_DQ + 512 + hi]
        qv = half_norm(dq) * (dq * cqd + dqr * sqd)
        qd_ref[0, 2 * h] = jnp.where(first, qv, 0.0).astype(jnp.bfloat16)
        qd_ref[0, 2 * h + 1] = jnp.where(first, 0.0, qv).astype(jnp.bfloat16)
        dk = proj[:, _C_DK + lo:_C_DK + hi]
        dkr = proj[:, _C_DK + 512 + lo:_C_DK + 512 + hi]
        kd_ref[0, h] = (half_norm(dk) * (dk * ckd + dkr * skd)).astype(jnp.bfloat16)
        vd_ref[0, h, 0] = proj[:, _C_DV + lo:_C_DV + hi].T.astype(jnp.bfloat16)

    sg_ref[0] = jax.nn.sigmoid(proj[:, _C_GATE:_C_END]).astype(jnp.bfloat16)


def _att_tiles(s):
    tq = min(ATT_TQ, s)
    return tq, tq


def _const_spec(shape):
    nd = len(shape)
    return pl.BlockSpec(shape, lambda *_: (0,) * nd)


def _prep_call(x, gattn, w1, glat, wq, wkv, tabs):
    b, s, _ = x.shape
    tm = min(PREP_TM, s)
    grid = (b, s // tm)
    head_out = lambda nh: pl.BlockSpec((1, nh, tm, LANES), lambda bi, ti: (bi, 0, ti, 0))
    _, tk = _att_tiles(s)
    per = tk // tm
    vt_out = lambda nh: pl.BlockSpec((1, nh, 1, LANES, tm), lambda bi, ti: (bi, 0, ti // per, 0, ti % per))
    bf = jnp.bfloat16
    out_shape = (
        jax.ShapeDtypeStruct((b, MLA_HEADS, s, LANES), bf),
        jax.ShapeDtypeStruct((b, MLA_HEADS, s, LANES), bf),
        jax.ShapeDtypeStruct((b, MLA_HEADS // 2, s // tk, LANES, tk), bf),
        jax.ShapeDtypeStruct((b, 2 * DIFF_HEADS, s, LANES), bf),
        jax.ShapeDtypeStruct((b, DIFF_HEADS, s, LANES), bf),
        jax.ShapeDtypeStruct((b, DIFF_HEADS, s // tk, LANES, tk), bf),
        jax.ShapeDtypeStruct((b, s, 2 * D_MODEL), bf),
    )
    return pl.pallas_call(
        _prep_kernel,
        grid=grid,
        in_specs=[
            pl.BlockSpec((1, tm, D_MODEL), lambda bi, ti: (bi, ti, 0)),
            _const_spec(gattn.shape),
            _const_spec(w1.shape),
            _const_spec(glat.shape),
            _const_spec(wq.shape),
            _const_spec(wkv.shape),
            pl.BlockSpec((8, tm, LANES), lambda bi, ti: (0, ti, 0)),
        ],
        out_specs=(
            head_out(MLA_HEADS), head_out(MLA_HEADS), vt_out(MLA_HEADS // 2),
            head_out(2 * DIFF_HEADS), head_out(DIFF_HEADS), vt_out(DIFF_HEADS),
            pl.BlockSpec((1, tm, 2 * D_MODEL), lambda bi, ti: (bi, ti, 0)),
        ),
        out_shape=out_shape,
        compiler_params=pltpu.CompilerParams(
            dimension_semantics=("arbitrary", "arbitrary"), vmem_limit_bytes=V7X_VMEM_LIMIT),
        name="prep",
    )(x, gattn, w1, glat, wq, wkv, tabs)


def _attn_kernel(q_ref, k_ref, vt_ref, lam_ref, subln_ref, o_ref, st_buf, mx_buf, m_scr, l_scr, acc_scr, *, mode, tq):
    nq = q_ref.shape[2] // tq
    shared_k = k_ref.shape[1] == 1
    DIAG = 2

    def scores(qr, kb, slot, masked):
        q0 = pl.multiple_of(qr * tq, tq)
        k0 = pl.multiple_of(kb * tq, tq)
        for hh in range(2):
            k = k_ref[0, 0 if shared_k else hh, pl.ds(k0, tq), :]
            st = _dot_nt(k, q_ref[0, hh, pl.ds(q0, tq), :])
            if masked:
                key = lax.broadcasted_iota(jnp.int32, st.shape, 0)
                qry = lax.broadcasted_iota(jnp.int32, st.shape, 1)
                st = jnp.where(key <= qry, st, NEG_BIG)
            st_buf[slot, hh] = st
            mx_buf[slot, hh] = jnp.max(st, axis=0, keepdims=True)

    def softmax_pv(kb, slot):
        vt = vt_ref[0, 0, kb]
        for hh in range(2):
            m_prev = m_scr[hh]
            m_new = jnp.maximum(m_prev, mx_buf[slot, hh])
            alpha = jnp.exp2(m_prev - m_new)
            p = jnp.exp2(st_buf[slot, hh] - m_new)
            l_scr[hh] = alpha * l_scr[hh] + jnp.sum(p, axis=0, keepdims=True)
            acc_scr[hh] = alpha * acc_scr[hh] + _f32dot(vt, p.astype(jnp.bfloat16))
            m_scr[hh] = m_new

    def reset():
        m_scr[...] = jnp.full(m_scr.shape, NEG_BIG, jnp.float32)
        l_scr[...] = jnp.zeros(l_scr.shape, jnp.float32)
        acc_scr[...] = jnp.zeros(acc_scr.shape, jnp.float32)

    def finalize(qr):
        oa = acc_scr[0] / l_scr[0]
        ob = acc_scr[1] / l_scr[1]
        rows = pl.ds(pl.multiple_of(qr * tq, tq), tq)
        if mode == "mla":
            sub = lax.broadcasted_iota(jnp.int32, oa.shape, 0)
            o_ref[0, rows, :] = jnp.where(sub < MLA_V, oa, ob).T.astype(o_ref.dtype)
        else:
            lp = lam_ref[...]
            lam = (jnp.exp(jnp.sum(lp[0:1] * lp[1:2], axis=-1, keepdims=True))
                   - jnp.exp(jnp.sum(lp[2:3] * lp[3:4], axis=-1, keepdims=True)) + LAMBDA_INIT)
            o = oa - lam * ob
            o = o * lax.rsqrt(jnp.sum(o * o, axis=0, keepdims=True) * (1.0 / DIFF_V) + NORM_EPS)
            o_ref[0, rows, :] = (o.T * subln_ref[...] * (1.0 - LAMBDA_INIT)).astype(o_ref.dtype)

    def next_diag(qr):
        nxt = jnp.minimum(qr + 1, nq - 1)
        scores(nxt, nxt, DIAG, True)

    reset()
    scores(0, 0, DIAG, True)
    softmax_pv(0, DIAG)
    finalize(0)
    if nq > 1:
        scores(1, 1, DIAG, True)

    def row(qr, carry):
        reset()
        scores(qr, 0, 0, False)
        softmax_pv(qr, DIAG)
        rest = qr - 1

        def pair(j, c):
            scores(qr, 2 * j + 1, 1, False)
            softmax_pv(2 * j, 0)
            scores(qr, 2 * j + 2, 0, False)
            softmax_pv(2 * j + 1, 1)
            return c

        lax.fori_loop(0, rest // 2, pair, 0)

        @pl.when(rest % 2 == 0)
        def _():
            softmax_pv(qr - 1, 0)
            next_diag(qr)

        @pl.when(rest % 2 == 1)
        def _():
            scores(qr, qr - 1, 1, False)
            softmax_pv(qr - 2, 0)
            softmax_pv(qr - 1, 1)
            next_diag(qr)

        finalize(qr)
        return carry

    lax.fori_loop(1, nq, row, 0)


def _attn_call(q, k, vt, lam, subln, *, mode):
    b, _, s, _ = q.shape
    n_groups = vt.shape[1]
    kh = k.shape[1] // n_groups
    tq, tk = _att_tiles(s)
    assert tk == tq
    kernel = functools.partial(_attn_kernel, mode=mode, tq=tq)
    return pl.pallas_call(
        kernel,
        grid=(b, n_groups),
        in_specs=[
            pl.BlockSpec((1, 2, s, LANES), lambda bi, g: (bi, g, 0, 0)),
            pl.BlockSpec((1, kh, s, LANES), lambda bi, g: (bi, g, 0, 0)),
            pl.BlockSpec((1, 1, s // tk, LANES, tk), lambda bi, g: (bi, g, 0, 0, 0)),
            _const_spec(lam.shape),
            _const_spec(subln.shape),
        ],
        out_specs=pl.BlockSpec((1, s, LANES), lambda bi, g: (bi, 0, g)),
        out_shape=jax.ShapeDtypeStruct((b, s, n_groups * LANES), jnp.bfloat16),
        scratch_shapes=[
            pltpu.VMEM((3, 2, tq, tq), jnp.float32),
            pltpu.VMEM((3, 2, 1, tq), jnp.float32),
            pltpu.VMEM((2, 1, tq), jnp.float32),
            pltpu.VMEM((2, 1, tq), jnp.float32),
            pltpu.VMEM((2, LANES, tq), jnp.float32),
        ],
        compiler_params=pltpu.CompilerParams(
            dimension_semantics=("arbitrary", "arbitrary"), vmem_limit_bytes=V7X_VMEM_LIMIT),
        name="attn_" + mode,
    )(q, k, vt, lam, subln)


def _to_token_tiles(ref, mat):
    n = mat.shape[0]
    for j in range(TOK_ROWS):
        ref[pl.ds(j, n, stride=TOK_ROWS), :] = mat[:, j * LANES:(j + 1) * LANES]


def _from_token_tiles(ref, n):
    return jnp.concatenate([ref[pl.ds(j, n, stride=TOK_ROWS), :] for j in range(TOK_ROWS)], axis=1)


def _tok_at(ref, row):
    return ref.at[pl.ds(pl.multiple_of(row, TOK_ROWS), TOK_ROWS)]


def _tok(ref, t):
    return _tok_at(ref, t * TOK_ROWS)


def _copy_token_run(src, dst, src_tok, dst_tok, count, sem, top_bit):
    bit = top_bit
    while bit >= 1:
        done = count & ~(2 * bit - 1)

        @pl.when((count & bit) != 0)
        def _(bit=bit, done=done):
            s = pl.multiple_of((src_tok + done) * TOK_ROWS, TOK_ROWS)
            d = pl.multiple_of((dst_tok + done) * TOK_ROWS, TOK_ROWS)
            pltpu.make_async_copy(src.at[pl.ds(s, bit * TOK_ROWS)], dst.at[pl.ds(d, bit * TOK_ROWS)], sem).start()

        bit //= 2


def _route_kernel(x_ref, oa_ref, ob_ref, sg_ref, wbm_ref, wbd_ref, wout_ref, gffn_ref, wrh_ref, wrl_ref, br_ref,
                  h_ref, hn_ref, idx_ref, w_ref, rank_ref, tcnt_ref, tbase_ref, cnt_ref, carry_scr):
    @pl.when(pl.program_id(0) == 0)
    def _():
        carry_scr[...] = jnp.zeros(carry_scr.shape, jnp.float32)

    sg = sg_ref[...]
    ma = _f32dot(oa_ref[...], wbm_ref[...])
    mb = _f32dot(ob_ref[...], wbd_ref[...])
    merged = sg[:, :D_MODEL] * ma + sg[:, D_MODEL:] * mb
    h = x_ref[...] + _f32dot(merged.astype(jnp.bfloat16), wout_ref[...])
    h_ref[...] = h

    hn = h * _rms_scale(h, D_MODEL) * gffn_ref[...]
    _to_token_tiles(hn_ref, hn)
    hn_hi = hn.astype(jnp.bfloat16)
    hn_lo = (hn - hn_hi.astype(jnp.float32)).astype(jnp.bfloat16)
    wrh = wrh_ref[...]
    logits = _f32dot(hn_hi, wrh) + _f32dot(hn_lo, wrh) + _f32dot(hn_hi, wrl_ref[...]) + br_ref[...]

    tm = logits.shape[0]
    lane = _lane_iota(logits.shape)
    work = logits
    vals, onehots = [], []
    for _k in range(TOP_K):
        mx = jnp.max(work, axis=-1, keepdims=True)
        sel = jnp.min(jnp.where(work == mx, lane, LANES), axis=-1, keepdims=True)
        oh = lane == sel
        vals.append(mx)
        onehots.append(oh)
        work = jnp.where(oh, NEG_BIG * 2, work)
    exps = [jnp.exp(vv - vals[0]) for vv in vals]
    denom = exps[0] + exps[1] + exps[2] + exps[3]

    chosen = jnp.zeros(logits.shape, jnp.float32)
    for oh in onehots:
        chosen = chosen + oh.astype(jnp.float32)
    r_i = lax.broadcasted_iota(jnp.int32, (tm, tm), 0)
    c_i = lax.broadcasted_iota(jnp.int32, (tm, tm), 1)
    ltri = (c_i < r_i).astype(jnp.bfloat16)
    before = _f32dot(ltri, chosen.astype(jnp.bfloat16))
    tile_counts = jnp.sum(chosen, axis=0, keepdims=True)
    tbase_ref[0] = carry_scr[...]
    tcnt_ref[0] = tile_counts
    carry_scr[...] = carry_scr[...] + tile_counts
    cnt_ref[...] = carry_scr[...]

    idx_out = jnp.zeros(logits.shape, jnp.int32)
    w_out = jnp.zeros(logits.shape, jnp.float32)
    rank_out = jnp.zeros(logits.shape, jnp.int32)
    for kk in range(TOP_K):
        oh = onehots[kk]
        e_k = jnp.sum(jnp.where(oh, lane, 0), axis=-1, keepdims=True)
        r_k = jnp.sum(jnp.where(oh, before, 0.0), axis=-1, keepdims=True).astype(jnp.int32)
        idx_out = jnp.where(lane == kk, e_k, idx_out)
        rank_out = jnp.where(lane == kk, r_k, rank_out)
        w_out = jnp.where(lane == kk, exps[kk] / denom, w_out)
    idx_ref[...] = idx_out
    w_ref[...] = w_out
    rank_ref[...] = rank_out


def _route_call(x2, oa, ob, sg, wbm, wbd, wout, gffn, wrh, wrl, br):
    n = x2.shape[0]
    tm = min(ROUTE_TM, n)
    nt = n // tm
    row = lambda w: pl.BlockSpec((tm, w), lambda i: (i, 0))
    per_tile = pl.BlockSpec((1, 1, LANES), lambda i: (i, 0, 0))
    out_shape = (
        jax.ShapeDtypeStruct((n, D_MODEL), jnp.float32),
        jax.ShapeDtypeStruct((n * TOK_ROWS, LANES), jnp.float32),
        jax.ShapeDtypeStruct((n, LANES), jnp.int32),
        jax.ShapeDtypeStruct((n, LANES), jnp.float32),
        jax.ShapeDtypeStruct((n, LANES), jnp.int32),
        jax.ShapeDtypeStruct((nt, 1, LANES), jnp.float32),
        jax.ShapeDtypeStruct((nt, 1, LANES), jnp.float32),
        jax.ShapeDtypeStruct((1, LANES), jnp.float32),
    )
    return pl.pallas_call(
        _route_kernel,
        grid=(nt,),
        in_specs=[row(D_MODEL), row(512), row(512), row(2 * D_MODEL),
                  _const_spec(wbm.shape), _const_spec(wbd.shape), _const_spec(wout.shape), _const_spec(gffn.shape),
                  _const_spec(wrh.shape), _const_spec(wrl.shape), _const_spec(br.shape)],
        out_specs=(row(D_MODEL), pl.BlockSpec((tm * TOK_ROWS, LANES), lambda i: (i, 0)), row(LANES), row(LANES),
                   row(LANES), per_tile, per_tile, _const_spec((1, LANES))),
        out_shape=out_shape,
        scratch_shapes=[pltpu.VMEM((1, LANES), jnp.float32)],
        compiler_params=pltpu.CompilerParams(dimension_semantics=("arbitrary",), vmem_limit_bytes=V7X_VMEM_LIMIT),
        name="route",
    )(x2, oa, ob, sg, wbm, wbd, wout, gffn, wrh, wrl, br)


def _dispatch_kernel(seg_ref, lp_ref, pad_ref, hn_ref, xs_ref, scr, zscr, sems, zsem):
    step = pl.program_id(0)
    last = pl.num_programs(0) - 1
    tm = hn_ref.shape[0] // TOK_ROWS
    par = step % 2
    buf = scr.at[par]

    def place(t, carry):
        v = _tok(hn_ref, t)[...]
        for kk in range(TOP_K):
            _tok_at(buf, lp_ref[0, 0, t * TOP_K + kk])[...] = v
        return carry

    lax.fori_loop(0, tm, place, 0, unroll=TOK_UNROLL)

    def run(e, carry):
        _copy_token_run(buf, xs_ref, seg_ref[0, 0, N_EXPERTS + e], seg_ref[0, 0, 2 * N_EXPERTS + e],
                        seg_ref[0, 0, e], sems.at[par], tm)
        return carry

    lax.fori_loop(0, N_EXPERTS, run, 0)

    def drain(p):
        pltpu.make_async_copy(scr.at[p], xs_ref.at[pl.ds(0, scr.shape[1])], sems.at[p]).wait()

    @pl.when(step == 0)
    def _():
        zscr[...] = jnp.zeros(zscr.shape, zscr.dtype)

        def pad_run(e, carry):
            _copy_token_run(zscr, xs_ref, 0, pad_ref[e], pad_ref[N_EXPERTS + e], zsem, FFN_TM // 2)
            return carry

        lax.fori_loop(0, N_EXPERTS, pad_run, 0)

        def pad_tile(j, carry):
            d = pl.multiple_of((pad_ref[2 * N_EXPERTS] + j * FFN_TM) * TOK_ROWS, TOK_ROWS)
            pltpu.make_async_copy(zscr, xs_ref.at[pl.ds(d, FFN_TM * TOK_ROWS)], zsem).start()
            return carry

        lax.fori_loop(0, pad_ref[2 * N_EXPERTS + 1], pad_tile, 0)

    @pl.when(step > 0)
    def _():
        drain(1 - par)

    @pl.when(step == last)
    def _():
        drain(par)
        for _e in range(N_EXPERTS):
            pltpu.make_async_copy(zscr, xs_ref.at[pl.ds(0, FFN_TM * TOK_ROWS)], zsem).wait()


def _dispatch_call(seg3, lp3, pad, hn, n_rows):
    nt = seg3.shape[0]
    tm = hn.shape[0] // TOK_ROWS // nt
    smem = lambda a: pl.BlockSpec((1, 1, a.shape[2]), lambda i: (i, 0, 0), memory_space=pltpu.SMEM)
    return pl.pallas_call(
        _dispatch_kernel,
        grid=(nt,),
        in_specs=[
            smem(seg3), smem(lp3),
            pl.BlockSpec(memory_space=pltpu.SMEM),
            pl.BlockSpec((tm * TOK_ROWS, LANES), lambda i: (i, 0)),
        ],
        out_specs=pl.BlockSpec(memory_space=pl.ANY),
        out_shape=jax.ShapeDtypeStruct((n_rows * TOK_ROWS, LANES), jnp.float32),
        scratch_shapes=[pltpu.VMEM((2, TOP_K * tm * TOK_ROWS, LANES), jnp.float32),
                        pltpu.VMEM((FFN_TM * TOK_ROWS, LANES), jnp.float32),
                        pltpu.SemaphoreType.DMA((2,)), pltpu.SemaphoreType.DMA(())],
        compiler_params=pltpu.CompilerParams(dimension_semantics=("arbitrary",), vmem_limit_bytes=V7X_VMEM_LIMIT),
        name="dispatch",
    )(seg3, lp3, pad, hn)


def _ffn_kernel(te_ref, tv_ref, nt_ref, xs_ref, w1_ref, b1_ref, w2_ref, b2_ref, y_ref):
    i = pl.program_id(0)
    tm = xs_ref.shape[0] // TOK_ROWS

    @pl.when(i < nt_ref[0])
    def _():
        rows = lax.broadcasted_iota(jnp.int32, (tm, 1), 0)
        xt = jnp.where(rows < tv_ref[i], _from_token_tiles(xs_ref, tm), 0.0).astype(jnp.bfloat16)
        u = _f32dot(xt, w1_ref[0]) + b1_ref[0]
        glu = jnp.minimum(u[:, :D_EXPERT], SWIGLU_LIMIT)
        lin = jnp.clip(u[:, D_EXPERT:], -SWIGLU_LIMIT, SWIGLU_LIMIT)
        a = glu * jax.nn.sigmoid(SWIGLU_ALPHA * glu) * (lin + 1.0)
        y = _f32dot(a.astype(jnp.bfloat16), w2_ref[0]) + b2_ref[0]
        _to_token_tiles(y_ref, y)

    @pl.when(i >= nt_ref[0])
    def _():
        y_ref[...] = jnp.zeros(y_ref.shape, y_ref.dtype)


def _ffn_call(tile_expert, tile_valid, n_tiles, xs, w1, b1, w2, b2):
    n_rows = xs.shape[0] // TOK_ROWS
    tm = FFN_TM
    t_tiles = n_rows // tm

    def tile_map(i, te, tv, nt):
        return (jnp.minimum(i, nt[0] - 1), 0)

    def out_map(i, te, tv, nt):
        return (i, 0)

    def w_map(i, te, tv, nt):
        return (te[i], 0, 0)

    grid_spec = pltpu.PrefetchScalarGridSpec(
        num_scalar_prefetch=3,
        grid=(t_tiles,),
        in_specs=[
            pl.BlockSpec((tm * TOK_ROWS, LANES), tile_map),
            pl.BlockSpec((1, D_MODEL, 2 * D_EXPERT), w_map),
            pl.BlockSpec((1, 1, 2 * D_EXPERT), w_map),
            pl.BlockSpec((1, D_EXPERT, D_MODEL), w_map),
            pl.BlockSpec((1, 1, D_MODEL), w_map),
        ],
        out_specs=pl.BlockSpec((tm * TOK_ROWS, LANES), out_map),
    )
    return pl.pallas_call(
        _ffn_kernel,
        grid_spec=grid_spec,
        out_shape=jax.ShapeDtypeStruct(xs.shape, jnp.float32),
        compiler_params=pltpu.CompilerParams(dimension_semantics=("arbitrary",), vmem_limit_bytes=V7X_VMEM_LIMIT),
        name="ffn",
    )(tile_expert, tile_valid, n_tiles, xs, w1, b1, w2, b2)


def _combine_kernel(seg_ref, snext_ref, lp_ref, w_ref, h_ref, y_ref, o_ref, ybuf, tok_scr, sems):
    step = pl.program_id(0)
    last = pl.num_programs(0) - 1
    tm = h_ref.shape[0]
    par = step % 2

    def fetch(s_ref, slot):
        def run(e, carry):
            _copy_token_run(y_ref, ybuf.at[slot], s_ref[0, 0, 2 * N_EXPERTS + e], s_ref[0, 0, N_EXPERTS + e],
                            s_ref[0, 0, e], sems.at[slot], tm)
            return carry

        lax.fori_loop(0, N_EXPERTS, run, 0)

    @pl.when(step == 0)
    def _():
        fetch(seg_ref, 0)

    @pl.when(step < last)
    def _():
        fetch(snext_ref, 1 - par)

    pltpu.make_async_copy(y_ref.at[pl.ds(0, ybuf.shape[1])], ybuf.at[par], sems.at[par]).wait()
    buf = ybuf.at[par]

    def token(t, carry):
        acc = None
        for kk in range(TOP_K):
            term = w_ref[0, 0, t * TOP_K + kk] * _tok_at(buf, lp_ref[0, 0, t * TOP_K + kk])[...]
            acc = term if acc is None else acc + term
        _tok(tok_scr, t)[...] = acc
        return carry

    lax.fori_loop(0, tm, token, 0, unroll=TOK_UNROLL)
    o_ref[...] = h_ref[...] + _from_token_tiles(tok_scr, tm)


def _combine_call(seg3, lp3, w3, h, y):
    n = h.shape[0]
    nt = seg3.shape[0]
    tm = n // nt
    smem = lambda a, f: pl.BlockSpec((1, 1, a.shape[2]), f, memory_space=pltpu.SMEM)
    here = lambda i: (i, 0, 0)
    return pl.pallas_call(
        _combine_kernel,
        grid=(nt,),
        in_specs=[
            smem(seg3, here), smem(seg3, lambda i: (jnp.minimum(i + 1, nt - 1), 0, 0)), smem(lp3, here), smem(w3, here),
            pl.BlockSpec((tm, D_MODEL), lambda i: (i, 0)),
            pl.BlockSpec(memory_space=pl.ANY),
        ],
        out_specs=pl.BlockSpec((tm, D_MODEL), lambda i: (i, 0)),
        out_shape=jax.ShapeDtypeStruct((n, D_MODEL), jnp.float32),
        scratch_shapes=[pltpu.VMEM((2, TOP_K * tm * TOK_ROWS, LANES), jnp.float32),
                        pltpu.VMEM((tm * TOK_ROWS, LANES), jnp.float32), pltpu.SemaphoreType.DMA((2,))],
        compiler_params=pltpu.CompilerParams(dimension_semantics=("arbitrary",), vmem_limit_bytes=V7X_VMEM_LIMIT),
        name="combine",
    )(seg3, seg3, lp3, w3, h, y)


def _rot_cols(w, group):
    lead = w.shape[:-1]
    g = w.reshape(lead + (-1, group))
    half = group // 2
    return jnp.concatenate([-g[..., half:], g[..., :half]], axis=-1).reshape(w.shape)


def _rope_tables(s, dim):
    inv_freq = 1.0 / (ROPE_THETA ** (jnp.arange(0, dim, 2, dtype=jnp.float32) / dim))
    ang = jnp.arange(s, dtype=jnp.float32)[:, None] * inv_freq[None, :]
    return jnp.cos(ang), jnp.sin(ang)


def _mla_tables(s, gain, scale):
    cos, sin = _rope_tables(s, MLA_ROPE)
    cos2 = jnp.concatenate([cos, cos], axis=-1)
    sin2 = jnp.concatenate([sin, sin], axis=-1)
    g_nope, g_rope = gain[:MLA_NOPE], gain[MLA_NOPE:]
    g_perm = jnp.concatenate([g_rope[MLA_ROPE // 2:], g_rope[:MLA_ROPE // 2]])
    pad = jnp.zeros((s, LANES - MLA_QK), jnp.float32)
    c = jnp.concatenate([jnp.broadcast_to(g_nope, (s, MLA_NOPE)), cos2 * g_rope, pad], axis=-1) * scale
    sn = jnp.concatenate([jnp.zeros((s, MLA_NOPE), jnp.float32), sin2 * g_perm, pad], axis=-1) * scale
    return c, sn


def _diff_tables(s, gain, scale):
    cos, sin = _rope_tables(s, DIFF_D)
    cos2 = jnp.concatenate([cos, cos], axis=-1)
    sin2 = jnp.concatenate([sin, sin], axis=-1)
    g_perm = jnp.concatenate([gain[DIFF_D // 2:], gain[:DIFF_D // 2]])
    c = cos2 * gain * scale
    sn = sin2 * g_perm * scale
    return jnp.concatenate([c, c], axis=-1), jnp.concatenate([sn, sn], axis=-1)


def _pad_heads(w, width):
    kdim = w.shape[0]
    g = w.reshape(kdim, -1, width)
    g = jnp.pad(g, ((0, 0), (0, 0), (0, LANES - width)))
    return g.reshape(kdim, -1)


def kernel(x, attn_norm, w_in, mla_q_lat_norm, mla_kv_lat_norm, mla_w_uq, mla_w_ukv, mla_q_norm, mla_k_norm,
           diff_q_norm, diff_k_norm, diff_lambda_q1, diff_lambda_k1, diff_lambda_q2, diff_lambda_k2, diff_subln,
           w_branch_mla, w_branch_diff, w_out, ffn_norm, w_router, b_router, w_mlp1, b_mlp1, w_mlp2, b_mlp2):
    b, s, d = x.shape
    n = b * s
    bf = jnp.bfloat16
    f32 = jnp.float32
    i = 0

    wi = w_in[i]
    c0 = MLA_Q_LORA + MLA_KV_LORA
    w_kpe = wi[:, c0:c0 + MLA_ROPE]
    c1 = c0 + MLA_ROPE
    w_dq, w_dk, w_dv = wi[:, c1:c1 + 512], wi[:, c1 + 512:c1 + 1024], wi[:, c1 + 1024:c1 + 1536]
    w_gate = wi[:, c1 + 1536:]
    place = lambda w: jnp.pad(w, ((0, 0), (MLA_NOPE, LANES - MLA_QK)))
    w1 = jnp.concatenate([
        wi[:, :c0], place(w_kpe), place(_rot_cols(w_kpe, MLA_ROPE)),
        w_dq, _rot_cols(w_dq, DIFF_D), w_dk, _rot_cols(w_dk, DIFF_D), w_dv, w_gate], axis=1).astype(bf)

    wuq = mla_w_uq[i].reshape(MLA_Q_LORA, MLA_HEADS, MLA_QK)
    wuq_rot = jnp.concatenate(
        [jnp.zeros((MLA_Q_LORA, MLA_HEADS, MLA_NOPE), f32), _rot_cols(wuq[..., MLA_NOPE:], MLA_ROPE)], axis=-1)
    wq = jnp.concatenate([_pad_heads(wuq.reshape(MLA_Q_LORA, -1), MLA_QK),
                          _pad_heads(wuq_rot.reshape(MLA_Q_LORA, -1), MLA_QK)], axis=1).astype(bf)
    wukv = mla_w_ukv[i].reshape(MLA_KV_LORA, MLA_HEADS, MLA_NOPE + MLA_V)
    wkv = jnp.concatenate([_pad_heads(wukv[..., :MLA_NOPE].reshape(MLA_KV_LORA, -1), MLA_NOPE),
                           wukv[..., MLA_NOPE:].reshape(MLA_KV_LORA, -1)], axis=1).astype(bf)

    cq, sq = _mla_tables(s, mla_q_norm[i], LOG2E / math.sqrt(MLA_QK))
    ck, sk = _mla_tables(s, mla_k_norm[i], 1.0)
    cqd, sqd = _diff_tables(s, diff_q_norm[i], LOG2E / math.sqrt(DIFF_D))
    ckd, skd = _diff_tables(s, diff_k_norm[i], 1.0)
    tabs = jnp.stack([cq, sq, ck, sk, cqd, sqd, ckd, skd])

    glat = jnp.concatenate([mla_q_lat_norm[i], mla_kv_lat_norm[i]])[None, :]
    qa, ka, va, qd, kd, vd, sg = _prep_call(x, attn_norm[i][None, :], w1, glat, wq, wkv, tabs)

    lam = jnp.pad(jnp.stack([diff_lambda_q1[i], diff_lambda_k1[i], diff_lambda_q2[i], diff_lambda_k2[i]]),
                  ((0, 0), (0, LANES - DIFF_D)))
    subln = diff_subln[i][None, :]
    o_a = _attn_call(qa, ka, va, lam, subln, mode="mla")
    o_b = _attn_call(qd, kd, vd, lam, subln, mode="diff")

    wr = jnp.pad(w_router[i], ((0, 0), (0, LANES - N_EXPERTS)))
    wr_hi = wr.astype(bf)
    wr_lo = (wr - wr_hi.astype(f32)).astype(bf)
    br = jnp.concatenate([b_router[i], jnp.full((LANES - N_EXPERTS,), NEG_BIG, f32)])[None, :]
    h, hn, idx, gw, rank, tcnt, tbase, cnt = _route_call(
        x.reshape(n, d), o_a.reshape(n, 512), o_b.reshape(n, 512), sg.reshape(n, 2 * d),
        w_branch_mla[i].astype(bf), w_branch_diff[i].astype(bf), w_out[i].astype(bf), ffn_norm[i][None, :],
        wr_hi, wr_lo, br)

    i32 = jnp.int32
    counts = cnt[0, :N_EXPERTS].astype(i32)
    tiles_per = (counts + FFN_TM - 1) // FFN_TM
    tile_end = jnp.cumsum(tiles_per)
    tile_start = tile_end - tiles_per
    n_tiles = tile_end[-1:]
    t_tiles = n * TOP_K // FFN_TM + N_EXPERTS
    tids = jnp.arange(t_tiles, dtype=i32)
    tile_expert = jnp.minimum(jnp.sum(tids[:, None] >= tile_end[None, :], axis=1), N_EXPERTS - 1).astype(i32)
    tile_valid = jnp.clip(counts[tile_expert] - (tids - tile_start[tile_expert]) * FFN_TM, 0, FFN_TM).astype(i32)

    nt = tcnt.shape[0]
    tc = tcnt[:, 0, :N_EXPERTS].astype(i32)
    tb = tbase[:, 0, :N_EXPERTS].astype(i32)
    off = jnp.cumsum(tc, axis=1) - tc
    first = (tile_start * FFN_TM)[None, :] + tb
    seg3 = jnp.concatenate([tc, off, first], axis=1).reshape(nt, 1, 3 * N_EXPERTS)
    per = n // nt * TOP_K
    picked = idx[:, :TOP_K].reshape(nt, per, 1) == jnp.arange(N_EXPERTS, dtype=i32)
    lp3 = ((jnp.sum(jnp.where(picked, off[:, None, :], 0), axis=-1)
            + rank[:, :TOP_K].reshape(nt, per)) * TOK_ROWS).reshape(nt, 1, per)
    w3 = gw[:, :TOP_K].reshape(nt, 1, per)
    pad = jnp.concatenate([tile_start * FFN_TM + counts, tiles_per * FFN_TM - counts,
                           n_tiles * FFN_TM, t_tiles - n_tiles]).astype(i32)

    xs = _dispatch_call(seg3, lp3, pad, hn, t_tiles * FFN_TM)
    y = _ffn_call(tile_expert, tile_valid, n_tiles.astype(i32), xs,
                  w_mlp1[i].astype(bf), b_mlp1[i][:, None, :], w_mlp2[i].astype(bf), b_mlp2[i][:, None, :])
    out = _combine_call(seg3, lp3, w3, h, y)
    return out.reshape(b, s, d)
```

```python
import functools
import math

import jax
import jax.numpy as jnp
import numpy as np
from jax import lax
from jax.experimental import pallas as pl
from jax.experimental.pallas import tpu as pltpu

D_MODEL = 1024
ROPE_THETA = 10000.0
NORM_EPS = 1e-6
MLA_HEADS = 8
MLA_NOPE = 64
MLA_ROPE = 32
MLA_QK = MLA_NOPE + MLA_ROPE
MLA_V = 64
MLA_Q_LORA = 768
MLA_KV_LORA = 256
DIFF_HEADS = 4
DIFF_D = 64
DIFF_V = 128
N_EXPERTS = 32
TOP_K = 4
D_EXPERT = 1024
SWIGLU_ALPHA = 1.702
SWIGLU_LIMIT = 7.0
LAMBDA_INIT = 0.8 - 0.6 * math.exp(-0.3 * 0)

LANES = 128
V7X_VMEM_LIMIT = 56 * 1024 * 1024

PREP_TM = 256
ATT_TQ = 512
ROUTE_TM = 512
VT_ROWS = 144
TOK_ROWS = 8
TOK_UNROLL = 8
FFN_TM = 512

NEG_BIG = -1e30
LOG2E = math.log2(math.e)

_C_LAT = 0
_C_KPE = 1024
_C_DQ = 1280
_C_DK = 2304
_C_DV = 3328
_C_GATE = 3840
_C_END = 5888


def _f32dot(a, b):
    return jnp.dot(a, b, preferred_element_type=jnp.float32)


def _dot_nt(a, b):
    return lax.dot_general(a, b, (((1,), (1,)), ((), ())), preferred_element_type=jnp.float32)


def _rms_scale(x, width):
    return lax.rsqrt(jnp.sum(x * x, axis=-1, keepdims=True) * (1.0 / width) + NORM_EPS)


def _lane_iota(shape):
    return lax.broadcasted_iota(jnp.int32, shape, len(shape) - 1)


def _prep_kernel(x_ref, gattn_ref, w1_ref, glat_ref, wq_ref, wkv_ref, tabs_ref,
                 qa_ref, ka_ref, va_ref, qd_ref, kd_ref, vd_ref, sg_ref):
    x = x_ref[0]
    xn = x * _rms_scale(x, D_MODEL) * gattn_ref[...]
    proj = _f32dot(xn.astype(jnp.bfloat16), w1_ref[...])

    glat = glat_ref[...]
    q_lat = proj[:, 0:MLA_Q_LORA]
    kv_lat = proj[:, MLA_Q_LORA:MLA_Q_LORA + MLA_KV_LORA]
    qn = q_lat * _rms_scale(q_lat, MLA_Q_LORA) * glat[:, 0:MLA_Q_LORA]
    kvn = kv_lat * _rms_scale(kv_lat, MLA_KV_LORA) * glat[:, MLA_Q_LORA:]
    qq = _f32dot(qn.astype(jnp.bfloat16), wq_ref[...])
    kk = _f32dot(kvn.astype(jnp.bfloat16), wkv_ref[...])

    kpe = proj[:, _C_KPE:_C_KPE + LANES]
    kper = proj[:, _C_KPE + LANES:_C_KPE + 2 * LANES]
    cq, sq = tabs_ref[0], tabs_ref[1]
    ck, sk = tabs_ref[2], tabs_ref[3]
    for h in range(MLA_HEADS):
        lo, hi = h * LANES, (h + 1) * LANES
        qh = qq[:, lo:hi]
        qa_ref[0, h] = (_rms_scale(qh, MLA_QK) * (qh * cq + qq[:, 1024 + lo:1024 + hi] * sq)).astype(jnp.bfloat16)
        kh = kk[:, lo:hi] + kpe
        ka_ref[0, h] = (_rms_scale(kh, MLA_QK) * (kh * ck + kper * sk)).astype(jnp.bfloat16)
    ones_rows = (lax.broadcasted_iota(jnp.int32, (VT_ROWS - LANES, x.shape[0]), 0) == 0).astype(jnp.bfloat16)
    for j in range(MLA_HEADS // 2):
        va_ref[0, j, 0, :LANES, :] = kk[:, 1024 + j * LANES:1024 + (j + 1) * LANES].T.astype(jnp.bfloat16)
        va_ref[0, j, 0, LANES:, :] = ones_rows

    cqd, sqd = tabs_ref[4], tabs_ref[5]
    ckd, skd = tabs_ref[6], tabs_ref[7]
    lane = _lane_iota((x.shape[0], LANES))
    first = lane < DIFF_D

    def half_norm(v):
        sqv = v * v
        s_all = jnp.sum(sqv, axis=-1, keepdims=True)
        s_lo = jnp.sum(jnp.where(first, sqv, 0.0), axis=-1, keepdims=True)
        r_lo = lax.rsqrt(s_lo * (1.0 / DIFF_D) + NORM_EPS)
        r_hi = lax.rsqrt((s_all - s_lo) * (1.0 / DIFF_D) + NORM_EPS)
        return jnp.where(first, r_lo, r_hi)

    for h in range(DIFF_HEADS):
        lo, hi = h * LANES, (h + 1) * LANES
        dq = proj[:, _C_DQ + lo:_C_DQ + hi]
        dqr = proj[:, _C_DQ + 512 + lo:_C_DQ + 512 + hi]
        qv = half_norm(dq) * (dq * cqd + dqr * sqd)
        qd_ref[0, 2 * h] = jnp.where(first, qv, 0.0).astype(jnp.bfloat16)
        qd_ref[0, 2 * h + 1] = jnp.where(first, 0.0, qv).astype(jnp.bfloat16)
        dk = proj[:, _C_DK + lo:_C_DK + hi]
        dkr = proj[:, _C_DK + 512 + lo:_C_DK + 512 + hi]
        kd_ref[0, h] = (half_norm(dk) * (dk * ckd + dkr * skd)).astype(jnp.bfloat16)
        vd_ref[0, h, 0, :LANES, :] = proj[:, _C_DV + lo:_C_DV + hi].T.astype(jnp.bfloat16)
        vd_ref[0, h, 0, LANES:, :] = ones_rows

    sg_ref[0] = jax.nn.sigmoid(proj[:, _C_GATE:_C_END]).astype(jnp.bfloat16)


def _att_tiles(s):
    tq = min(ATT_TQ, s)
    return tq, tq


def _const_spec(shape):
    nd = len(shape)
    return pl.BlockSpec(shape, lambda *_: (0,) * nd)


def _prep_call(x, gattn, w1, glat, wq, wkv, tabs):
    b, s, _ = x.shape
    tm = min(PREP_TM, s)
    grid = (b, s // tm)
    head_out = lambda nh: pl.BlockSpec((1, nh, tm, LANES), lambda bi, ti: (bi, 0, ti, 0))
    _, tk = _att_tiles(s)
    per = tk // tm
    vt_out = lambda nh: pl.BlockSpec((1, nh, 1, VT_ROWS, tm), lambda bi, ti: (bi, 0, ti // per, 0, ti % per))
    bf = jnp.bfloat16
    out_shape = (
        jax.ShapeDtypeStruct((b, MLA_HEADS, s, LANES), bf),
        jax.ShapeDtypeStruct((b, MLA_HEADS, s, LANES), bf),
        jax.ShapeDtypeStruct((b, MLA_HEADS // 2, s // tk, VT_ROWS, tk), bf),
        jax.ShapeDtypeStruct((b, 2 * DIFF_HEADS, s, LANES), bf),
        jax.ShapeDtypeStruct((b, DIFF_HEADS, s, LANES), bf),
        jax.ShapeDtypeStruct((b, DIFF_HEADS, s // tk, VT_ROWS, tk), bf),
        jax.ShapeDtypeStruct((b, s, 2 * D_MODEL), bf),
    )
    return pl.pallas_call(
        _prep_kernel,
        grid=grid,
        in_specs=[
            pl.BlockSpec((1, tm, D_MODEL), lambda bi, ti: (bi, ti, 0)),
            _const_spec(gattn.shape),
            _const_spec(w1.shape),
            _const_spec(glat.shape),
            _const_spec(wq.shape),
            _const_spec(wkv.shape),
            pl.BlockSpec((8, tm, LANES), lambda bi, ti: (0, ti, 0)),
        ],
        out_specs=(
            head_out(MLA_HEADS), head_out(MLA_HEADS), vt_out(MLA_HEADS // 2),
            head_out(2 * DIFF_HEADS), head_out(DIFF_HEADS), vt_out(DIFF_HEADS),
            pl.BlockSpec((1, tm, 2 * D_MODEL), lambda bi, ti: (bi, ti, 0)),
        ),
        out_shape=out_shape,
        compiler_params=pltpu.CompilerParams(
            dimension_semantics=("arbitrary", "arbitrary"), vmem_limit_bytes=V7X_VMEM_LIMIT),
        name="prep",
    )(x, gattn, w1, glat, wq, wkv, tabs)


def _attn_kernel(q_ref, k_ref, vt_ref, lam_ref, subln_ref, o_ref, st_buf, mx_buf, m_scr, acc_scr, *, mode, tq):
    nq = q_ref.shape[2] // tq
    shared_k = k_ref.shape[1] == 1
    DIAG = 2

    def scores(qr, kb, slot, masked):
        q0 = pl.multiple_of(qr * tq, tq)
        k0 = pl.multiple_of(kb * tq, tq)
        for hh in range(2):
            k = k_ref[0, 0 if shared_k else hh, pl.ds(k0, tq), :]
            st = _dot_nt(k, q_ref[0, hh, pl.ds(q0, tq), :])
            if masked:
                key = lax.broadcasted_iota(jnp.int32, st.shape, 0)
                qry = lax.broadcasted_iota(jnp.int32, st.shape, 1)
                st = jnp.where(key <= qry, st, NEG_BIG)
            st_buf[slot, hh] = st
            mx_buf[slot, hh] = jnp.max(st, axis=0, keepdims=True)

    def softmax_pv(kb, slot):
        vt = vt_ref[0, 0, kb]
        for hh in range(2):
            m_prev = m_scr[hh]
            m_new = jnp.maximum(m_prev, mx_buf[slot, hh])
            alpha = jnp.exp2(m_prev - m_new)
            p = jnp.exp2(st_buf[slot, hh] - m_new)
            acc_scr[hh] = alpha * acc_scr[hh] + _f32dot(vt, p.astype(jnp.bfloat16))
            m_scr[hh] = m_new

    def reset():
        m_scr[...] = jnp.full(m_scr.shape, NEG_BIG, jnp.float32)
        acc_scr[...] = jnp.zeros(acc_scr.shape, jnp.float32)

    def finalize(qr):
        oa = acc_scr[0, :LANES] / acc_scr[0, LANES:LANES + 1]
        ob = acc_scr[1, :LANES] / acc_scr[1, LANES:LANES + 1]
        rows = pl.ds(pl.multiple_of(qr * tq, tq), tq)
        if mode == "mla":
            sub = lax.broadcasted_iota(jnp.int32, oa.shape, 0)
            o_ref[0, rows, :] = jnp.where(sub < MLA_V, oa, ob).T.astype(o_ref.dtype)
        else:
            lp = lam_ref[...]
            lam = (jnp.exp(jnp.sum(lp[0:1] * lp[1:2], axis=-1, keepdims=True))
                   - jnp.exp(jnp.sum(lp[2:3] * lp[3:4], axis=-1, keepdims=True)) + LAMBDA_INIT)
            o = oa - lam * ob
            o = o * lax.rsqrt(jnp.sum(o * o, axis=0, keepdims=True) * (1.0 / DIFF_V) + NORM_EPS)
            o_ref[0, rows, :] = (o.T * subln_ref[...] * (1.0 - LAMBDA_INIT)).astype(o_ref.dtype)

    def next_diag(qr):
        nxt = jnp.minimum(qr + 1, nq - 1)
        scores(nxt, nxt, DIAG, True)

    reset()
    scores(0, 0, DIAG, True)
    softmax_pv(0, DIAG)
    finalize(0)
    if nq > 1:
        scores(1, 1, DIAG, True)

    def row(qr, carry):
        reset()
        scores(qr, 0, 0, False)
        softmax_pv(qr, DIAG)
        rest = qr - 1

        def pair(j, c):
            scores(qr, 2 * j + 1, 1, False)
            softmax_pv(2 * j, 0)
            scores(qr, 2 * j + 2, 0, False)
            softmax_pv(2 * j + 1, 1)
            return c

        lax.fori_loop(0, rest // 2, pair, 0)

        @pl.when(rest % 2 == 0)
        def _():
            softmax_pv(qr - 1, 0)
            next_diag(qr)

        @pl.when(rest % 2 == 1)
        def _():
            scores(qr, qr - 1, 1, False)
            softmax_pv(qr - 2, 0)
            softmax_pv(qr - 1, 1)
            next_diag(qr)

        finalize(qr)
        return carry

    lax.fori_loop(1, nq, row, 0)


def _attn_call(q, k, vt, lam, subln, *, mode):
    b, _, s, _ = q.shape
    n_groups = vt.shape[1]
    kh = k.shape[1] // n_groups
    tq, tk = _att_tiles(s)
    assert tk == tq
    kernel = functools.partial(_attn_kernel, mode=mode, tq=tq)
    return pl.pallas_call(
        kernel,
        grid=(b, n_groups),
        in_specs=[
            pl.BlockSpec((1, 2, s, LANES), lambda bi, g: (bi, g, 0, 0)),
            pl.BlockSpec((1, kh, s, LANES), lambda bi, g: (bi, g, 0, 0)),
            pl.BlockSpec((1, 1, s // tk, VT_ROWS, tk), lambda bi, g: (bi, g, 0, 0, 0)),
            _const_spec(lam.shape),
            _const_spec(subln.shape),
        ],
        out_specs=pl.BlockSpec((1, s, LANES), lambda bi, g: (bi, 0, g)),
        out_shape=jax.ShapeDtypeStruct((b, s, n_groups * LANES), jnp.bfloat16),
        scratch_shapes=[
            pltpu.VMEM((3, 2, tq, tq), jnp.float32),
            pltpu.VMEM((3, 2, 1, tq), jnp.float32),
            pltpu.VMEM((2, 1, tq), jnp.float32),
            pltpu.VMEM((2, VT_ROWS, tq), jnp.float32),
        ],
        compiler_params=pltpu.CompilerParams(
            dimension_semantics=("arbitrary", "arbitrary"), vmem_limit_bytes=V7X_VMEM_LIMIT),
        name="attn_" + mode,
    )(q, k, vt, lam, subln)


def _to_token_tiles(ref, mat):
    n = mat.shape[0]
    for j in range(TOK_ROWS):
        ref[pl.ds(j, n, stride=TOK_ROWS), :] = mat[:, j * LANES:(j + 1) * LANES]


def _from_token_tiles(ref, n):
    return jnp.concatenate([ref[pl.ds(j, n, stride=TOK_ROWS), :] for j in range(TOK_ROWS)], axis=1)


def _tok_at(ref, row):
    return ref.at[pl.ds(pl.multiple_of(row, TOK_ROWS), TOK_ROWS)]


def _tok(ref, t):
    return _tok_at(ref, t * TOK_ROWS)


def _copy_token_run(src, dst, src_tok, dst_tok, count, sem, top_bit):
    bit = top_bit
    while bit >= 1:
        done = count & ~(2 * bit - 1)

        @pl.when((count & bit) != 0)
        def _(bit=bit, done=done):
            s = pl.multiple_of((src_tok + done) * TOK_ROWS, TOK_ROWS)
            d = pl.multiple_of((dst_tok + done) * TOK_ROWS, TOK_ROWS)
            pltpu.make_async_copy(src.at[pl.ds(s, bit * TOK_ROWS)], dst.at[pl.ds(d, bit * TOK_ROWS)], sem).start()

        bit //= 2


def _route_kernel(x_ref, oa_ref, ob_ref, sg_ref, wbm_ref, wbd_ref, wout_ref, gffn_ref, wrh_ref, wrl_ref, br_ref,
                  h_ref, hn_ref, idx_ref, w_ref, rank_ref, tcnt_ref, tbase_ref, cnt_ref, carry_scr):
    @pl.when(pl.program_id(0) == 0)
    def _():
        carry_scr[...] = jnp.zeros(carry_scr.shape, jnp.float32)

    sg = sg_ref[...]
    ma = _f32dot(oa_ref[...], wbm_ref[...])
    mb = _f32dot(ob_ref[...], wbd_ref[...])
    merged = sg[:, :D_MODEL] * ma + sg[:, D_MODEL:] * mb
    h = x_ref[...] + _f32dot(merged.astype(jnp.bfloat16), wout_ref[...])
    h_ref[...] = h

    hn = h * _rms_scale(h, D_MODEL) * gffn_ref[...]
    _to_token_tiles(hn_ref, hn)
    hn_hi = hn.astype(jnp.bfloat16)
    hn_lo = (hn - hn_hi.astype(jnp.float32)).astype(jnp.bfloat16)
    wrh = wrh_ref[...]
    logits = _f32dot(hn_hi, wrh) + _f32dot(hn_lo, wrh) + _f32dot(hn_hi, wrl_ref[...]) + br_ref[...]

    tm = logits.shape[0]
    lane = _lane_iota(logits.shape)
    work = logits
    vals, picks, onehots = [], [], []
    for _k in range(TOP_K):
        mx = jnp.max(work, axis=-1, keepdims=True)
        sel = jnp.min(jnp.where(work == mx, lane, LANES), axis=-1, keepdims=True)
        oh = lane == sel
        vals.append(mx)
        picks.append(sel)
        onehots.append(oh)
        work = jnp.where(oh, NEG_BIG * 2, work)
    exps = [jnp.exp(vv - vals[0]) for vv in vals]
    denom = exps[0] + exps[1] + exps[2] + exps[3]

    chosen = jnp.zeros(logits.shape, jnp.float32)
    for oh in onehots:
        chosen = chosen + oh.astype(jnp.float32)
    r_i = lax.broadcasted_iota(jnp.int32, (tm, tm), 0)
    c_i = lax.broadcasted_iota(jnp.int32, (tm, tm), 1)
    ltri = (c_i < r_i).astype(jnp.bfloat16)
    before = _f32dot(ltri, chosen.astype(jnp.bfloat16))
    tile_counts = jnp.sum(chosen, axis=0, keepdims=True)
    tbase_ref[0] = carry_scr[...]
    tcnt_ref[0] = tile_counts
    carry_scr[...] = carry_scr[...] + tile_counts
    cnt_ref[...] = carry_scr[...]

    idx_out = jnp.zeros(logits.shape, jnp.int32)
    w_out = jnp.zeros(logits.shape, jnp.float32)
    rank_out = jnp.zeros(logits.shape, jnp.int32)
    for kk in range(TOP_K):
        oh = onehots[kk]
        r_k = jnp.sum(jnp.where(oh, before, 0.0), axis=-1, keepdims=True).astype(jnp.int32)
        idx_out = jnp.where(lane == kk, picks[kk], idx_out)
        rank_out = jnp.where(lane == kk, r_k, rank_out)
        w_out = jnp.where(lane == kk, exps[kk] / denom, w_out)
    idx_ref[...] = idx_out
    w_ref[...] = w_out
    rank_ref[...] = rank_out


def _route_call(x2, oa, ob, sg, wbm, wbd, wout, gffn, wrh, wrl, br):
    n = x2.shape[0]
    tm = min(ROUTE_TM, n)
    nt = n // tm
    row = lambda w: pl.BlockSpec((tm, w), lambda i: (i, 0))
    per_tile = pl.BlockSpec((1, 1, LANES), lambda i: (i, 0, 0))
    out_shape = (
        jax.ShapeDtypeStruct((n, D_MODEL), jnp.float32),
        jax.ShapeDtypeStruct((n * TOK_ROWS, LANES), jnp.float32),
        jax.ShapeDtypeStruct((n, LANES), jnp.int32),
        jax.ShapeDtypeStruct((n, LANES), jnp.float32),
        jax.ShapeDtypeStruct((n, LANES), jnp.int32),
        jax.ShapeDtypeStruct((nt, 1, LANES), jnp.float32),
        jax.ShapeDtypeStruct((nt, 1, LANES), jnp.float32),
        jax.ShapeDtypeStruct((1, LANES), jnp.float32),
    )
    return pl.pallas_call(
        _route_kernel,
        grid=(nt,),
        in_specs=[row(D_MODEL), row(512), row(512), row(2 * D_MODEL),
                  _const_spec(wbm.shape), _const_spec(wbd.shape), _const_spec(wout.shape), _const_spec(gffn.shape),
                  _const_spec(wrh.shape), _const_spec(wrl.shape), _const_spec(br.shape)],
        out_specs=(row(D_MODEL), pl.BlockSpec((tm * TOK_ROWS, LANES), lambda i: (i, 0)), row(LANES), row(LANES),
                   row(LANES), per_tile, per_tile, _const_spec((1, LANES))),
        out_shape=out_shape,
        scratch_shapes=[pltpu.VMEM((1, LANES), jnp.float32)],
        compiler_params=pltpu.CompilerParams(dimension_semantics=("arbitrary",), vmem_limit_bytes=V7X_VMEM_LIMIT),
        name="route",
    )(x2, oa, ob, sg, wbm, wbd, wout, gffn, wrh, wrl, br)


def _dispatch_kernel(seg_ref, lp_ref, pad_ref, hn_ref, xs_ref, scr, zscr, sems, zsem):
    step = pl.program_id(0)
    last = pl.num_programs(0) - 1
    tm = hn_ref.shape[0] // TOK_ROWS
    par = step % 2
    buf = scr.at[par]

    def place(t, carry):
        v = _tok(hn_ref, t)[...]
        for kk in range(TOP_K):
            _tok_at(buf, lp_ref[0, 0, t * TOP_K + kk])[...] = v
        return carry

    lax.fori_loop(0, tm, place, 0, unroll=TOK_UNROLL)

    def run(e, carry):
        _copy_token_run(buf, xs_ref, seg_ref[0, 0, N_EXPERTS + e], seg_ref[0, 0, 2 * N_EXPERTS + e],
                        seg_ref[0, 0, e], sems.at[par], tm)
        return carry

    lax.fori_loop(0, N_EXPERTS, run, 0)

    def drain(p):
        pltpu.make_async_copy(scr.at[p], xs_ref.at[pl.ds(0, scr.shape[1])], sems.at[p]).wait()

    @pl.when(step == 0)
    def _():
        zscr[...] = jnp.zeros(zscr.shape, zscr.dtype)

        def pad_run(e, carry):
            _copy_token_run(zscr, xs_ref, 0, pad_ref[e], pad_ref[N_EXPERTS + e], zsem, FFN_TM // 2)
            return carry

        lax.fori_loop(0, N_EXPERTS, pad_run, 0)

        def pad_tile(j, carry):
            d = pl.multiple_of((pad_ref[2 * N_EXPERTS] + j * FFN_TM) * TOK_ROWS, TOK_ROWS)
            pltpu.make_async_copy(zscr, xs_ref.at[pl.ds(d, FFN_TM * TOK_ROWS)], zsem).start()
            return carry

        lax.fori_loop(0, pad_ref[2 * N_EXPERTS + 1], pad_tile, 0)

    @pl.when(step > 0)
    def _():
        drain(1 - par)

    @pl.when(step == last)
    def _():
        drain(par)
        for _e in range(N_EXPERTS):
            pltpu.make_async_copy(zscr, xs_ref.at[pl.ds(0, FFN_TM * TOK_ROWS)], zsem).wait()


def _dispatch_call(seg3, lp3, pad, hn, n_rows):
    nt = seg3.shape[0]
    tm = hn.shape[0] // TOK_ROWS // nt
    smem = lambda a: pl.BlockSpec((1, 1, a.shape[2]), lambda i: (i, 0, 0), memory_space=pltpu.SMEM)
    return pl.pallas_call(
        _dispatch_kernel,
        grid=(nt,),
        in_specs=[
            smem(seg3), smem(lp3),
            pl.BlockSpec(memory_space=pltpu.SMEM),
            pl.BlockSpec((tm * TOK_ROWS, LANES), lambda i: (i, 0)),
        ],
        out_specs=pl.BlockSpec(memory_space=pl.ANY),
        out_shape=jax.ShapeDtypeStruct((n_rows * TOK_ROWS, LANES), jnp.float32),
        scratch_shapes=[pltpu.VMEM((2, TOP_K * tm * TOK_ROWS, LANES), jnp.float32),
                        pltpu.VMEM((FFN_TM * TOK_ROWS, LANES), jnp.float32),
                        pltpu.SemaphoreType.DMA((2,)), pltpu.SemaphoreType.DMA(())],
        compiler_params=pltpu.CompilerParams(dimension_semantics=("arbitrary",), vmem_limit_bytes=V7X_VMEM_LIMIT),
        name="dispatch",
    )(seg3, lp3, pad, hn)


def _ffn_kernel(te_ref, tv_ref, nt_ref, xs_ref, w1_ref, b1_ref, w2_ref, b2_ref, y_ref, w1b, w2b):
    i = pl.program_id(0)
    tm = xs_ref.shape[0] // TOK_ROWS

    @pl.when((i == 0) | (te_ref[i] != te_ref[jnp.maximum(i - 1, 0)]))
    def _():
        w1b[...] = w1_ref[0].astype(jnp.bfloat16)
        w2b[...] = w2_ref[0].astype(jnp.bfloat16)

    @pl.when(i < nt_ref[0])
    def _():
        rows = lax.broadcasted_iota(jnp.int32, (tm, 1), 0)
        xt = jnp.where(rows < tv_ref[i], _from_token_tiles(xs_ref, tm), 0.0).astype(jnp.bfloat16)
        u = _f32dot(xt, w1b[...]) + b1_ref[0]
        glu = jnp.minimum(u[:, :D_EXPERT], SWIGLU_LIMIT)
        lin = jnp.clip(u[:, D_EXPERT:], -SWIGLU_LIMIT, SWIGLU_LIMIT)
        a = glu * jax.nn.sigmoid(SWIGLU_ALPHA * glu) * (lin + 1.0)
        y = _f32dot(a.astype(jnp.bfloat16), w2b[...]) + b2_ref[0]
        _to_token_tiles(y_ref, y)

    @pl.when(i >= nt_ref[0])
    def _():
        y_ref[...] = jnp.zeros(y_ref.shape, y_ref.dtype)


def _ffn_call(tile_expert, tile_valid, n_tiles, xs, w1, b1, w2, b2):
    n_rows = xs.shape[0] // TOK_ROWS
    tm = FFN_TM
    t_tiles = n_rows // tm

    def tile_map(i, te, tv, nt):
        return (jnp.minimum(i, nt[0] - 1), 0)

    def out_map(i, te, tv, nt):
        return (i, 0)

    def w_map(i, te, tv, nt):
        return (te[i], 0, 0)

    grid_spec = pltpu.PrefetchScalarGridSpec(
        num_scalar_prefetch=3,
        grid=(t_tiles,),
        in_specs=[
            pl.BlockSpec((tm * TOK_ROWS, LANES), tile_map),
            pl.BlockSpec((1, D_MODEL, 2 * D_EXPERT), w_map),
            pl.BlockSpec((1, 1, 2 * D_EXPERT), w_map),
            pl.BlockSpec((1, D_EXPERT, D_MODEL), w_map),
            pl.BlockSpec((1, 1, D_MODEL), w_map),
        ],
        out_specs=pl.BlockSpec((tm * TOK_ROWS, LANES), out_map),
        scratch_shapes=[pltpu.VMEM((D_MODEL, 2 * D_EXPERT), jnp.bfloat16), pltpu.VMEM((D_EXPERT, D_MODEL), jnp.bfloat16)],
    )
    return pl.pallas_call(
        _ffn_kernel,
        grid_spec=grid_spec,
        out_shape=jax.ShapeDtypeStruct(xs.shape, jnp.float32),
        compiler_params=pltpu.CompilerParams(dimension_semantics=("arbitrary",), vmem_limit_bytes=V7X_VMEM_LIMIT),
        name="ffn",
    )(tile_expert, tile_valid, n_tiles, xs, w1, b1, w2, b2)


def _combine_kernel(seg_ref, snext_ref, lp_ref, w_ref, h_ref, y_ref, o_ref, ybuf, tok_scr, sems):
    step = pl.program_id(0)
    last = pl.num_programs(0) - 1
    tm = h_ref.shape[0]
    par = step % 2

    def fetch(s_ref, slot):
        def run(e, carry):
            _copy_token_run(y_ref, ybuf.at[slot], s_ref[0, 0, 2 * N_EXPERTS + e], s_ref[0, 0, N_EXPERTS + e],
                            s_ref[0, 0, e], sems.at[slot], tm)
            return carry

        lax.fori_loop(0, N_EXPERTS, run, 0)

    @pl.when(step == 0)
    def _():
        fetch(seg_ref, 0)

    @pl.when(step < last)
    def _():
        fetch(snext_ref, 1 - par)

    pltpu.make_async_copy(y_ref.at[pl.ds(0, ybuf.shape[1])], ybuf.at[par], sems.at[par]).wait()
    buf = ybuf.at[par]

    def token(t, carry):
        acc = None
        for kk in range(TOP_K):
            term = w_ref[0, 0, t * TOP_K + kk] * _tok_at(buf, lp_ref[0, 0, t * TOP_K + kk])[...]
            acc = term if acc is None else acc + term
        _tok(tok_scr, t)[...] = acc
        return carry

    lax.fori_loop(0, tm, token, 0, unroll=TOK_UNROLL)
    o_ref[...] = h_ref[...] + _from_token_tiles(tok_scr, tm)


def _combine_call(seg3, lp3, w3, h, y):
    n = h.shape[0]
    nt = seg3.shape[0]
    tm = n // nt
    smem = lambda a, f: pl.BlockSpec((1, 1, a.shape[2]), f, memory_space=pltpu.SMEM)
    here = lambda i: (i, 0, 0)
    return pl.pallas_call(
        _combine_kernel,
        grid=(nt,),
        in_specs=[
            smem(seg3, here), smem(seg3, lambda i: (jnp.minimum(i + 1, nt - 1), 0, 0)), smem(lp3, here), smem(w3, here),
            pl.BlockSpec((tm, D_MODEL), lambda i: (i, 0)),
            pl.BlockSpec(memory_space=pl.ANY),
        ],
        out_specs=pl.BlockSpec((tm, D_MODEL), lambda i: (i, 0)),
        out_shape=jax.ShapeDtypeStruct((n, D_MODEL), jnp.float32),
        scratch_shapes=[pltpu.VMEM((2, TOP_K * tm * TOK_ROWS, LANES), jnp.float32),
                        pltpu.VMEM((tm * TOK_ROWS, LANES), jnp.float32), pltpu.SemaphoreType.DMA((2,))],
        compiler_params=pltpu.CompilerParams(dimension_semantics=("arbitrary",), vmem_limit_bytes=V7X_VMEM_LIMIT),
        name="combine",
    )(seg3, seg3, lp3, w3, h, y)


def _rot_cols(w, group):
    lead = w.shape[:-1]
    g = w.reshape(lead + (-1, group))
    half = group // 2
    return jnp.concatenate([-g[..., half:], g[..., :half]], axis=-1).reshape(w.shape)


def _rope_tables(s, dim):
    inv_freq = 1.0 / (ROPE_THETA ** (jnp.arange(0, dim, 2, dtype=jnp.float32) / dim))
    ang = jnp.arange(s, dtype=jnp.float32)[:, None] * inv_freq[None, :]
    return jnp.cos(ang), jnp.sin(ang)


def _mla_tables(s, gain, scale):
    cos, sin = _rope_tables(s, MLA_ROPE)
    cos2 = jnp.concatenate([cos, cos], axis=-1)
    sin2 = jnp.concatenate([sin, sin], axis=-1)
    g_nope, g_rope = gain[:MLA_NOPE], gain[MLA_NOPE:]
    g_perm = jnp.concatenate([g_rope[MLA_ROPE // 2:], g_rope[:MLA_ROPE // 2]])
    pad = jnp.zeros((s, LANES - MLA_QK), jnp.float32)
    c = jnp.concatenate([jnp.broadcast_to(g_nope, (s, MLA_NOPE)), cos2 * g_rope, pad], axis=-1) * scale
    sn = jnp.concatenate([jnp.zeros((s, MLA_NOPE), jnp.float32), sin2 * g_perm, pad], axis=-1) * scale
    return c, sn


def _diff_tables(s, gain, scale):
    cos, sin = _rope_tables(s, DIFF_D)
    cos2 = jnp.concatenate([cos, cos], axis=-1)
    sin2 = jnp.concatenate([sin, sin], axis=-1)
    g_perm = jnp.concatenate([gain[DIFF_D // 2:], gain[:DIFF_D // 2]])
    c = cos2 * gain * scale
    sn = sin2 * g_perm * scale
    return jnp.concatenate([c, c], axis=-1), jnp.concatenate([sn, sn], axis=-1)


def _pad_heads(w, width):
    kdim = w.shape[0]
    g = w.reshape(kdim, -1, width)
    g = jnp.pad(g, ((0, 0), (0, 0), (0, LANES - width)))
    return g.reshape(kdim, -1)


def kernel(x, attn_norm, w_in, mla_q_lat_norm, mla_kv_lat_norm, mla_w_uq, mla_w_ukv, mla_q_norm, mla_k_norm,
           diff_q_norm, diff_k_norm, diff_lambda_q1, diff_lambda_k1, diff_lambda_q2, diff_lambda_k2, diff_subln,
           w_branch_mla, w_branch_diff, w_out, ffn_norm, w_router, b_router, w_mlp1, b_mlp1, w_mlp2, b_mlp2):
    b, s, d = x.shape
    n = b * s
    bf = jnp.bfloat16
    f32 = jnp.float32
    i = 0

    wi = w_in[i]
    c0 = MLA_Q_LORA + MLA_KV_LORA
    w_kpe = wi[:, c0:c0 + MLA_ROPE]
    c1 = c0 + MLA_ROPE
    w_dq, w_dk, w_dv = wi[:, c1:c1 + 512], wi[:, c1 + 512:c1 + 1024], wi[:, c1 + 1024:c1 + 1536]
    w_gate = wi[:, c1 + 1536:]
    place = lambda w: jnp.pad(w, ((0, 0), (MLA_NOPE, LANES - MLA_QK)))
    w1 = jnp.concatenate([
        wi[:, :c0], place(w_kpe), place(_rot_cols(w_kpe, MLA_ROPE)),
        w_dq, _rot_cols(w_dq, DIFF_D), w_dk, _rot_cols(w_dk, DIFF_D), w_dv, w_gate], axis=1).astype(bf)

    wuq = mla_w_uq[i].reshape(MLA_Q_LORA, MLA_HEADS, MLA_QK)
    wuq_rot = jnp.concatenate(
        [jnp.zeros((MLA_Q_LORA, MLA_HEADS, MLA_NOPE), f32), _rot_cols(wuq[..., MLA_NOPE:], MLA_ROPE)], axis=-1)
    wq = jnp.concatenate([_pad_heads(wuq.reshape(MLA_Q_LORA, -1), MLA_QK),
                          _pad_heads(wuq_rot.reshape(MLA_Q_LORA, -1), MLA_QK)], axis=1).astype(bf)
    wukv = mla_w_ukv[i].reshape(MLA_KV_LORA, MLA_HEADS, MLA_NOPE + MLA_V)
    wkv = jnp.concatenate([_pad_heads(wukv[..., :MLA_NOPE].reshape(MLA_KV_LORA, -1), MLA_NOPE),
                           wukv[..., MLA_NOPE:].reshape(MLA_KV_LORA, -1)], axis=1).astype(bf)

    cq, sq = _mla_tables(s, mla_q_norm[i], LOG2E / math.sqrt(MLA_QK))
    ck, sk = _mla_tables(s, mla_k_norm[i], 1.0)
    cqd, sqd = _diff_tables(s, diff_q_norm[i], LOG2E / math.sqrt(DIFF_D))
    ckd, skd = _diff_tables(s, diff_k_norm[i], 1.0)
    tabs = jnp.stack([cq, sq, ck, sk, cqd, sqd, ckd, skd])

    glat = jnp.concatenate([mla_q_lat_norm[i], mla_kv_lat_norm[i]])[None, :]
    qa, ka, va, qd, kd, vd, sg = _prep_call(x, attn_norm[i][None, :], w1, glat, wq, wkv, tabs)

    lam = jnp.pad(jnp.stack([diff_lambda_q1[i], diff_lambda_k1[i], diff_lambda_q2[i], diff_lambda_k2[i]]),
                  ((0, 0), (0, LANES - DIFF_D)))
    subln = diff_subln[i][None, :]
    o_a = _attn_call(qa, ka, va, lam, subln, mode="mla")
    o_b = _attn_call(qd, kd, vd, lam, subln, mode="diff")

    wr = jnp.pad(w_router[i], ((0, 0), (0, LANES - N_EXPERTS)))
    wr_hi = wr.astype(bf)
    wr_lo = (wr - wr_hi.astype(f32)).astype(bf)
    br = jnp.concatenate([b_router[i], jnp.full((LANES - N_EXPERTS,), NEG_BIG, f32)])[None, :]
    h, hn, idx, gw, rank, tcnt, tbase, cnt = _route_call(
        x.reshape(n, d), o_a.reshape(n, 512), o_b.reshape(n, 512), sg.reshape(n, 2 * d),
        w_branch_mla[i].astype(bf), w_branch_diff[i].astype(bf), w_out[i].astype(bf), ffn_norm[i][None, :],
        wr_hi, wr_lo, br)

    i32 = jnp.int32
    counts = cnt[0, :N_EXPERTS].astype(i32)
    tiles_per = (counts + FFN_TM - 1) // FFN_TM
    tile_end = jnp.cumsum(tiles_per)
    tile_start = tile_end - tiles_per
    n_tiles = tile_end[-1:]
    t_tiles = n * TOP_K // FFN_TM + N_EXPERTS
    tids = jnp.arange(t_tiles, dtype=i32)
    tile_expert = jnp.minimum(jnp.sum(tids[:, None] >= tile_end[None, :], axis=1), N_EXPERTS - 1).astype(i32)
    tile_valid = jnp.clip(counts[tile_expert] - (tids - tile_start[tile_expert]) * FFN_TM, 0, FFN_TM).astype(i32)

    nt = tcnt.shape[0]
    tc = tcnt[:, 0, :N_EXPERTS].astype(i32)
    tb = tbase[:, 0, :N_EXPERTS].astype(i32)
    off = jnp.cumsum(tc, axis=1) - tc
    first = (tile_start * FFN_TM)[None, :] + tb
    seg3 = jnp.concatenate([tc, off, first], axis=1).reshape(nt, 1, 3 * N_EXPERTS)
    per = n // nt * TOP_K
    picked = idx[:, :TOP_K].reshape(nt, per, 1) == jnp.arange(N_EXPERTS, dtype=i32)
    lp3 = ((jnp.sum(jnp.where(picked, off[:, None, :], 0), axis=-1)
            + rank[:, :TOP_K].reshape(nt, per)) * TOK_ROWS).reshape(nt, 1, per)
    w3 = gw[:, :TOP_K].reshape(nt, 1, per)
    pad = jnp.concatenate([tile_start * FFN_TM + counts, tiles_per * FFN_TM - counts,
                           n_tiles * FFN_TM, t_tiles - n_tiles]).astype(i32)

    xs = _dispatch_call(seg3, lp3, pad, hn, t_tiles * FFN_TM)
    y = _ffn_call(tile_expert, tile_valid, n_tiles.astype(i32), xs,
                  w_mlp1[i], b_mlp1[i][:, None, :], w_mlp2[i], b_mlp2[i][:, None, :])
    out = _combine_call(seg3, lp3, w3, h, y)
    return out.reshape(b, s, d)
```

```python
import functools
import math

import jax
import jax.numpy as jnp
import numpy as np
from jax import lax
from jax.experimental import pallas as pl
from jax.experimental.pallas import tpu as pltpu

D_MODEL = 1024
ROPE_THETA = 10000.0
NORM_EPS = 1e-6
MLA_HEADS = 8
MLA_NOPE = 64
MLA_ROPE = 32
MLA_QK = MLA_NOPE + MLA_ROPE
MLA_V = 64
MLA_Q_LORA = 768
MLA_KV_LORA = 256
DIFF_HEADS = 4
DIFF_D = 64
DIFF_V = 128
N_EXPERTS = 32
TOP_K = 4
D_EXPERT = 1024
SWIGLU_ALPHA = 1.702
SWIGLU_LIMIT = 7.0
LAMBDA_INIT = 0.8 - 0.6 * math.exp(-0.3 * 0)

LANES = 128
V7X_VMEM_LIMIT = 56 * 1024 * 1024

PREP_TM = 256
ATT_TQ = 512
ROUTE_TM = 512
VT_ROWS = 144
TOK_ROWS = 8
TOK_UNROLL = 8
FFN_TM = 512

NEG_BIG = -1e30
LOG2E = math.log2(math.e)

_C_LAT = 0
_C_KPE = 1024
_C_DQ = 1280
_C_DK = 2304
_C_DV = 3328
_C_GATE = 3840
_C_END = 5888


def _f32dot(a, b):
    return jnp.dot(a, b, preferred_element_type=jnp.float32)


def _dot_nt(a, b):
    return lax.dot_general(a, b, (((1,), (1,)), ((), ())), preferred_element_type=jnp.float32)


def _rms_scale(x, width):
    return lax.rsqrt(jnp.sum(x * x, axis=-1, keepdims=True) * (1.0 / width) + NORM_EPS)


def _lane_iota(shape):
    return lax.broadcasted_iota(jnp.int32, shape, len(shape) - 1)


def _prep_kernel(x_ref, gattn_ref, w1_ref, glat_ref, wq_ref, wkv_ref, tabs_ref,
                 qa_ref, ka_ref, va_ref, qd_ref, kd_ref, vd_ref, sg_ref):
    x = x_ref[0]
    xn = x * _rms_scale(x, D_MODEL) * gattn_ref[...]
    proj = _f32dot(xn.astype(jnp.bfloat16), w1_ref[...])

    glat = glat_ref[...]
    q_lat = proj[:, 0:MLA_Q_LORA]
    kv_lat = proj[:, MLA_Q_LORA:MLA_Q_LORA + MLA_KV_LORA]
    qn = q_lat * _rms_scale(q_lat, MLA_Q_LORA) * glat[:, 0:MLA_Q_LORA]
    kvn = kv_lat * _rms_scale(kv_lat, MLA_KV_LORA) * glat[:, MLA_Q_LORA:]
    qq = _f32dot(qn.astype(jnp.bfloat16), wq_ref[...])
    kk = _f32dot(kvn.astype(jnp.bfloat16), wkv_ref[...])

    kpe = proj[:, _C_KPE:_C_KPE + LANES]
    kper = proj[:, _C_KPE + LANES:_C_KPE + 2 * LANES]
    cq, sq = tabs_ref[0], tabs_ref[1]
    ck, sk = tabs_ref[2], tabs_ref[3]
    for h in range(MLA_HEADS):
        lo, hi = h * LANES, (h + 1) * LANES
        qh = qq[:, lo:hi]
        qa_ref[0, h] = (_rms_scale(qh, MLA_QK) * (qh * cq + qq[:, 1024 + lo:1024 + hi] * sq)).astype(jnp.bfloat16)
        kh = kk[:, lo:hi] + kpe
        ka_ref[0, h] = (_rms_scale(kh, MLA_QK) * (kh * ck + kper * sk)).astype(jnp.bfloat16)
    ones_rows = (lax.broadcasted_iota(jnp.int32, (VT_ROWS - LANES, x.shape[0]), 0) == 0).astype(jnp.bfloat16)
    for j in range(MLA_HEADS // 2):
        va_ref[0, j, 0, :LANES, :] = kk[:, 1024 + j * LANES:1024 + (j + 1) * LANES].T.astype(jnp.bfloat16)
        va_ref[0, j, 0, LANES:, :] = ones_rows

    cqd, sqd = tabs_ref[4], tabs_ref[5]
    ckd, skd = tabs_ref[6], tabs_ref[7]
    lane = _lane_iota((x.shape[0], LANES))
    first = lane < DIFF_D

    def half_norm(v):
        sqv = v * v
        s_all = jnp.sum(sqv, axis=-1, keepdims=True)
        s_lo = jnp.sum(jnp.where(first, sqv, 0.0), axis=-1, keepdims=True)
        r_lo = lax.rsqrt(s_lo * (1.0 / DIFF_D) + NORM_EPS)
        r_hi = lax.rsqrt((s_all - s_lo) * (1.0 / DIFF_D) + NORM_EPS)
        return jnp.where(first, r_lo, r_hi)

    for h in range(DIFF_HEADS):
        lo, hi = h * LANES, (h + 1) * LANES
        dq = proj[:, _C_DQ + lo:_C_DQ + hi]
        dqr = proj[:, _C_DQ + 512 + lo:_C_DQ + 512 + hi]
        qv = half_norm(dq) * (dq * cqd + dqr * sqd)
        qd_ref[0, 2 * h] = jnp.where(first, qv, 0.0).astype(jnp.bfloat16)
        qd_ref[0, 2 * h + 1] = jnp.where(first, 0.0, qv).astype(jnp.bfloat16)
        dk = proj[:, _C_DK + lo:_C_DK + hi]
        dkr = proj[:, _C_DK + 512 + lo:_C_DK + 512 + hi]
        kd_ref[0, h] = (half_norm(dk) * (dk * ckd + dkr * skd)).astype(jnp.bfloat16)
        vd_ref[0, h, 0, :LANES, :] = proj[:, _C_DV + lo:_C_DV + hi].T.astype(jnp.bfloat16)
        vd_ref[0, h, 0, LANES:, :] = ones_rows

    sg_ref[0] = jax.nn.sigmoid(proj[:, _C_GATE:_C_END]).astype(jnp.bfloat16)


def _att_tiles(s):
    tq = min(ATT_TQ, s)
    return tq, tq


def _const_spec(shape):
    nd = len(shape)
    return pl.BlockSpec(shape, lambda *_: (0,) * nd)


def _prep_call(x, gattn, w1, glat, wq, wkv, tabs):
    b, s, _ = x.shape
    tm = min(PREP_TM, s)
    grid = (b, s // tm)
    head_out = lambda nh: pl.BlockSpec((1, nh, tm, LANES), lambda bi, ti: (bi, 0, ti, 0))
    _, tk = _att_tiles(s)
    per = tk // tm
    vt_out = lambda nh: pl.BlockSpec((1, nh, 1, VT_ROWS, tm), lambda bi, ti: (bi, 0, ti // per, 0, ti % per))
    bf = jnp.bfloat16
    out_shape = (
        jax.ShapeDtypeStruct((b, MLA_HEADS, s, LANES), bf),
        jax.ShapeDtypeStruct((b, MLA_HEADS, s, LANES), bf),
        jax.ShapeDtypeStruct((b, MLA_HEADS // 2, s // tk, VT_ROWS, tk), bf),
        jax.ShapeDtypeStruct((b, 2 * DIFF_HEADS, s, LANES), bf),
        jax.ShapeDtypeStruct((b, DIFF_HEADS, s, LANES), bf),
        jax.ShapeDtypeStruct((b, DIFF_HEADS, s // tk, VT_ROWS, tk), bf),
        jax.ShapeDtypeStruct((b, s, 2 * D_MODEL), bf),
    )
    return pl.pallas_call(
        _prep_kernel,
        grid=grid,
        in_specs=[
            pl.BlockSpec((1, tm, D_MODEL), lambda bi, ti: (bi, ti, 0)),
            _const_spec(gattn.shape),
            _const_spec(w1.shape),
            _const_spec(glat.shape),
            _const_spec(wq.shape),
            _const_spec(wkv.shape),
            pl.BlockSpec((8, tm, LANES), lambda bi, ti: (0, ti, 0)),
        ],
        out_specs=(
            head_out(MLA_HEADS), head_out(MLA_HEADS), vt_out(MLA_HEADS // 2),
            head_out(2 * DIFF_HEADS), head_out(DIFF_HEADS), vt_out(DIFF_HEADS),
            pl.BlockSpec((1, tm, 2 * D_MODEL), lambda bi, ti: (bi, ti, 0)),
        ),
        out_shape=out_shape,
        compiler_params=pltpu.CompilerParams(
            dimension_semantics=("arbitrary", "arbitrary"), vmem_limit_bytes=V7X_VMEM_LIMIT),
        name="prep",
    )(x, gattn, w1, glat, wq, wkv, tabs)


def _attn_kernel(q_ref, k_ref, vt_ref, lam_ref, subln_ref, o_ref, st_buf, mx_buf, m_scr, acc_scr, *, mode, tq):
    nq = q_ref.shape[2] // tq
    shared_k = k_ref.shape[1] == 1
    DIAG = 2

    def scores(qr, kb, slot, masked):
        q0 = pl.multiple_of(qr * tq, tq)
        k0 = pl.multiple_of(kb * tq, tq)
        for hh in range(2):
            k = k_ref[0, 0 if shared_k else hh, pl.ds(k0, tq), :]
            st = _dot_nt(k, q_ref[0, hh, pl.ds(q0, tq), :])
            if masked:
                key = lax.broadcasted_iota(jnp.int32, st.shape, 0)
                qry = lax.broadcasted_iota(jnp.int32, st.shape, 1)
                st = jnp.where(key <= qry, st, NEG_BIG)
            st_buf[slot, hh] = st
            mx_buf[slot, hh] = jnp.max(st, axis=0, keepdims=True)

    def softmax_pv(kb, slot):
        vt = vt_ref[0, 0, kb]
        for hh in range(2):
            m_prev = m_scr[hh]
            m_new = jnp.maximum(m_prev, mx_buf[slot, hh])
            alpha = jnp.exp2(m_prev - m_new)
            p = jnp.exp2(st_buf[slot, hh] - m_new)
            acc_scr[hh] = alpha * acc_scr[hh] + _f32dot(vt, p.astype(jnp.bfloat16))
            m_scr[hh] = m_new

    def reset():
        m_scr[...] = jnp.full(m_scr.shape, NEG_BIG, jnp.float32)
        acc_scr[...] = jnp.zeros(acc_scr.shape, jnp.float32)

    def finalize(qr):
        oa = acc_scr[0, :LANES] / acc_scr[0, LANES:LANES + 1]
        ob = acc_scr[1, :LANES] / acc_scr[1, LANES:LANES + 1]
        rows = pl.ds(pl.multiple_of(qr * tq, tq), tq)
        if mode == "mla":
            sub = lax.broadcasted_iota(jnp.int32, oa.shape, 0)
            o_ref[0, rows, :] = jnp.where(sub < MLA_V, oa, ob).T.astype(o_ref.dtype)
        else:
            lp = lam_ref[...]
            lam = (jnp.exp(jnp.sum(lp[0:1] * lp[1:2], axis=-1, keepdims=True))
                   - jnp.exp(jnp.sum(lp[2:3] * lp[3:4], axis=-1, keepdims=True)) + LAMBDA_INIT)
            o = oa - lam * ob
            o = o * lax.rsqrt(jnp.sum(o * o, axis=0, keepdims=True) * (1.0 / DIFF_V) + NORM_EPS)
            o_ref[0, rows, :] = (o.T * subln_ref[...] * (1.0 - LAMBDA_INIT)).astype(o_ref.dtype)

    def next_diag(qr):
        nxt = jnp.minimum(qr + 1, nq - 1)
        scores(nxt, nxt, DIAG, True)

    reset()
    scores(0, 0, DIAG, True)
    softmax_pv(0, DIAG)
    finalize(0)
    if nq > 1:
        scores(1, 1, DIAG, True)

    def row(qr, carry):
        reset()
        scores(qr, 0, 0, False)
        softmax_pv(qr, DIAG)
        rest = qr - 1

        def pair(p):
            scores(qr, 2 * p + 1, 1, False)
            softmax_pv(2 * p, 0)
            scores(qr, 2 * p + 2, 0, False)
            softmax_pv(2 * p + 1, 1)

        def quad(j, c):
            pair(2 * j)
            pair(2 * j + 1)
            return c

        lax.fori_loop(0, rest // 4, quad, 0)

        @pl.when(rest % 4 >= 2)
        def _():
            pair(rest // 4 * 2)

        @pl.when(rest % 2 == 0)
        def _():
            softmax_pv(qr - 1, 0)
            next_diag(qr)

        @pl.when(rest % 2 == 1)
        def _():
            scores(qr, qr - 1, 1, False)
            softmax_pv(qr - 2, 0)
            softmax_pv(qr - 1, 1)
            next_diag(qr)

        finalize(qr)
        return carry

    lax.fori_loop(1, nq, row, 0)


def _attn_call(q, k, vt, lam, subln, *, mode):
    b, _, s, _ = q.shape
    n_groups = vt.shape[1]
    kh = k.shape[1] // n_groups
    tq, tk = _att_tiles(s)
    assert tk == tq
    kernel = functools.partial(_attn_kernel, mode=mode, tq=tq)
    return pl.pallas_call(
        kernel,
        grid=(b, n_groups),
        in_specs=[
            pl.BlockSpec((1, 2, s, LANES), lambda bi, g: (bi, g, 0, 0)),
            pl.BlockSpec((1, kh, s, LANES), lambda bi, g: (bi, g, 0, 0)),
            pl.BlockSpec((1, 1, s // tk, VT_ROWS, tk), lambda bi, g: (bi, g, 0, 0, 0)),
            _const_spec(lam.shape),
            _const_spec(subln.shape),
        ],
        out_specs=pl.BlockSpec((1, s, LANES), lambda bi, g: (bi, 0, g)),
        out_shape=jax.ShapeDtypeStruct((b, s, n_groups * LANES), jnp.bfloat16),
        scratch_shapes=[
            pltpu.VMEM((3, 2, tq, tq), jnp.float32),
            pltpu.VMEM((3, 2, 1, tq), jnp.float32),
            pltpu.VMEM((2, 1, tq), jnp.float32),
            pltpu.VMEM((2, VT_ROWS, tq), jnp.float32),
        ],
        compiler_params=pltpu.CompilerParams(
            dimension_semantics=("arbitrary", "arbitrary"), vmem_limit_bytes=V7X_VMEM_LIMIT),
        name="attn_" + mode,
    )(q, k, vt, lam, subln)


def _to_token_tiles(ref, mat):
    n = mat.shape[0]
    for j in range(TOK_ROWS):
        ref[pl.ds(j, n, stride=TOK_ROWS), :] = mat[:, j * LANES:(j + 1) * LANES]


def _from_token_tiles(ref, n):
    return jnp.concatenate([ref[pl.ds(j, n, stride=TOK_ROWS), :] for j in range(TOK_ROWS)], axis=1)


def _tok_at(ref, row):
    return ref.at[pl.ds(pl.multiple_of(row, TOK_ROWS), TOK_ROWS)]


def _tok(ref, t):
    return _tok_at(ref, t * TOK_ROWS)


def _copy_token_run(src, dst, src_tok, dst_tok, count, sem, top_bit):
    bit = top_bit
    while bit >= 1:
        done = count & ~(2 * bit - 1)

        @pl.when((count & bit) != 0)
        def _(bit=bit, done=done):
            s = pl.multiple_of((src_tok + done) * TOK_ROWS, TOK_ROWS)
            d = pl.multiple_of((dst_tok + done) * TOK_ROWS, TOK_ROWS)
            pltpu.make_async_copy(src.at[pl.ds(s, bit * TOK_ROWS)], dst.at[pl.ds(d, bit * TOK_ROWS)], sem).start()

        bit //= 2


def _route_kernel(x_ref, oa_ref, ob_ref, sg_ref, wbm_ref, wbd_ref, wout_ref, gffn_ref, wrh_ref, wrl_ref, br_ref,
                  h_ref, hn_ref, idx_ref, w_ref, rank_ref, tcnt_ref, tbase_ref, cnt_ref, carry_scr):
    @pl.when(pl.program_id(0) == 0)
    def _():
        carry_scr[...] = jnp.zeros(carry_scr.shape, jnp.float32)

    sg = sg_ref[...]
    ma = _f32dot(oa_ref[...], wbm_ref[...])
    mb = _f32dot(ob_ref[...], wbd_ref[...])
    merged = sg[:, :D_MODEL] * ma + sg[:, D_MODEL:] * mb
    h = x_ref[...] + _f32dot(merged.astype(jnp.bfloat16), wout_ref[...])
    h_ref[...] = h

    hn = h * _rms_scale(h, D_MODEL) * gffn_ref[...]
    _to_token_tiles(hn_ref, hn)
    hn_hi = hn.astype(jnp.bfloat16)
    hn_lo = (hn - hn_hi.astype(jnp.float32)).astype(jnp.bfloat16)
    wrh = wrh_ref[...]
    logits = _f32dot(hn_hi, wrh) + _f32dot(hn_lo, wrh) + _f32dot(hn_hi, wrl_ref[...]) + br_ref[...]

    tm = logits.shape[0]
    lane = _lane_iota(logits.shape)
    work = logits
    vals, picks, onehots = [], [], []
    for _k in range(TOP_K):
        mx = jnp.max(work, axis=-1, keepdims=True)
        sel = jnp.min(jnp.where(work == mx, lane, LANES), axis=-1, keepdims=True)
        oh = lane == sel
        vals.append(mx)
        picks.append(sel)
        onehots.append(oh)
        work = jnp.where(oh, NEG_BIG * 2, work)
    exps = [jnp.exp(vv - vals[0]) for vv in vals]
    denom = exps[0] + exps[1] + exps[2] + exps[3]

    chosen = jnp.zeros(logits.shape, jnp.float32)
    for oh in onehots:
        chosen = chosen + oh.astype(jnp.float32)
    r_i = lax.broadcasted_iota(jnp.int32, (tm, tm), 0)
    c_i = lax.broadcasted_iota(jnp.int32, (tm, tm), 1)
    ltri = (c_i < r_i).astype(jnp.bfloat16)
    before = _f32dot(ltri, chosen.astype(jnp.bfloat16))
    tile_counts = jnp.sum(chosen, axis=0, keepdims=True)
    tbase_ref[0] = carry_scr[...]
    tcnt_ref[0] = tile_counts
    carry_scr[...] = carry_scr[...] + tile_counts
    cnt_ref[...] = carry_scr[...]

    idx_out = jnp.zeros(logits.shape, jnp.int32)
    w_out = jnp.zeros(logits.shape, jnp.float32)
    rank_out = jnp.zeros(logits.shape, jnp.int32)
    for kk in range(TOP_K):
        oh = onehots[kk]
        r_k = jnp.sum(jnp.where(oh, before, 0.0), axis=-1, keepdims=True).astype(jnp.int32)
        idx_out = jnp.where(lane == kk, picks[kk], idx_out)
        rank_out = jnp.where(lane == kk, r_k, rank_out)
        w_out = jnp.where(lane == kk, exps[kk] / denom, w_out)
    idx_ref[...] = idx_out
    w_ref[...] = w_out
    rank_ref[...] = rank_out


def _route_call(x2, oa, ob, sg, wbm, wbd, wout, gffn, wrh, wrl, br):
    n = x2.shape[0]
    tm = min(ROUTE_TM, n)
    nt = n // tm
    row = lambda w: pl.BlockSpec((tm, w), lambda i: (i, 0))
    per_tile = pl.BlockSpec((1, 1, LANES), lambda i: (i, 0, 0))
    out_shape = (
        jax.ShapeDtypeStruct((n, D_MODEL), jnp.float32),
        jax.ShapeDtypeStruct((n * TOK_ROWS, LANES), jnp.float32),
        jax.ShapeDtypeStruct((n, LANES), jnp.int32),
        jax.ShapeDtypeStruct((n, LANES), jnp.float32),
        jax.ShapeDtypeStruct((n, LANES), jnp.int32),
        jax.ShapeDtypeStruct((nt, 1, LANES), jnp.float32),
        jax.ShapeDtypeStruct((nt, 1, LANES), jnp.float32),
        jax.ShapeDtypeStruct((1, LANES), jnp.float32),
    )
    return pl.pallas_call(
        _route_kernel,
        grid=(nt,),
        in_specs=[row(D_MODEL), row(512), row(512), row(2 * D_MODEL),
                  _const_spec(wbm.shape), _const_spec(wbd.shape), _const_spec(wout.shape), _const_spec(gffn.shape),
                  _const_spec(wrh.shape), _const_spec(wrl.shape), _const_spec(br.shape)],
        out_specs=(row(D_MODEL), pl.BlockSpec((tm * TOK_ROWS, LANES), lambda i: (i, 0)), row(LANES), row(LANES),
                   row(LANES), per_tile, per_tile, _const_spec((1, LANES))),
        out_shape=out_shape,
        scratch_shapes=[pltpu.VMEM((1, LANES), jnp.float32)],
        compiler_params=pltpu.CompilerParams(dimension_semantics=("arbitrary",), vmem_limit_bytes=V7X_VMEM_LIMIT),
        name="route",
    )(x2, oa, ob, sg, wbm, wbd, wout, gffn, wrh, wrl, br)


def _dispatch_kernel(seg_ref, lp_ref, pad_ref, hn_ref, xs_ref, scr, zscr, sems, zsem):
    step = pl.program_id(0)
    last = pl.num_programs(0) - 1
    tm = hn_ref.shape[0] // TOK_ROWS
    par = step % 2
    buf = scr.at[par]

    def place(t, carry):
        v = _tok(hn_ref, t)[...]
        for kk in range(TOP_K):
            _tok_at(buf, lp_ref[0, 0, t * TOP_K + kk])[...] = v
        return carry

    lax.fori_loop(0, tm, place, 0, unroll=TOK_UNROLL)

    def run(e, carry):
        _copy_token_run(buf, xs_ref, seg_ref[0, 0, N_EXPERTS + e], seg_ref[0, 0, 2 * N_EXPERTS + e],
                        seg_ref[0, 0, e], sems.at[par], tm)
        return carry

    lax.fori_loop(0, N_EXPERTS, run, 0)

    def drain(p):
        pltpu.make_async_copy(scr.at[p], xs_ref.at[pl.ds(0, scr.shape[1])], sems.at[p]).wait()

    @pl.when(step == 0)
    def _():
        zscr[...] = jnp.zeros(zscr.shape, zscr.dtype)

        def pad_run(e, carry):
            _copy_token_run(zscr, xs_ref, 0, pad_ref[e], pad_ref[N_EXPERTS + e], zsem, FFN_TM // 2)
            return carry

        lax.fori_loop(0, N_EXPERTS, pad_run, 0)

        def pad_tile(j, carry):
            d = pl.multiple_of((pad_ref[2 * N_EXPERTS] + j * FFN_TM) * TOK_ROWS, TOK_ROWS)
            pltpu.make_async_copy(zscr, xs_ref.at[pl.ds(d, FFN_TM * TOK_ROWS)], zsem).start()
            return carry

        lax.fori_loop(0, pad_ref[2 * N_EXPERTS + 1], pad_tile, 0)

    @pl.when(step > 0)
    def _():
        drain(1 - par)

    @pl.when(step == last)
    def _():
        drain(par)
        for _e in range(N_EXPERTS):
            pltpu.make_async_copy(zscr, xs_ref.at[pl.ds(0, FFN_TM * TOK_ROWS)], zsem).wait()


def _dispatch_call(seg3, lp3, pad, hn, n_rows):
    nt = seg3.shape[0]
    tm = hn.shape[0] // TOK_ROWS // nt
    smem = lambda a: pl.BlockSpec((1, 1, a.shape[2]), lambda i: (i, 0, 0), memory_space=pltpu.SMEM)
    return pl.pallas_call(
        _dispatch_kernel,
        grid=(nt,),
        in_specs=[
            smem(seg3), smem(lp3),
            pl.BlockSpec(memory_space=pltpu.SMEM),
            pl.BlockSpec((tm * TOK_ROWS, LANES), lambda i: (i, 0)),
        ],
        out_specs=pl.BlockSpec(memory_space=pl.ANY),
        out_shape=jax.ShapeDtypeStruct((n_rows * TOK_ROWS, LANES), jnp.float32),
        scratch_shapes=[pltpu.VMEM((2, TOP_K * tm * TOK_ROWS, LANES), jnp.float32),
                        pltpu.VMEM((FFN_TM * TOK_ROWS, LANES), jnp.float32),
                        pltpu.SemaphoreType.DMA((2,)), pltpu.SemaphoreType.DMA(())],
        compiler_params=pltpu.CompilerParams(dimension_semantics=("arbitrary",), vmem_limit_bytes=V7X_VMEM_LIMIT),
        name="dispatch",
    )(seg3, lp3, pad, hn)


def _ffn_kernel(te_ref, tv_ref, nt_ref, xs_ref, w1_ref, b1_ref, w2_ref, b2_ref, y_ref, w1b, w2b):
    i = pl.program_id(0)
    tm = xs_ref.shape[0] // TOK_ROWS

    @pl.when((i == 0) | (te_ref[i] != te_ref[jnp.maximum(i - 1, 0)]))
    def _():
        w1b[...] = w1_ref[0].astype(jnp.bfloat16)
        w2b[...] = w2_ref[0].astype(jnp.bfloat16)

    @pl.when(i < nt_ref[0])
    def _():
        rows = lax.broadcasted_iota(jnp.int32, (tm, 1), 0)
        xt = jnp.where(rows < tv_ref[i], _from_token_tiles(xs_ref, tm), 0.0).astype(jnp.bfloat16)
        u = _f32dot(xt, w1b[...]) + b1_ref[0]
        glu = jnp.minimum(u[:, :D_EXPERT], SWIGLU_LIMIT)
        lin = jnp.clip(u[:, D_EXPERT:], -SWIGLU_LIMIT, SWIGLU_LIMIT)
        a = glu * jax.nn.sigmoid(SWIGLU_ALPHA * glu) * (lin + 1.0)
        y = _f32dot(a.astype(jnp.bfloat16), w2b[...]) + b2_ref[0]
        _to_token_tiles(y_ref, y)

    @pl.when(i >= nt_ref[0])
    def _():
        y_ref[...] = jnp.zeros(y_ref.shape, y_ref.dtype)


def _ffn_call(tile_expert, tile_valid, n_tiles, xs, w1, b1, w2, b2):
    n_rows = xs.shape[0] // TOK_ROWS
    tm = FFN_TM
    t_tiles = n_rows // tm

    def tile_map(i, te, tv, nt):
        return (jnp.minimum(i, nt[0] - 1), 0)

    def out_map(i, te, tv, nt):
        return (i, 0)

    def w_map(i, te, tv, nt):
        return (te[i], 0, 0)

    grid_spec = pltpu.PrefetchScalarGridSpec(
        num_scalar_prefetch=3,
        grid=(t_tiles,),
        in_specs=[
            pl.BlockSpec((tm * TOK_ROWS, LANES), tile_map),
            pl.BlockSpec((1, D_MODEL, 2 * D_EXPERT), w_map),
            pl.BlockSpec((1, 1, 2 * D_EXPERT), w_map),
            pl.BlockSpec((1, D_EXPERT, D_MODEL), w_map),
            pl.BlockSpec((1, 1, D_MODEL), w_map),
        ],
        out_specs=pl.BlockSpec((tm * TOK_ROWS, LANES), out_map),
        scratch_shapes=[pltpu.VMEM((D_MODEL, 2 * D_EXPERT), jnp.bfloat16), pltpu.VMEM((D_EXPERT, D_MODEL), jnp.bfloat16)],
    )
    return pl.pallas_call(
        _ffn_kernel,
        grid_spec=grid_spec,
        out_shape=jax.ShapeDtypeStruct(xs.shape, jnp.float32),
        compiler_params=pltpu.CompilerParams(dimension_semantics=("arbitrary",), vmem_limit_bytes=V7X_VMEM_LIMIT),
        name="ffn",
    )(tile_expert, tile_valid, n_tiles, xs, w1, b1, w2, b2)


def _combine_kernel(seg_ref, snext_ref, lp_ref, w_ref, h_ref, y_ref, o_ref, ybuf, tok_scr, sems):
    step = pl.program_id(0)
    last = pl.num_programs(0) - 1
    tm = h_ref.shape[0]
    par = step % 2

    def fetch(s_ref, slot):
        def run(e, carry):
            _copy_token_run(y_ref, ybuf.at[slot], s_ref[0, 0, 2 * N_EXPERTS + e], s_ref[0, 0, N_EXPERTS + e],
                            s_ref[0, 0, e], sems.at[slot], tm)
            return carry

        lax.fori_loop(0, N_EXPERTS, run, 0)

    @pl.when(step == 0)
    def _():
        fetch(seg_ref, 0)

    @pl.when(step < last)
    def _():
        fetch(snext_ref, 1 - par)

    pltpu.make_async_copy(y_ref.at[pl.ds(0, ybuf.shape[1])], ybuf.at[par], sems.at[par]).wait()
    buf = ybuf.at[par]

    def token(t, carry):
        acc = None
        for kk in range(TOP_K):
            term = w_ref[0, 0, t * TOP_K + kk] * _tok_at(buf, lp_ref[0, 0, t * TOP_K + kk])[...]
            acc = term if acc is None else acc + term
        _tok(tok_scr, t)[...] = acc
        return carry

    lax.fori_loop(0, tm, token, 0, unroll=TOK_UNROLL)
    o_ref[...] = h_ref[...] + _from_token_tiles(tok_scr, tm)


def _combine_call(seg3, lp3, w3, h, y):
    n = h.shape[0]
    nt = seg3.shape[0]
    tm = n // nt
    smem = lambda a, f: pl.BlockSpec((1, 1, a.shape[2]), f, memory_space=pltpu.SMEM)
    here = lambda i: (i, 0, 0)
    return pl.pallas_call(
        _combine_kernel,
        grid=(nt,),
        in_specs=[
            smem(seg3, here), smem(seg3, lambda i: (jnp.minimum(i + 1, nt - 1), 0, 0)), smem(lp3, here), smem(w3, here),
            pl.BlockSpec((tm, D_MODEL), lambda i: (i, 0)),
            pl.BlockSpec(memory_space=pl.ANY),
        ],
        out_specs=pl.BlockSpec((tm, D_MODEL), lambda i: (i, 0)),
        out_shape=jax.ShapeDtypeStruct((n, D_MODEL), jnp.float32),
        scratch_shapes=[pltpu.VMEM((2, TOP_K * tm * TOK_ROWS, LANES), jnp.float32),
                        pltpu.VMEM((tm * TOK_ROWS, LANES), jnp.float32), pltpu.SemaphoreType.DMA((2,))],
        compiler_params=pltpu.CompilerParams(dimension_semantics=("arbitrary",), vmem_limit_bytes=V7X_VMEM_LIMIT),
        name="combine",
    )(seg3, seg3, lp3, w3, h, y)


def _rot_cols(w, group):
    lead = w.shape[:-1]
    g = w.reshape(lead + (-1, group))
    half = group // 2
    return jnp.concatenate([-g[..., half:], g[..., :half]], axis=-1).reshape(w.shape)


def _rope_tables(s, dim):
    inv_freq = 1.0 / (ROPE_THETA ** (jnp.arange(0, dim, 2, dtype=jnp.float32) / dim))
    ang = jnp.arange(s, dtype=jnp.float32)[:, None] * inv_freq[None, :]
    return jnp.cos(ang), jnp.sin(ang)


def _mla_tables(s, gain, scale):
    cos, sin = _rope_tables(s, MLA_ROPE)
    cos2 = jnp.concatenate([cos, cos], axis=-1)
    sin2 = jnp.concatenate([sin, sin], axis=-1)
    g_nope, g_rope = gain[:MLA_NOPE], gain[MLA_NOPE:]
    g_perm = jnp.concatenate([g_rope[MLA_ROPE // 2:], g_rope[:MLA_ROPE // 2]])
    pad = jnp.zeros((s, LANES - MLA_QK), jnp.float32)
    c = jnp.concatenate([jnp.broadcast_to(g_nope, (s, MLA_NOPE)), cos2 * g_rope, pad], axis=-1) * scale
    sn = jnp.concatenate([jnp.zeros((s, MLA_NOPE), jnp.float32), sin2 * g_perm, pad], axis=-1) * scale
    return c, sn


def _diff_tables(s, gain, scale):
    cos, sin = _rope_tables(s, DIFF_D)
    cos2 = jnp.concatenate([cos, cos], axis=-1)
    sin2 = jnp.concatenate([sin, sin], axis=-1)
    g_perm = jnp.concatenate([gain[DIFF_D // 2:], gain[:DIFF_D // 2]])
    c = cos2 * gain * scale
    sn = sin2 * g_perm * scale
    return jnp.concatenate([c, c], axis=-1), jnp.concatenate([sn, sn], axis=-1)


def _pad_heads(w, width):
    kdim = w.shape[0]
    g = w.reshape(kdim, -1, width)
    g = jnp.pad(g, ((0, 0), (0, 0), (0, LANES - width)))
    return g.reshape(kdim, -1)


def kernel(x, attn_norm, w_in, mla_q_lat_norm, mla_kv_lat_norm, mla_w_uq, mla_w_ukv, mla_q_norm, mla_k_norm,
           diff_q_norm, diff_k_norm, diff_lambda_q1, diff_lambda_k1, diff_lambda_q2, diff_lambda_k2, diff_subln,
           w_branch_mla, w_branch_diff, w_out, ffn_norm, w_router, b_router, w_mlp1, b_mlp1, w_mlp2, b_mlp2):
    b, s, d = x.shape
    n = b * s
    bf = jnp.bfloat16
    f32 = jnp.float32
    i = 0

    wi = w_in[i]
    c0 = MLA_Q_LORA + MLA_KV_LORA
    w_kpe = wi[:, c0:c0 + MLA_ROPE]
    c1 = c0 + MLA_ROPE
    w_dq, w_dk, w_dv = wi[:, c1:c1 + 512], wi[:, c1 + 512:c1 + 1024], wi[:, c1 + 1024:c1 + 1536]
    w_gate = wi[:, c1 + 1536:]
    place = lambda w: jnp.pad(w, ((0, 0), (MLA_NOPE, LANES - MLA_QK)))
    w1 = jnp.concatenate([
        wi[:, :c0], place(w_kpe), place(_rot_cols(w_kpe, MLA_ROPE)),
        w_dq, _rot_cols(w_dq, DIFF_D), w_dk, _rot_cols(w_dk, DIFF_D), w_dv, w_gate], axis=1).astype(bf)

    wuq = mla_w_uq[i].reshape(MLA_Q_LORA, MLA_HEADS, MLA_QK)
    wuq_rot = jnp.concatenate(
        [jnp.zeros((MLA_Q_LORA, MLA_HEADS, MLA_NOPE), f32), _rot_cols(wuq[..., MLA_NOPE:], MLA_ROPE)], axis=-1)
    wq = jnp.concatenate([_pad_heads(wuq.reshape(MLA_Q_LORA, -1), MLA_QK),
                          _pad_heads(wuq_rot.reshape(MLA_Q_LORA, -1), MLA_QK)], axis=1).astype(bf)
    wukv = mla_w_ukv[i].reshape(MLA_KV_LORA, MLA_HEADS, MLA_NOPE + MLA_V)
    wkv = jnp.concatenate([_pad_heads(wukv[..., :MLA_NOPE].reshape(MLA_KV_LORA, -1), MLA_NOPE),
                           wukv[..., MLA_NOPE:].reshape(MLA_KV_LORA, -1)], axis=1).astype(bf)

    cq, sq = _mla_tables(s, mla_q_norm[i], LOG2E / math.sqrt(MLA_QK))
    ck, sk = _mla_tables(s, mla_k_norm[i], 1.0)
    cqd, sqd = _diff_tables(s, diff_q_norm[i], LOG2E / math.sqrt(DIFF_D))
    ckd, skd = _diff_tables(s, diff_k_norm[i], 1.0)
    tabs = jnp.stack([cq, sq, ck, sk, cqd, sqd, ckd, skd])

    glat = jnp.concatenate([mla_q_lat_norm[i], mla_kv_lat_norm[i]])[None, :]
    qa, ka, va, qd, kd, vd, sg = _prep_call(x, attn_norm[i][None, :], w1, glat, wq, wkv, tabs)

    lam = jnp.pad(jnp.stack([diff_lambda_q1[i], diff_lambda_k1[i], diff_lambda_q2[i], diff_lambda_k2[i]]),
                  ((0, 0), (0, LANES - DIFF_D)))
    subln = diff_subln[i][None, :]
    o_a = _attn_call(qa, ka, va, lam, subln, mode="mla")
    o_b = _attn_call(qd, kd, vd, lam, subln, mode="diff")

    wr = jnp.pad(w_router[i], ((0, 0), (0, LANES - N_EXPERTS)))
    wr_hi = wr.astype(bf)
    wr_lo = (wr - wr_hi.astype(f32)).astype(bf)
    br = jnp.concatenate([b_router[i], jnp.full((LANES - N_EXPERTS,), NEG_BIG, f32)])[None, :]
    h, hn, idx, gw, rank, tcnt, tbase, cnt = _route_call(
        x.reshape(n, d), o_a.reshape(n, 512), o_b.reshape(n, 512), sg.reshape(n, 2 * d),
        w_branch_mla[i].astype(bf), w_branch_diff[i].astype(bf), w_out[i].astype(bf), ffn_norm[i][None, :],
        wr_hi, wr_lo, br)

    i32 = jnp.int32
    counts = cnt[0, :N_EXPERTS].astype(i32)
    tiles_per = (counts + FFN_TM - 1) // FFN_TM
    tile_end = jnp.cumsum(tiles_per)
    tile_start = tile_end - tiles_per
    n_tiles = tile_end[-1:]
    t_tiles = n * TOP_K // FFN_TM + N_EXPERTS
    tids = jnp.arange(t_tiles, dtype=i32)
    tile_expert = jnp.minimum(jnp.sum(tids[:, None] >= tile_end[None, :], axis=1), N_EXPERTS - 1).astype(i32)
    tile_valid = jnp.clip(counts[tile_expert] - (tids - tile_start[tile_expert]) * FFN_TM, 0, FFN_TM).astype(i32)

    nt = tcnt.shape[0]
    tc = tcnt[:, 0, :N_EXPERTS].astype(i32)
    tb = tbase[:, 0, :N_EXPERTS].astype(i32)
    off = jnp.cumsum(tc, axis=1) - tc
    first = (tile_start * FFN_TM)[None, :] + tb
    seg3 = jnp.concatenate([tc, off, first], axis=1).reshape(nt, 1, 3 * N_EXPERTS)
    per = n // nt * TOP_K
    picked = idx[:, :TOP_K].reshape(nt, per, 1) == jnp.arange(N_EXPERTS, dtype=i32)
    lp3 = ((jnp.sum(jnp.where(picked, off[:, None, :], 0), axis=-1)
            + rank[:, :TOP_K].reshape(nt, per)) * TOK_ROWS).reshape(nt, 1, per)
    w3 = gw[:, :TOP_K].reshape(nt, 1, per)
    pad = jnp.concatenate([tile_start * FFN_TM + counts, tiles_per * FFN_TM - counts,
                           n_tiles * FFN_TM, t_tiles - n_tiles]).astype(i32)

    xs = _dispatch_call(seg3, lp3, pad, hn, t_tiles * FFN_TM)
    y = _ffn_call(tile_expert, tile_valid, n_tiles.astype(i32), xs,
                  w_mlp1[i], b_mlp1[i][:, None, :], w_mlp2[i], b_mlp2[i][:, None, :])
    out = _combine_call(seg3, lp3, w3, h, y)
    return out.reshape(b, s, d)
```

```python
import functools
import math

import jax
import jax.numpy as jnp
import numpy as np
from jax import lax
from jax.experimental import pallas as pl
from jax.experimental.pallas import tpu as pltpu

D_MODEL = 1024
ROPE_THETA = 10000.0
NORM_EPS = 1e-6
MLA_HEADS = 8
MLA_NOPE = 64
MLA_ROPE = 32
MLA_QK = MLA_NOPE + MLA_ROPE
MLA_V = 64
MLA_Q_LORA = 768
MLA_KV_LORA = 256
DIFF_HEADS = 4
DIFF_D = 64
DIFF_V = 128
N_EXPERTS = 32
TOP_K = 4
D_EXPERT = 1024
SWIGLU_ALPHA = 1.702
SWIGLU_LIMIT = 7.0
LAMBDA_INIT = 0.8 - 0.6 * math.exp(-0.3 * 0)

LANES = 128
V7X_VMEM_LIMIT = 56 * 1024 * 1024

PREP_TM = 256
ATT_TQ = 512
ROUTE_TM = 512
VT_ROWS = 144
TOK_ROWS = 8
RUN_COMMON = 128
TOK_UNROLL = 8
FFN_TM = 512

NEG_BIG = -1e30
LOG2E = math.log2(math.e)

_C_LAT = 0
_C_KPE = 1024
_C_DQ = 1280
_C_DK = 2304
_C_DV = 3328
_C_GATE = 3840
_C_END = 5888


def _f32dot(a, b):
    return jnp.dot(a, b, preferred_element_type=jnp.float32)


def _dot_nt(a, b):
    return lax.dot_general(a, b, (((1,), (1,)), ((), ())), preferred_element_type=jnp.float32)


def _rms_scale(x, width):
    return lax.rsqrt(jnp.sum(x * x, axis=-1, keepdims=True) * (1.0 / width) + NORM_EPS)


def _lane_iota(shape):
    return lax.broadcasted_iota(jnp.int32, shape, len(shape) - 1)


def _prep_kernel(x_ref, gattn_ref, w1_ref, glat_ref, wq_ref, wkv_ref, tabs_ref,
                 qa_ref, ka_ref, va_ref, qd_ref, kd_ref, vd_ref, sg_ref):
    x = x_ref[0]
    xn = x * _rms_scale(x, D_MODEL) * gattn_ref[...]
    proj = _f32dot(xn.astype(jnp.bfloat16), w1_ref[...])

    glat = glat_ref[...]
    q_lat = proj[:, 0:MLA_Q_LORA]
    kv_lat = proj[:, MLA_Q_LORA:MLA_Q_LORA + MLA_KV_LORA]
    qn = q_lat * _rms_scale(q_lat, MLA_Q_LORA) * glat[:, 0:MLA_Q_LORA]
    kvn = kv_lat * _rms_scale(kv_lat, MLA_KV_LORA) * glat[:, MLA_Q_LORA:]
    qq = _f32dot(qn.astype(jnp.bfloat16), wq_ref[...])
    kk = _f32dot(kvn.astype(jnp.bfloat16), wkv_ref[...])

    kpe = proj[:, _C_KPE:_C_KPE + LANES]
    kper = proj[:, _C_KPE + LANES:_C_KPE + 2 * LANES]
    cq, sq = tabs_ref[0], tabs_ref[1]
    ck, sk = tabs_ref[2], tabs_ref[3]
    for h in range(MLA_HEADS):
        lo, hi = h * LANES, (h + 1) * LANES
        qh = qq[:, lo:hi]
        qa_ref[0, h] = (_rms_scale(qh, MLA_QK) * (qh * cq + qq[:, 1024 + lo:1024 + hi] * sq)).astype(jnp.bfloat16)
        kh = kk[:, lo:hi] + kpe
        ka_ref[0, h] = (_rms_scale(kh, MLA_QK) * (kh * ck + kper * sk)).astype(jnp.bfloat16)
    ones_rows = (lax.broadcasted_iota(jnp.int32, (VT_ROWS - LANES, x.shape[0]), 0) == 0).astype(jnp.bfloat16)
    for j in range(MLA_HEADS // 2):
        va_ref[0, j, 0, :LANES, :] = kk[:, 1024 + j * LANES:1024 + (j + 1) * LANES].T.astype(jnp.bfloat16)
        va_ref[0, j, 0, LANES:, :] = ones_rows

    cqd, sqd = tabs_ref[4], tabs_ref[5]
    ckd, skd = tabs_ref[6], tabs_ref[7]
    lane = _lane_iota((x.shape[0], LANES))
    first = lane < DIFF_D

    def half_norm(v):
        sqv = v * v
        s_all = jnp.sum(sqv, axis=-1, keepdims=True)
        s_lo = jnp.sum(jnp.where(first, sqv, 0.0), axis=-1, keepdims=True)
        r_lo = lax.rsqrt(s_lo * (1.0 / DIFF_D) + NORM_EPS)
        r_hi = lax.rsqrt((s_all - s_lo) * (1.0 / DIFF_D) + NORM_EPS)
        return jnp.where(first, r_lo, r_hi)

    for h in range(DIFF_HEADS):
        lo, hi = h * LANES, (h + 1) * LANES
        dq = proj[:, _C_DQ + lo:_C_DQ + hi]
        dqr = proj[:, _C_DQ + 512 + lo:_C_DQ + 512 + hi]
        qv = half_norm(dq) * (dq * cqd + dqr * sqd)
        qd_ref[0, 2 * h] = jnp.where(first, qv, 0.0).astype(jnp.bfloat16)
        qd_ref[0, 2 * h + 1] = jnp.where(first, 0.0, qv).astype(jnp.bfloat16)
        dk = proj[:, _C_DK + lo:_C_DK + hi]
        dkr = proj[:, _C_DK + 512 + lo:_C_DK + 512 + hi]
        kd_ref[0, h] = (half_norm(dk) * (dk * ckd + dkr * skd)).astype(jnp.bfloat16)
        vd_ref[0, h, 0, :LANES, :] = proj[:, _C_DV + lo:_C_DV + hi].T.astype(jnp.bfloat16)
        vd_ref[0, h, 0, LANES:, :] = ones_rows

    sg_ref[0] = jax.nn.sigmoid(proj[:, _C_GATE:_C_END]).astype(jnp.bfloat16)


def _att_tiles(s):
    tq = min(ATT_TQ, s)
    return tq, tq


def _const_spec(shape):
    nd = len(shape)
    return pl.BlockSpec(shape, lambda *_: (0,) * nd)


def _prep_call(x, gattn, w1, glat, wq, wkv, tabs):
    b, s, _ = x.shape
    tm = min(PREP_TM, s)
    grid = (b, s // tm)
    head_out = lambda nh: pl.BlockSpec((1, nh, tm, LANES), lambda bi, ti: (bi, 0, ti, 0))
    _, tk = _att_tiles(s)
    per = tk // tm
    vt_out = lambda nh: pl.BlockSpec((1, nh, 1, VT_ROWS, tm), lambda bi, ti: (bi, 0, ti // per, 0, ti % per))
    bf = jnp.bfloat16
    out_shape = (
        jax.ShapeDtypeStruct((b, MLA_HEADS, s, LANES), bf),
        jax.ShapeDtypeStruct((b, MLA_HEADS, s, LANES), bf),
        jax.ShapeDtypeStruct((b, MLA_HEADS // 2, s // tk, VT_ROWS, tk), bf),
        jax.ShapeDtypeStruct((b, 2 * DIFF_HEADS, s, LANES), bf),
        jax.ShapeDtypeStruct((b, DIFF_HEADS, s, LANES), bf),
        jax.ShapeDtypeStruct((b, DIFF_HEADS, s // tk, VT_ROWS, tk), bf),
        jax.ShapeDtypeStruct((b, s, 2 * D_MODEL), bf),
    )
    return pl.pallas_call(
        _prep_kernel,
        grid=grid,
        in_specs=[
            pl.BlockSpec((1, tm, D_MODEL), lambda bi, ti: (bi, ti, 0)),
            _const_spec(gattn.shape),
            _const_spec(w1.shape),
            _const_spec(glat.shape),
            _const_spec(wq.shape),
            _const_spec(wkv.shape),
            pl.BlockSpec((8, tm, LANES), lambda bi, ti: (0, ti, 0)),
        ],
        out_specs=(
            head_out(MLA_HEADS), head_out(MLA_HEADS), vt_out(MLA_HEADS // 2),
            head_out(2 * DIFF_HEADS), head_out(DIFF_HEADS), vt_out(DIFF_HEADS),
            pl.BlockSpec((1, tm, 2 * D_MODEL), lambda bi, ti: (bi, ti, 0)),
        ),
        out_shape=out_shape,
        compiler_params=pltpu.CompilerParams(
            dimension_semantics=("arbitrary", "arbitrary"), vmem_limit_bytes=V7X_VMEM_LIMIT),
        name="prep",
    )(x, gattn, w1, glat, wq, wkv, tabs)


def _attn_kernel(q_ref, k_ref, vt_ref, lam_ref, subln_ref, o_ref, st_buf, mx_buf, m_scr, acc_scr, *, mode, tq):
    nq = q_ref.shape[2] // tq
    shared_k = k_ref.shape[1] == 1
    DIAG = 2

    def scores(qr, kb, slot, masked):
        q0 = pl.multiple_of(qr * tq, tq)
        k0 = pl.multiple_of(kb * tq, tq)
        for hh in range(2):
            k = k_ref[0, 0 if shared_k else hh, pl.ds(k0, tq), :]
            st = _dot_nt(k, q_ref[0, hh, pl.ds(q0, tq), :])
            if masked:
                key = lax.broadcasted_iota(jnp.int32, st.shape, 0)
                qry = lax.broadcasted_iota(jnp.int32, st.shape, 1)
                st = jnp.where(key <= qry, st, NEG_BIG)
            st_buf[slot, hh] = st
            mx_buf[slot, hh] = jnp.max(st, axis=0, keepdims=True)

    def softmax_pv(kb, slot):
        vt = vt_ref[0, 0, kb]
        for hh in range(2):
            m_prev = m_scr[hh]
            m_new = jnp.maximum(m_prev, mx_buf[slot, hh])
            alpha = jnp.exp2(m_prev - m_new)
            p = jnp.exp2(st_buf[slot, hh] - m_new)
            acc_scr[hh] = alpha * acc_scr[hh] + _f32dot(vt, p.astype(jnp.bfloat16))
            m_scr[hh] = m_new

    def reset():
        m_scr[...] = jnp.full(m_scr.shape, NEG_BIG, jnp.float32)
        acc_scr[...] = jnp.zeros(acc_scr.shape, jnp.float32)

    def finalize(qr):
        oa = acc_scr[0, :LANES] / acc_scr[0, LANES:LANES + 1]
        ob = acc_scr[1, :LANES] / acc_scr[1, LANES:LANES + 1]
        rows = pl.ds(pl.multiple_of(qr * tq, tq), tq)
        if mode == "mla":
            sub = lax.broadcasted_iota(jnp.int32, oa.shape, 0)
            o_ref[0, rows, :] = jnp.where(sub < MLA_V, oa, ob).T.astype(o_ref.dtype)
        else:
            lp = lam_ref[...]
            lam = (jnp.exp(jnp.sum(lp[0:1] * lp[1:2], axis=-1, keepdims=True))
                   - jnp.exp(jnp.sum(lp[2:3] * lp[3:4], axis=-1, keepdims=True)) + LAMBDA_INIT)
            o = oa - lam * ob
            o = o * lax.rsqrt(jnp.sum(o * o, axis=0, keepdims=True) * (1.0 / DIFF_V) + NORM_EPS)
            o_ref[0, rows, :] = (o.T * subln_ref[...] * (1.0 - LAMBDA_INIT)).astype(o_ref.dtype)

    def next_diag(qr):
        nxt = jnp.minimum(qr + 1, nq - 1)
        scores(nxt, nxt, DIAG, True)

    reset()
    scores(0, 0, DIAG, True)
    softmax_pv(0, DIAG)
    finalize(0)
    if nq > 1:
        scores(1, 1, DIAG, True)

    def row(qr, carry):
        reset()
        scores(qr, 0, 0, False)
        softmax_pv(qr, DIAG)
        rest = qr - 1

        def pair(p):
            scores(qr, 2 * p + 1, 1, False)
            softmax_pv(2 * p, 0)
            scores(qr, 2 * p + 2, 0, False)
            softmax_pv(2 * p + 1, 1)

        def quad(j, c):
            pair(2 * j)
            pair(2 * j + 1)
            return c

        lax.fori_loop(0, rest // 4, quad, 0)

        @pl.when(rest % 4 >= 2)
        def _():
            pair(rest // 4 * 2)

        @pl.when(rest % 2 == 0)
        def _():
            softmax_pv(qr - 1, 0)
            next_diag(qr)

        @pl.when(rest % 2 == 1)
        def _():
            scores(qr, qr - 1, 1, False)
            softmax_pv(qr - 2, 0)
            softmax_pv(qr - 1, 1)
            next_diag(qr)

        finalize(qr)
        return carry

    lax.fori_loop(1, nq, row, 0)


def _attn_call(q, k, vt, lam, subln, *, mode):
    b, _, s, _ = q.shape
    n_groups = vt.shape[1]
    kh = k.shape[1] // n_groups
    tq, tk = _att_tiles(s)
    assert tk == tq
    kernel = functools.partial(_attn_kernel, mode=mode, tq=tq)
    return pl.pallas_call(
        kernel,
        grid=(b, n_groups),
        in_specs=[
            pl.BlockSpec((1, 2, s, LANES), lambda bi, g: (bi, g, 0, 0)),
            pl.BlockSpec((1, kh, s, LANES), lambda bi, g: (bi, g, 0, 0)),
            pl.BlockSpec((1, 1, s // tk, VT_ROWS, tk), lambda bi, g: (bi, g, 0, 0, 0)),
            _const_spec(lam.shape),
            _const_spec(subln.shape),
        ],
        out_specs=pl.BlockSpec((1, s, LANES), lambda bi, g: (bi, 0, g)),
        out_shape=jax.ShapeDtypeStruct((b, s, n_groups * LANES), jnp.bfloat16),
        scratch_shapes=[
            pltpu.VMEM((3, 2, tq, tq), jnp.float32),
            pltpu.VMEM((3, 2, 1, tq), jnp.float32),
            pltpu.VMEM((2, 1, tq), jnp.float32),
            pltpu.VMEM((2, VT_ROWS, tq), jnp.float32),
        ],
        compiler_params=pltpu.CompilerParams(
            dimension_semantics=("arbitrary", "arbitrary"), vmem_limit_bytes=V7X_VMEM_LIMIT),
        name="attn_" + mode,
    )(q, k, vt, lam, subln)


def _to_token_tiles(ref, mat):
    n = mat.shape[0]
    for j in range(TOK_ROWS):
        ref[pl.ds(j, n, stride=TOK_ROWS), :] = mat[:, j * LANES:(j + 1) * LANES]


def _from_token_tiles(ref, n):
    return jnp.concatenate([ref[pl.ds(j, n, stride=TOK_ROWS), :] for j in range(TOK_ROWS)], axis=1)


def _tok_at(ref, row):
    return ref.at[pl.ds(pl.multiple_of(row, TOK_ROWS), TOK_ROWS)]


def _tok(ref, t):
    return _tok_at(ref, t * TOK_ROWS)


def _copy_token_run(src, dst, src_tok, dst_tok, count, sem, top_bit):
    def chunks(hi_bit, lo_bit):
        bit = hi_bit
        while bit >= lo_bit:
            done = count & ~(2 * bit - 1)

            @pl.when((count & bit) != 0)
            def _(bit=bit, done=done):
                s = pl.multiple_of((src_tok + done) * TOK_ROWS, TOK_ROWS)
                d = pl.multiple_of((dst_tok + done) * TOK_ROWS, TOK_ROWS)
                pltpu.make_async_copy(src.at[pl.ds(s, bit * TOK_ROWS)], dst.at[pl.ds(d, bit * TOK_ROWS)], sem).start()

            bit //= 2

    if top_bit >= RUN_COMMON:
        @pl.when(count >= RUN_COMMON)
        def _():
            chunks(top_bit, RUN_COMMON)

        chunks(RUN_COMMON // 2, 1)
    else:
        chunks(top_bit, 1)


def _route_kernel(x_ref, oa_ref, ob_ref, sg_ref, wbm_ref, wbd_ref, wout_ref, gffn_ref, wrh_ref, wrl_ref, br_ref,
                  h_ref, hn_ref, idx_ref, w_ref, rank_ref, tcnt_ref, tbase_ref, cnt_ref, carry_scr):
    @pl.when(pl.program_id(0) == 0)
    def _():
        carry_scr[...] = jnp.zeros(carry_scr.shape, jnp.float32)

    sg = sg_ref[...]
    ma = _f32dot(oa_ref[...], wbm_ref[...])
    mb = _f32dot(ob_ref[...], wbd_ref[...])
    merged = sg[:, :D_MODEL] * ma + sg[:, D_MODEL:] * mb
    h = x_ref[...] + _f32dot(merged.astype(jnp.bfloat16), wout_ref[...])
    h_ref[...] = h

    hn = h * _rms_scale(h, D_MODEL) * gffn_ref[...]
    _to_token_tiles(hn_ref, hn)
    hn_hi = hn.astype(jnp.bfloat16)
    hn_lo = (hn - hn_hi.astype(jnp.float32)).astype(jnp.bfloat16)
    wrh = wrh_ref[...]
    logits = _f32dot(hn_hi, wrh) + _f32dot(hn_lo, wrh) + _f32dot(hn_hi, wrl_ref[...]) + br_ref[...]

    tm = logits.shape[0]
    lane = _lane_iota(logits.shape)
    work = logits
    vals, picks, onehots = [], [], []
    for _k in range(TOP_K):
        mx = jnp.max(work, axis=-1, keepdims=True)
        sel = jnp.min(jnp.where(work == mx, lane, LANES), axis=-1, keepdims=True)
        oh = lane == sel
        vals.append(mx)
        picks.append(sel)
        onehots.append(oh)
        work = jnp.where(oh, NEG_BIG * 2, work)
    exps = [jnp.exp(vv - vals[0]) for vv in vals]
    denom = exps[0] + exps[1] + exps[2] + exps[3]

    chosen = jnp.zeros(logits.shape, jnp.float32)
    for oh in onehots:
        chosen = chosen + oh.astype(jnp.float32)
    r_i = lax.broadcasted_iota(jnp.int32, (tm, tm), 0)
    c_i = lax.broadcasted_iota(jnp.int32, (tm, tm), 1)
    ltri = (c_i < r_i).astype(jnp.bfloat16)
    before = _f32dot(ltri, chosen.astype(jnp.bfloat16))
    tile_counts = jnp.sum(chosen, axis=0, keepdims=True)
    tbase_ref[0] = carry_scr[...]
    tcnt_ref[0] = tile_counts
    carry_scr[...] = carry_scr[...] + tile_counts
    cnt_ref[...] = carry_scr[...]

    idx_out = jnp.zeros(logits.shape, jnp.float32)
    w_out = jnp.zeros(logits.shape, jnp.float32)
    rank_out = jnp.zeros(logits.shape, jnp.float32)
    for kk in range(TOP_K):
        oh = onehots[kk]
        r_k = jnp.sum(jnp.where(oh, before, 0.0), axis=-1, keepdims=True)
        idx_out = jnp.where(lane == kk, picks[kk].astype(jnp.float32), idx_out)
        rank_out = jnp.where(lane == kk, r_k, rank_out)
        w_out = jnp.where(lane == kk, exps[kk] / denom, w_out)
    idx_ref[0] = idx_out.T[:8]
    w_ref[0] = w_out.T[:8]
    rank_ref[0] = rank_out.T[:8]


def _route_call(x2, oa, ob, sg, wbm, wbd, wout, gffn, wrh, wrl, br):
    n = x2.shape[0]
    tm = min(ROUTE_TM, n)
    nt = n // tm
    row = lambda w: pl.BlockSpec((tm, w), lambda i: (i, 0))
    per_tile = pl.BlockSpec((1, 1, LANES), lambda i: (i, 0, 0))
    per_choice = pl.BlockSpec((1, 8, tm), lambda i: (i, 0, 0))
    out_shape = (
        jax.ShapeDtypeStruct((n, D_MODEL), jnp.float32),
        jax.ShapeDtypeStruct((n * TOK_ROWS, LANES), jnp.float32),
        jax.ShapeDtypeStruct((nt, 8, tm), jnp.float32),
        jax.ShapeDtypeStruct((nt, 8, tm), jnp.float32),
        jax.ShapeDtypeStruct((nt, 8, tm), jnp.float32),
        jax.ShapeDtypeStruct((nt, 1, LANES), jnp.float32),
        jax.ShapeDtypeStruct((nt, 1, LANES), jnp.float32),
        jax.ShapeDtypeStruct((1, LANES), jnp.float32),
    )
    return pl.pallas_call(
        _route_kernel,
        grid=(nt,),
        in_specs=[row(D_MODEL), row(512), row(512), row(2 * D_MODEL),
                  _const_spec(wbm.shape), _const_spec(wbd.shape), _const_spec(wout.shape), _const_spec(gffn.shape),
                  _const_spec(wrh.shape), _const_spec(wrl.shape), _const_spec(br.shape)],
        out_specs=(row(D_MODEL), pl.BlockSpec((tm * TOK_ROWS, LANES), lambda i: (i, 0)), per_choice, per_choice,
                   per_choice, per_tile, per_tile, _const_spec((1, LANES))),
        out_shape=out_shape,
        scratch_shapes=[pltpu.VMEM((1, LANES), jnp.float32)],
        compiler_params=pltpu.CompilerParams(dimension_semantics=("arbitrary",), vmem_limit_bytes=V7X_VMEM_LIMIT),
        name="route",
    )(x2, oa, ob, sg, wbm, wbd, wout, gffn, wrh, wrl, br)


def _dispatch_kernel(seg_ref, lp_ref, pad_ref, hn_ref, xs_ref, scr, zscr, sems, zsem):
    step = pl.program_id(0)
    last = pl.num_programs(0) - 1
    tm = hn_ref.shape[0] // TOK_ROWS
    par = step % 2
    buf = scr.at[par]

    def place(t, carry):
        v = _tok(hn_ref, t)[...]
        for kk in range(TOP_K):
            _tok_at(buf, lp_ref[0, 0, t * TOP_K + kk])[...] = v
        return carry

    lax.fori_loop(0, tm, place, 0, unroll=TOK_UNROLL)

    def run(e, carry):
        _copy_token_run(buf, xs_ref, seg_ref[0, 0, N_EXPERTS + e], seg_ref[0, 0, 2 * N_EXPERTS + e],
                        seg_ref[0, 0, e], sems.at[par], tm)
        return carry

    lax.fori_loop(0, N_EXPERTS, run, 0)

    def drain(p):
        pltpu.make_async_copy(scr.at[p], xs_ref.at[pl.ds(0, scr.shape[1])], sems.at[p]).wait()

    @pl.when(step == 0)
    def _():
        zscr[...] = jnp.zeros(zscr.shape, zscr.dtype)

        def pad_run(e, carry):
            _copy_token_run(zscr, xs_ref, 0, pad_ref[e], pad_ref[N_EXPERTS + e], zsem, FFN_TM // 2)
            return carry

        lax.fori_loop(0, N_EXPERTS, pad_run, 0)

        def pad_tile(j, carry):
            d = pl.multiple_of((pad_ref[2 * N_EXPERTS] + j * FFN_TM) * TOK_ROWS, TOK_ROWS)
            pltpu.make_async_copy(zscr, xs_ref.at[pl.ds(d, FFN_TM * TOK_ROWS)], zsem).start()
            return carry

        lax.fori_loop(0, pad_ref[2 * N_EXPERTS + 1], pad_tile, 0)

    @pl.when(step > 0)
    def _():
        drain(1 - par)

    @pl.when(step == last)
    def _():
        drain(par)
        for _e in range(N_EXPERTS):
            pltpu.make_async_copy(zscr, xs_ref.at[pl.ds(0, FFN_TM * TOK_ROWS)], zsem).wait()


def _dispatch_call(seg3, lp3, pad, hn, n_rows):
    nt = seg3.shape[0]
    tm = hn.shape[0] // TOK_ROWS // nt
    smem = lambda a: pl.BlockSpec((1, 1, a.shape[2]), lambda i: (i, 0, 0), memory_space=pltpu.SMEM)
    return pl.pallas_call(
        _dispatch_kernel,
        grid=(nt,),
        in_specs=[
            smem(seg3), smem(lp3),
            pl.BlockSpec(memory_space=pltpu.SMEM),
            pl.BlockSpec((tm * TOK_ROWS, LANES), lambda i: (i, 0)),
        ],
        out_specs=pl.BlockSpec(memory_space=pl.ANY),
        out_shape=jax.ShapeDtypeStruct((n_rows * TOK_ROWS, LANES), jnp.float32),
        scratch_shapes=[pltpu.VMEM((2, TOP_K * tm * TOK_ROWS, LANES), jnp.float32),
                        pltpu.VMEM((FFN_TM * TOK_ROWS, LANES), jnp.float32),
                        pltpu.SemaphoreType.DMA((2,)), pltpu.SemaphoreType.DMA(())],
        compiler_params=pltpu.CompilerParams(dimension_semantics=("arbitrary",), vmem_limit_bytes=V7X_VMEM_LIMIT),
        name="dispatch",
    )(seg3, lp3, pad, hn)


def _ffn_kernel(te_ref, tv_ref, nt_ref, xs_ref, w1_ref, b1_ref, w2_ref, b2_ref, y_ref, w1b, w2b):
    i = pl.program_id(0)
    tm = xs_ref.shape[0] // TOK_ROWS

    @pl.when((i == 0) | (te_ref[i] != te_ref[jnp.maximum(i - 1, 0)]))
    def _():
        w1b[...] = w1_ref[0].astype(jnp.bfloat16)
        w2b[...] = w2_ref[0].astype(jnp.bfloat16)

    @pl.when(i < nt_ref[0])
    def _():
        rows = lax.broadcasted_iota(jnp.int32, (tm, 1), 0)
        xt = jnp.where(rows < tv_ref[i], _from_token_tiles(xs_ref, tm), 0.0).astype(jnp.bfloat16)
        u = _f32dot(xt, w1b[...]) + b1_ref[0]
        glu = jnp.minimum(u[:, :D_EXPERT], SWIGLU_LIMIT)
        lin = jnp.clip(u[:, D_EXPERT:], -SWIGLU_LIMIT, SWIGLU_LIMIT)
        a = glu * jax.nn.sigmoid(SWIGLU_ALPHA * glu) * (lin + 1.0)
        y = _f32dot(a.astype(jnp.bfloat16), w2b[...]) + b2_ref[0]
        _to_token_tiles(y_ref, y)

    @pl.when(i >= nt_ref[0])
    def _():
        y_ref[...] = jnp.zeros(y_ref.shape, y_ref.dtype)


def _ffn_call(tile_expert, tile_valid, n_tiles, xs, w1, b1, w2, b2):
    n_rows = xs.shape[0] // TOK_ROWS
    tm = FFN_TM
    t_tiles = n_rows // tm

    def tile_map(i, te, tv, nt):
        return (jnp.minimum(i, nt[0] - 1), 0)

    def out_map(i, te, tv, nt):
        return (i, 0)

    def w_map(i, te, tv, nt):
        return (te[i], 0, 0)

    grid_spec = pltpu.PrefetchScalarGridSpec(
        num_scalar_prefetch=3,
        grid=(t_tiles,),
        in_specs=[
            pl.BlockSpec((tm * TOK_ROWS, LANES), tile_map),
            pl.BlockSpec((1, D_MODEL, 2 * D_EXPERT), w_map),
            pl.BlockSpec((1, 1, 2 * D_EXPERT), w_map),
            pl.BlockSpec((1, D_EXPERT, D_MODEL), w_map),
            pl.BlockSpec((1, 1, D_MODEL), w_map),
        ],
        out_specs=pl.BlockSpec((tm * TOK_ROWS, LANES), out_map),
        scratch_shapes=[pltpu.VMEM((D_MODEL, 2 * D_EXPERT), jnp.bfloat16), pltpu.VMEM((D_EXPERT, D_MODEL), jnp.bfloat16)],
    )
    return pl.pallas_call(
        _ffn_kernel,
        grid_spec=grid_spec,
        out_shape=jax.ShapeDtypeStruct(xs.shape, jnp.float32),
        compiler_params=pltpu.CompilerParams(dimension_semantics=("arbitrary",), vmem_limit_bytes=V7X_VMEM_LIMIT),
        name="ffn",
    )(tile_expert, tile_valid, n_tiles, xs, w1, b1, w2, b2)


def _combine_kernel(seg_ref, snext_ref, lp_ref, w_ref, h_ref, y_ref, o_ref, ybuf, tok_scr, sems):
    step = pl.program_id(0)
    last = pl.num_programs(0) - 1
    tm = h_ref.shape[0]
    par = step % 2

    def fetch(s_ref, slot):
        def run(e, carry):
            _copy_token_run(y_ref, ybuf.at[slot], s_ref[0, 0, 2 * N_EXPERTS + e], s_ref[0, 0, N_EXPERTS + e],
                            s_ref[0, 0, e], sems.at[slot], tm)
            return carry

        lax.fori_loop(0, N_EXPERTS, run, 0)

    @pl.when(step == 0)
    def _():
        fetch(seg_ref, 0)

    @pl.when(step < last)
    def _():
        fetch(snext_ref, 1 - par)

    pltpu.make_async_copy(y_ref.at[pl.ds(0, ybuf.shape[1])], ybuf.at[par], sems.at[par]).wait()
    buf = ybuf.at[par]

    def token(t, carry):
        acc = None
        for kk in range(TOP_K):
            term = w_ref[0, 0, t * TOP_K + kk] * _tok_at(buf, lp_ref[0, 0, t * TOP_K + kk])[...]
            acc = term if acc is None else acc + term
        _tok(tok_scr, t)[...] = acc
        return carry

    lax.fori_loop(0, tm, token, 0, unroll=TOK_UNROLL)
    o_ref[...] = h_ref[...] + _from_token_tiles(tok_scr, tm)


def _combine_call(seg3, lp3, w3, h, y):
    n = h.shape[0]
    nt = seg3.shape[0]
    tm = n // nt
    smem = lambda a, f: pl.BlockSpec((1, 1, a.shape[2]), f, memory_space=pltpu.SMEM)
    here = lambda i: (i, 0, 0)
    return pl.pallas_call(
        _combine_kernel,
        grid=(nt,),
        in_specs=[
            smem(seg3, here), smem(seg3, lambda i: (jnp.minimum(i + 1, nt - 1), 0, 0)), smem(lp3, here), smem(w3, here),
            pl.BlockSpec((tm, D_MODEL), lambda i: (i, 0)),
            pl.BlockSpec(memory_space=pl.ANY),
        ],
        out_specs=pl.BlockSpec((tm, D_MODEL), lambda i: (i, 0)),
        out_shape=jax.ShapeDtypeStruct((n, D_MODEL), jnp.float32),
        scratch_shapes=[pltpu.VMEM((2, TOP_K * tm * TOK_ROWS, LANES), jnp.float32),
                        pltpu.VMEM((tm * TOK_ROWS, LANES), jnp.float32), pltpu.SemaphoreType.DMA((2,))],
        compiler_params=pltpu.CompilerParams(dimension_semantics=("arbitrary",), vmem_limit_bytes=V7X_VMEM_LIMIT),
        name="combine",
    )(seg3, seg3, lp3, w3, h, y)


def _rot_cols(w, group):
    lead = w.shape[:-1]
    g = w.reshape(lead + (-1, group))
    half = group // 2
    return jnp.concatenate([-g[..., half:], g[..., :half]], axis=-1).reshape(w.shape)


def _rope_tables(s, dim):
    inv_freq = 1.0 / (ROPE_THETA ** (jnp.arange(0, dim, 2, dtype=jnp.float32) / dim))
    ang = jnp.arange(s, dtype=jnp.float32)[:, None] * inv_freq[None, :]
    return jnp.cos(ang), jnp.sin(ang)


def _mla_tables(s, gain, scale):
    cos, sin = _rope_tables(s, MLA_ROPE)
    cos2 = jnp.concatenate([cos, cos], axis=-1)
    sin2 = jnp.concatenate([sin, sin], axis=-1)
    g_nope, g_rope = gain[:MLA_NOPE], gain[MLA_NOPE:]
    g_perm = jnp.concatenate([g_rope[MLA_ROPE // 2:], g_rope[:MLA_ROPE // 2]])
    pad = jnp.zeros((s, LANES - MLA_QK), jnp.float32)
    c = jnp.concatenate([jnp.broadcast_to(g_nope, (s, MLA_NOPE)), cos2 * g_rope, pad], axis=-1) * scale
    sn = jnp.concatenate([jnp.zeros((s, MLA_NOPE), jnp.float32), sin2 * g_perm, pad], axis=-1) * scale
    return c, sn


def _diff_tables(s, gain, scale):
    cos, sin = _rope_tables(s, DIFF_D)
    cos2 = jnp.concatenate([cos, cos], axis=-1)
    sin2 = jnp.concatenate([sin, sin], axis=-1)
    g_perm = jnp.concatenate([gain[DIFF_D // 2:], gain[:DIFF_D // 2]])
    c = cos2 * gain * scale
    sn = sin2 * g_perm * scale
    return jnp.concatenate([c, c], axis=-1), jnp.concatenate([sn, sn], axis=-1)


def _pad_heads(w, width):
    kdim = w.shape[0]
    g = w.reshape(kdim, -1, width)
    g = jnp.pad(g, ((0, 0), (0, 0), (0, LANES - width)))
    return g.reshape(kdim, -1)


def kernel(x, attn_norm, w_in, mla_q_lat_norm, mla_kv_lat_norm, mla_w_uq, mla_w_ukv, mla_q_norm, mla_k_norm,
           diff_q_norm, diff_k_norm, diff_lambda_q1, diff_lambda_k1, diff_lambda_q2, diff_lambda_k2, diff_subln,
           w_branch_mla, w_branch_diff, w_out, ffn_norm, w_router, b_router, w_mlp1, b_mlp1, w_mlp2, b_mlp2):
    b, s, d = x.shape
    n = b * s
    bf = jnp.bfloat16
    f32 = jnp.float32
    i = 0

    wi = w_in[i]
    c0 = MLA_Q_LORA + MLA_KV_LORA
    w_kpe = wi[:, c0:c0 + MLA_ROPE]
    c1 = c0 + MLA_ROPE
    w_dq, w_dk, w_dv = wi[:, c1:c1 + 512], wi[:, c1 + 512:c1 + 1024], wi[:, c1 + 1024:c1 + 1536]
    w_gate = wi[:, c1 + 1536:]
    place = lambda w: jnp.pad(w, ((0, 0), (MLA_NOPE, LANES - MLA_QK)))
    w1 = jnp.concatenate([
        wi[:, :c0], place(w_kpe), place(_rot_cols(w_kpe, MLA_ROPE)),
        w_dq, _rot_cols(w_dq, DIFF_D), w_dk, _rot_cols(w_dk, DIFF_D), w_dv, w_gate], axis=1).astype(bf)

    wuq = mla_w_uq[i].reshape(MLA_Q_LORA, MLA_HEADS, MLA_QK)
    wuq_rot = jnp.concatenate(
        [jnp.zeros((MLA_Q_LORA, MLA_HEADS, MLA_NOPE), f32), _rot_cols(wuq[..., MLA_NOPE:], MLA_ROPE)], axis=-1)
    wq = jnp.concatenate([_pad_heads(wuq.reshape(MLA_Q_LORA, -1), MLA_QK),
                          _pad_heads(wuq_rot.reshape(MLA_Q_LORA, -1), MLA_QK)], axis=1).astype(bf)
    wukv = mla_w_ukv[i].reshape(MLA_KV_LORA, MLA_HEADS, MLA_NOPE + MLA_V)
    wkv = jnp.concatenate([_pad_heads(wukv[..., :MLA_NOPE].reshape(MLA_KV_LORA, -1), MLA_NOPE),
                           wukv[..., MLA_NOPE:].reshape(MLA_KV_LORA, -1)], axis=1).astype(bf)

    cq, sq = _mla_tables(s, mla_q_norm[i], LOG2E / math.sqrt(MLA_QK))
    ck, sk = _mla_tables(s, mla_k_norm[i], 1.0)
    cqd, sqd = _diff_tables(s, diff_q_norm[i], LOG2E / math.sqrt(DIFF_D))
    ckd, skd = _diff_tables(s, diff_k_norm[i], 1.0)
    tabs = jnp.stack([cq, sq, ck, sk, cqd, sqd, ckd, skd])

    glat = jnp.concatenate([mla_q_lat_norm[i], mla_kv_lat_norm[i]])[None, :]
    qa, ka, va, qd, kd, vd, sg = _prep_call(x, attn_norm[i][None, :], w1, glat, wq, wkv, tabs)

    lam = jnp.pad(jnp.stack([diff_lambda_q1[i], diff_lambda_k1[i], diff_lambda_q2[i], diff_lambda_k2[i]]),
                  ((0, 0), (0, LANES - DIFF_D)))
    subln = diff_subln[i][None, :]
    o_a = _attn_call(qa, ka, va, lam, subln, mode="mla")
    o_b = _attn_call(qd, kd, vd, lam, subln, mode="diff")

    wr = jnp.pad(w_router[i], ((0, 0), (0, LANES - N_EXPERTS)))
    wr_hi = wr.astype(bf)
    wr_lo = (wr - wr_hi.astype(f32)).astype(bf)
    br = jnp.concatenate([b_router[i], jnp.full((LANES - N_EXPERTS,), NEG_BIG, f32)])[None, :]
    h, hn, idx, gw, rank, tcnt, tbase, cnt = _route_call(
        x.reshape(n, d), o_a.reshape(n, 512), o_b.reshape(n, 512), sg.reshape(n, 2 * d),
        w_branch_mla[i].astype(bf), w_branch_diff[i].astype(bf), w_out[i].astype(bf), ffn_norm[i][None, :],
        wr_hi, wr_lo, br)

    i32 = jnp.int32
    counts = cnt[0, :N_EXPERTS].astype(i32)
    tiles_per = (counts + FFN_TM - 1) // FFN_TM
    tile_end = jnp.cumsum(tiles_per)
    tile_start = tile_end - tiles_per
    n_tiles = tile_end[-1:]
    t_tiles = n * TOP_K // FFN_TM + N_EXPERTS
    tids = jnp.arange(t_tiles, dtype=i32)
    tile_expert = jnp.minimum(jnp.sum(tids[:, None] >= tile_end[None, :], axis=1), N_EXPERTS - 1).astype(i32)
    tile_valid = jnp.clip(counts[tile_expert] - (tids - tile_start[tile_expert]) * FFN_TM, 0, FFN_TM).astype(i32)

    nt = tcnt.shape[0]
    tc = tcnt[:, 0, :N_EXPERTS].astype(i32)
    tb = tbase[:, 0, :N_EXPERTS].astype(i32)
    off = jnp.cumsum(tc, axis=1) - tc
    first = (tile_start * FFN_TM)[None, :] + tb
    seg3 = jnp.concatenate([tc, off, first], axis=1).reshape(nt, 1, 3 * N_EXPERTS)
    per = n // nt * TOP_K
    by_token = lambda a: a[:, :TOP_K, :].transpose(0, 2, 1).reshape(nt, per)
    picked = by_token(idx).astype(i32)[:, :, None] == jnp.arange(N_EXPERTS, dtype=i32)
    lp3 = ((jnp.sum(jnp.where(picked, off[:, None, :], 0), axis=-1)
            + by_token(rank).astype(i32)) * TOK_ROWS).reshape(nt, 1, per)
    w3 = by_token(gw).reshape(nt, 1, per)
    pad = jnp.concatenate([tile_start * FFN_TM + counts, tiles_per * FFN_TM - counts,
                           n_tiles * FFN_TM, t_tiles - n_tiles]).astype(i32)

    xs = _dispatch_call(seg3, lp3, pad, hn, t_tiles * FFN_TM)
    y = _ffn_call(tile_expert, tile_valid, n_tiles.astype(i32), xs,
                  w_mlp1[i], b_mlp1[i][:, None, :], w_mlp2[i], b_mlp2[i][:, None, :])
    out = _combine_call(seg3, lp3, w3, h, y)
    return out.reshape(b, s, d)
```

```python
import functools
import math

import jax
import jax.numpy as jnp
from jax import lax
from jax.experimental import pallas as pl
from jax.experimental.pallas import tpu as pltpu

D_MODEL = 1024
ROPE_THETA = 10000.0
NORM_EPS = 1e-6
MLA_HEADS = 8
MLA_NOPE = 64
MLA_ROPE = 32
MLA_QK = MLA_NOPE + MLA_ROPE
MLA_V = 64
MLA_Q_LORA = 768
MLA_KV_LORA = 256
DIFF_HEADS = 4
DIFF_D = 64
DIFF_V = 128
MLA_WIDTH = MLA_HEADS * MLA_V
DIFF_WIDTH = DIFF_HEADS * DIFF_V
N_EXPERTS = 32
TOP_K = 4
D_EXPERT = 1024
SWIGLU_ALPHA = 1.702
SWIGLU_LIMIT = 7.0
LAMBDA_INIT = 0.8 - 0.6 * math.exp(-0.3 * 0)

LANES = 128
SUBLANES = 8
HEADS_PADDED = MLA_HEADS * LANES
V7X_VMEM_LIMIT = 56 * 1024 * 1024

PREP_TM = 256
ATT_TQ = 512
ROUTE_TM = 512
VT_ROWS = 144
TOK_ROWS = 8
TOK_UNROLL = 8
FFN_TM = 512

NEG_BIG = -1e30
LOG2E = math.log2(math.e)

_C_LAT = 0
_C_KPE = 1024
_C_DQ = 1280
_C_DK = 2304
_C_DV = 3328
_C_GATE = 3840
_C_END = 5888


def _f32dot(a, b):
    return jnp.dot(a, b, preferred_element_type=jnp.float32)


def _dot_nt(a, b):
    return lax.dot_general(a, b, (((1,), (1,)), ((), ())), preferred_element_type=jnp.float32)


def _rms_scale(x, width):
    return lax.rsqrt(jnp.sum(x * x, axis=-1, keepdims=True) * (1.0 / width) + NORM_EPS)


def _lane_iota(shape):
    return lax.broadcasted_iota(jnp.int32, shape, len(shape) - 1)


def _prep_kernel(x_ref, gattn_ref, w1_ref, glat_ref, wq_ref, wkv_ref, tabs_ref,
                 qa_ref, ka_ref, va_ref, qd_ref, kd_ref, vd_ref, sg_ref):
    x = x_ref[0]
    xn = x * _rms_scale(x, D_MODEL) * gattn_ref[...]
    proj = _f32dot(xn.astype(jnp.bfloat16), w1_ref[...])

    glat = glat_ref[...]
    q_lat = proj[:, 0:MLA_Q_LORA]
    kv_lat = proj[:, MLA_Q_LORA:MLA_Q_LORA + MLA_KV_LORA]
    qn = q_lat * _rms_scale(q_lat, MLA_Q_LORA) * glat[:, 0:MLA_Q_LORA]
    kvn = kv_lat * _rms_scale(kv_lat, MLA_KV_LORA) * glat[:, MLA_Q_LORA:]
    qq = _f32dot(qn.astype(jnp.bfloat16), wq_ref[...])
    kk = _f32dot(kvn.astype(jnp.bfloat16), wkv_ref[...])

    kpe = proj[:, _C_KPE:_C_KPE + LANES]
    kper = proj[:, _C_KPE + LANES:_C_KPE + 2 * LANES]
    cq, sq = tabs_ref[0], tabs_ref[1]
    ck, sk = tabs_ref[2], tabs_ref[3]
    for h in range(MLA_HEADS):
        lo, hi = h * LANES, (h + 1) * LANES
        qh = qq[:, lo:hi]
        qa_ref[0, h] = (_rms_scale(qh, MLA_QK) * (qh * cq + qq[:, HEADS_PADDED + lo:HEADS_PADDED + hi] * sq)).astype(jnp.bfloat16)
        kh = kk[:, lo:hi] + kpe
        ka_ref[0, h] = (_rms_scale(kh, MLA_QK) * (kh * ck + kper * sk)).astype(jnp.bfloat16)
    ones_rows = (lax.broadcasted_iota(jnp.int32, (VT_ROWS - LANES, x.shape[0]), 0) == 0).astype(jnp.bfloat16)
    for j in range(MLA_HEADS // 2):
        va_ref[0, j, 0, :LANES, :] = kk[:, HEADS_PADDED + j * LANES:HEADS_PADDED + (j + 1) * LANES].T.astype(jnp.bfloat16)
        va_ref[0, j, 0, LANES:, :] = ones_rows

    cqd, sqd = tabs_ref[4], tabs_ref[5]
    ckd, skd = tabs_ref[6], tabs_ref[7]
    lane = _lane_iota((x.shape[0], LANES))
    first = lane < DIFF_D

    def half_norm(v):
        sqv = v * v
        s_all = jnp.sum(sqv, axis=-1, keepdims=True)
        s_lo = jnp.sum(jnp.where(first, sqv, 0.0), axis=-1, keepdims=True)
        r_lo = lax.rsqrt(s_lo * (1.0 / DIFF_D) + NORM_EPS)
        r_hi = lax.rsqrt((s_all - s_lo) * (1.0 / DIFF_D) + NORM_EPS)
        return jnp.where(first, r_lo, r_hi)

    for h in range(DIFF_HEADS):
        lo, hi = h * LANES, (h + 1) * LANES
        dq = proj[:, _C_DQ + lo:_C_DQ + hi]
        dqr = proj[:, _C_DQ + DIFF_WIDTH + lo:_C_DQ + DIFF_WIDTH + hi]
        qv = half_norm(dq) * (dq * cqd + dqr * sqd)
        qd_ref[0, 2 * h] = jnp.where(first, qv, 0.0).astype(jnp.bfloat16)
        qd_ref[0, 2 * h + 1] = jnp.where(first, 0.0, qv).astype(jnp.bfloat16)
        dk = proj[:, _C_DK + lo:_C_DK + hi]
        dkr = proj[:, _C_DK + DIFF_WIDTH + lo:_C_DK + DIFF_WIDTH + hi]
        kd_ref[0, h] = (half_norm(dk) * (dk * ckd + dkr * skd)).astype(jnp.bfloat16)
        vd_ref[0, h, 0, :LANES, :] = proj[:, _C_DV + lo:_C_DV + hi].T.astype(jnp.bfloat16)
        vd_ref[0, h, 0, LANES:, :] = ones_rows

    sg_ref[0] = jax.nn.sigmoid(proj[:, _C_GATE:_C_END]).astype(jnp.bfloat16)


def _att_tiles(s):
    tq = min(ATT_TQ, s)
    return tq, tq


def _const_spec(shape):
    nd = len(shape)
    return pl.BlockSpec(shape, lambda *_: (0,) * nd)


def _prep_call(x, gattn, w1, glat, wq, wkv, tabs):
    b, s, _ = x.shape
    tm = min(PREP_TM, s)
    grid = (b, s // tm)
    head_out = lambda nh: pl.BlockSpec((1, nh, tm, LANES), lambda bi, ti: (bi, 0, ti, 0))
    _, tk = _att_tiles(s)
    per = tk // tm
    vt_out = lambda nh: pl.BlockSpec((1, nh, 1, VT_ROWS, tm), lambda bi, ti: (bi, 0, ti // per, 0, ti % per))
    bf = jnp.bfloat16
    out_shape = (
        jax.ShapeDtypeStruct((b, MLA_HEADS, s, LANES), bf),
        jax.ShapeDtypeStruct((b, MLA_HEADS, s, LANES), bf),
        jax.ShapeDtypeStruct((b, MLA_HEADS // 2, s // tk, VT_ROWS, tk), bf),
        jax.ShapeDtypeStruct((b, 2 * DIFF_HEADS, s, LANES), bf),
        jax.ShapeDtypeStruct((b, DIFF_HEADS, s, LANES), bf),
        jax.ShapeDtypeStruct((b, DIFF_HEADS, s // tk, VT_ROWS, tk), bf),
        jax.ShapeDtypeStruct((b, s, 2 * D_MODEL), bf),
    )
    return pl.pallas_call(
        _prep_kernel,
        grid=grid,
        in_specs=[
            pl.BlockSpec((1, tm, D_MODEL), lambda bi, ti: (bi, ti, 0)),
            _const_spec(gattn.shape),
            _const_spec(w1.shape),
            _const_spec(glat.shape),
            _const_spec(wq.shape),
            _const_spec(wkv.shape),
            pl.BlockSpec((8, tm, LANES), lambda bi, ti: (0, ti, 0)),
        ],
        out_specs=(
            head_out(MLA_HEADS), head_out(MLA_HEADS), vt_out(MLA_HEADS // 2),
            head_out(2 * DIFF_HEADS), head_out(DIFF_HEADS), vt_out(DIFF_HEADS),
            pl.BlockSpec((1, tm, 2 * D_MODEL), lambda bi, ti: (bi, ti, 0)),
        ),
        out_shape=out_shape,
        compiler_params=pltpu.CompilerParams(
            dimension_semantics=("arbitrary", "arbitrary"), vmem_limit_bytes=V7X_VMEM_LIMIT),
        name="prep",
    )(x, gattn, w1, glat, wq, wkv, tabs)


def _attn_kernel(q_ref, k_ref, vt_ref, lam_ref, subln_ref, o_ref, st_buf, mx_buf, m_scr, acc_scr, *, mode, tq):
    nq = q_ref.shape[2] // tq
    shared_k = k_ref.shape[1] == 1
    DIAG = 2

    def scores(qr, kb, slot, masked):
        q0 = pl.multiple_of(qr * tq, tq)
        k0 = pl.multiple_of(kb * tq, tq)
        for hh in range(2):
            k = k_ref[0, 0 if shared_k else hh, pl.ds(k0, tq), :]
            st = _dot_nt(k, q_ref[0, hh, pl.ds(q0, tq), :])
            if masked:
                key = lax.broadcasted_iota(jnp.int32, st.shape, 0)
                qry = lax.broadcasted_iota(jnp.int32, st.shape, 1)
                st = jnp.where(key <= qry, st, NEG_BIG)
            st_buf[slot, hh] = st
            mx_buf[slot, hh] = jnp.max(st, axis=0, keepdims=True)

    def softmax_pv(kb, slot):
        vt = vt_ref[0, 0, kb]
        for hh in range(2):
            m_prev = m_scr[hh]
            m_new = jnp.maximum(m_prev, mx_buf[slot, hh])
            alpha = jnp.exp2(m_prev - m_new)
            p = jnp.exp2(st_buf[slot, hh] - m_new)
            acc_scr[hh] = alpha * acc_scr[hh] + _f32dot(vt, p.astype(jnp.bfloat16))
            m_scr[hh] = m_new

    def reset():
        m_scr[...] = jnp.full(m_scr.shape, NEG_BIG, jnp.float32)
        acc_scr[...] = jnp.zeros(acc_scr.shape, jnp.float32)

    def finalize(qr):
        oa = acc_scr[0, :LANES] / acc_scr[0, LANES:LANES + 1]
        ob = acc_scr[1, :LANES] / acc_scr[1, LANES:LANES + 1]
        rows = pl.ds(pl.multiple_of(qr * tq, tq), tq)
        if mode == "mla":
            sub = lax.broadcasted_iota(jnp.int32, oa.shape, 0)
            o_ref[0, rows, :] = jnp.where(sub < MLA_V, oa, ob).T.astype(o_ref.dtype)
        else:
            lp = lam_ref[...]
            lam = (jnp.exp(jnp.sum(lp[0:1] * lp[1:2], axis=-1, keepdims=True))
                   - jnp.exp(jnp.sum(lp[2:3] * lp[3:4], axis=-1, keepdims=True)) + LAMBDA_INIT)
            o = oa - lam * ob
            o = o * lax.rsqrt(jnp.sum(o * o, axis=0, keepdims=True) * (1.0 / DIFF_V) + NORM_EPS)
            o_ref[0, rows, :] = (o.T * subln_ref[...] * (1.0 - LAMBDA_INIT)).astype(o_ref.dtype)

    def next_diag(qr):
        nxt = jnp.minimum(qr + 1, nq - 1)
        scores(nxt, nxt, DIAG, True)

    reset()
    scores(0, 0, DIAG, True)
    softmax_pv(0, DIAG)
    finalize(0)
    if nq > 1:
        scores(1, 1, DIAG, True)

    def row(qr, carry):
        reset()
        scores(qr, 0, 0, False)
        softmax_pv(qr, DIAG)
        rest = qr - 1

        def pair(p):
            scores(qr, 2 * p + 1, 1, False)
            softmax_pv(2 * p, 0)
            scores(qr, 2 * p + 2, 0, False)
            softmax_pv(2 * p + 1, 1)

        def quad(j, c):
            pair(2 * j)
            pair(2 * j + 1)
            return c

        lax.fori_loop(0, rest // 4, quad, 0)

        @pl.when(rest % 4 >= 2)
        def _():
            pair(rest // 4 * 2)

        @pl.when(rest % 2 == 0)
        def _():
            softmax_pv(qr - 1, 0)
            next_diag(qr)

        @pl.when(rest % 2 == 1)
        def _():
            scores(qr, qr - 1, 1, False)
            softmax_pv(qr - 2, 0)
            softmax_pv(qr - 1, 1)
            next_diag(qr)

        finalize(qr)
        return carry

    lax.fori_loop(1, nq, row, 0)


def _attn_call(q, k, vt, lam, subln, *, mode):
    b, _, s, _ = q.shape
    n_groups = vt.shape[1]
    kh = k.shape[1] // n_groups
    tq, tk = _att_tiles(s)
    assert tk == tq
    kernel = functools.partial(_attn_kernel, mode=mode, tq=tq)
    return pl.pallas_call(
        kernel,
        grid=(b, n_groups),
        in_specs=[
            pl.BlockSpec((1, 2, s, LANES), lambda bi, g: (bi, g, 0, 0)),
            pl.BlockSpec((1, kh, s, LANES), lambda bi, g: (bi, g, 0, 0)),
            pl.BlockSpec((1, 1, s // tk, VT_ROWS, tk), lambda bi, g: (bi, g, 0, 0, 0)),
            _const_spec(lam.shape),
            _const_spec(subln.shape),
        ],
        out_specs=pl.BlockSpec((1, s, LANES), lambda bi, g: (bi, 0, g)),
        out_shape=jax.ShapeDtypeStruct((b, s, n_groups * LANES), jnp.bfloat16),
        scratch_shapes=[
            pltpu.VMEM((3, 2, tq, tq), jnp.float32),
            pltpu.VMEM((3, 2, 1, tq), jnp.float32),
            pltpu.VMEM((2, 1, tq), jnp.float32),
            pltpu.VMEM((2, VT_ROWS, tq), jnp.float32),
        ],
        compiler_params=pltpu.CompilerParams(
            dimension_semantics=("arbitrary", "arbitrary"), vmem_limit_bytes=V7X_VMEM_LIMIT),
        name="attn_" + mode,
    )(q, k, vt, lam, subln)


def _to_token_tiles(ref, mat):
    n = mat.shape[0]
    for j in range(TOK_ROWS):
        ref[pl.ds(j, n, stride=TOK_ROWS), :] = mat[:, j * LANES:(j + 1) * LANES]


def _from_token_tiles(ref, n):
    return jnp.concatenate([ref[pl.ds(j, n, stride=TOK_ROWS), :] for j in range(TOK_ROWS)], axis=1)


def _tok_at(ref, row):
    return ref.at[pl.ds(pl.multiple_of(row, TOK_ROWS), TOK_ROWS)]


def _tok(ref, t):
    return _tok_at(ref, t * TOK_ROWS)


def _copy_token_run(src, dst, src_tok, dst_tok, count, sem, top_bit):
    bit = top_bit
    while bit >= 1:
        done = count & ~(2 * bit - 1)

        @pl.when((count & bit) != 0)
        def _(bit=bit, done=done):
            s = pl.multiple_of((src_tok + done) * TOK_ROWS, TOK_ROWS)
            d = pl.multiple_of((dst_tok + done) * TOK_ROWS, TOK_ROWS)
            pltpu.make_async_copy(src.at[pl.ds(s, bit * TOK_ROWS)], dst.at[pl.ds(d, bit * TOK_ROWS)], sem).start()

        bit //= 2


def _route_kernel(x_ref, oa_ref, ob_ref, sg_ref, wbm_ref, wbd_ref, wout_ref, gffn_ref, wrh_ref, wrl_ref, br_ref,
                  h_ref, hn_ref, idx_ref, w_ref, rank_ref, tcnt_ref, tbase_ref, cnt_ref, carry_scr):
    @pl.when(pl.program_id(0) == 0)
    def _():
        carry_scr[...] = jnp.zeros(carry_scr.shape, jnp.float32)

    sg = sg_ref[...]
    ma = _f32dot(oa_ref[...], wbm_ref[...])
    mb = _f32dot(ob_ref[...], wbd_ref[...])
    merged = sg[:, :D_MODEL] * ma + sg[:, D_MODEL:] * mb
    h = x_ref[...] + _f32dot(merged.astype(jnp.bfloat16), wout_ref[...])
    h_ref[...] = h

    hn = h * _rms_scale(h, D_MODEL) * gffn_ref[...]
    _to_token_tiles(hn_ref, hn)
    hn_hi = hn.astype(jnp.bfloat16)
    hn_lo = (hn - hn_hi.astype(jnp.float32)).astype(jnp.bfloat16)
    wrh = wrh_ref[...]
    logits = _f32dot(hn_hi, wrh) + _f32dot(hn_lo, wrh) + _f32dot(hn_hi, wrl_ref[...]) + br_ref[...]

    tm = logits.shape[0]
    lane = _lane_iota(logits.shape)
    work = logits
    vals, picks, onehots = [], [], []
    for _k in range(TOP_K):
        mx = jnp.max(work, axis=-1, keepdims=True)
        sel = jnp.min(jnp.where(work == mx, lane, LANES), axis=-1, keepdims=True)
        oh = lane == sel
        vals.append(mx)
        picks.append(sel)
        onehots.append(oh)
        work = jnp.where(oh, NEG_BIG * 2, work)
    exps = [jnp.exp(vv - vals[0]) for vv in vals]
    denom = exps[0] + exps[1] + exps[2] + exps[3]

    chosen = jnp.zeros(logits.shape, jnp.float32)
    for oh in onehots:
        chosen = chosen + oh.astype(jnp.float32)
    r_i = lax.broadcasted_iota(jnp.int32, (tm, tm), 0)
    c_i = lax.broadcasted_iota(jnp.int32, (tm, tm), 1)
    ltri = (c_i < r_i).astype(jnp.bfloat16)
    before = _f32dot(ltri, chosen.astype(jnp.bfloat16))
    tile_counts = jnp.sum(chosen, axis=0, keepdims=True)
    tbase_ref[0] = carry_scr[...]
    tcnt_ref[0] = tile_counts
    carry_scr[...] = carry_scr[...] + tile_counts
    cnt_ref[...] = carry_scr[...]

    idx_out = jnp.zeros(logits.shape, jnp.float32)
    w_out = jnp.zeros(logits.shape, jnp.float32)
    rank_out = jnp.zeros(logits.shape, jnp.float32)
    for kk in range(TOP_K):
        oh = onehots[kk]
        r_k = jnp.sum(jnp.where(oh, before, 0.0), axis=-1, keepdims=True)
        idx_out = jnp.where(lane == kk, picks[kk].astype(jnp.float32), idx_out)
        rank_out = jnp.where(lane == kk, r_k, rank_out)
        w_out = jnp.where(lane == kk, exps[kk] / denom, w_out)
    idx_ref[0] = idx_out.T[:SUBLANES]
    w_ref[0] = w_out.T[:SUBLANES]
    rank_ref[0] = rank_out.T[:SUBLANES]


def _route_call(x2, oa, ob, sg, wbm, wbd, wout, gffn, wrh, wrl, br):
    n = x2.shape[0]
    tm = min(ROUTE_TM, n)
    nt = n // tm
    row = lambda w: pl.BlockSpec((tm, w), lambda i: (i, 0))
    per_tile = pl.BlockSpec((1, 1, LANES), lambda i: (i, 0, 0))
    per_choice = pl.BlockSpec((1, SUBLANES, tm), lambda i: (i, 0, 0))
    out_shape = (
        jax.ShapeDtypeStruct((n, D_MODEL), jnp.float32),
        jax.ShapeDtypeStruct((n * TOK_ROWS, LANES), jnp.float32),
        jax.ShapeDtypeStruct((nt, SUBLANES, tm), jnp.float32),
        jax.ShapeDtypeStruct((nt, SUBLANES, tm), jnp.float32),
        jax.ShapeDtypeStruct((nt, SUBLANES, tm), jnp.float32),
        jax.ShapeDtypeStruct((nt, 1, LANES), jnp.float32),
        jax.ShapeDtypeStruct((nt, 1, LANES), jnp.float32),
        jax.ShapeDtypeStruct((1, LANES), jnp.float32),
    )
    return pl.pallas_call(
        _route_kernel,
        grid=(nt,),
        in_specs=[row(D_MODEL), row(MLA_WIDTH), row(DIFF_WIDTH), row(2 * D_MODEL),
                  _const_spec(wbm.shape), _const_spec(wbd.shape), _const_spec(wout.shape), _const_spec(gffn.shape),
                  _const_spec(wrh.shape), _const_spec(wrl.shape), _const_spec(br.shape)],
        out_specs=(row(D_MODEL), pl.BlockSpec((tm * TOK_ROWS, LANES), lambda i: (i, 0)), per_choice, per_choice,
                   per_choice, per_tile, per_tile, _const_spec((1, LANES))),
        out_shape=out_shape,
        scratch_shapes=[pltpu.VMEM((1, LANES), jnp.float32)],
        compiler_params=pltpu.CompilerParams(dimension_semantics=("arbitrary",), vmem_limit_bytes=V7X_VMEM_LIMIT),
        name="route",
    )(x2, oa, ob, sg, wbm, wbd, wout, gffn, wrh, wrl, br)


def _dispatch_kernel(seg_ref, lp_ref, pad_ref, hn_ref, xs_ref, scr, zscr, sems, zsem):
    step = pl.program_id(0)
    last = pl.num_programs(0) - 1
    tm = hn_ref.shape[0] // TOK_ROWS
    par = step % 2
    buf = scr.at[par]

    def place(t, carry):
        v = _tok(hn_ref, t)[...]
        for kk in range(TOP_K):
            _tok_at(buf, lp_ref[0, 0, t * TOP_K + kk])[...] = v
        return carry

    lax.fori_loop(0, tm, place, 0, unroll=TOK_UNROLL)

    def run(e, carry):
        _copy_token_run(buf, xs_ref, seg_ref[0, 0, N_EXPERTS + e], seg_ref[0, 0, 2 * N_EXPERTS + e],
                        seg_ref[0, 0, e], sems.at[par], tm)
        return carry

    lax.fori_loop(0, N_EXPERTS, run, 0)

    def drain(p):
        pltpu.make_async_copy(scr.at[p], xs_ref.at[pl.ds(0, scr.shape[1])], sems.at[p]).wait()

    @pl.when(step == 0)
    def _():
        zscr[...] = jnp.zeros(zscr.shape, zscr.dtype)

        def pad_run(e, carry):
            _copy_token_run(zscr, xs_ref, 0, pad_ref[e], pad_ref[N_EXPERTS + e], zsem, FFN_TM // 2)
            return carry

        lax.fori_loop(0, N_EXPERTS, pad_run, 0)

        def pad_tile(j, carry):
            d = pl.multiple_of((pad_ref[2 * N_EXPERTS] + j * FFN_TM) * TOK_ROWS, TOK_ROWS)
            pltpu.make_async_copy(zscr, xs_ref.at[pl.ds(d, FFN_TM * TOK_ROWS)], zsem).start()
            return carry

        lax.fori_loop(0, pad_ref[2 * N_EXPERTS + 1], pad_tile, 0)

    @pl.when(step > 0)
    def _():
        drain(1 - par)

    @pl.when(step == last)
    def _():
        drain(par)
        for _e in range(N_EXPERTS):
            pltpu.make_async_copy(zscr, xs_ref.at[pl.ds(0, FFN_TM * TOK_ROWS)], zsem).wait()


def _dispatch_call(seg3, lp3, pad, hn, n_rows):
    nt = seg3.shape[0]
    tm = hn.shape[0] // TOK_ROWS // nt
    smem = lambda a: pl.BlockSpec((1, 1, a.shape[2]), lambda i: (i, 0, 0), memory_space=pltpu.SMEM)
    return pl.pallas_call(
        _dispatch_kernel,
        grid=(nt,),
        in_specs=[
            smem(seg3), smem(lp3),
            pl.BlockSpec(memory_space=pltpu.SMEM),
            pl.BlockSpec((tm * TOK_ROWS, LANES), lambda i: (i, 0)),
        ],
        out_specs=pl.BlockSpec(memory_space=pl.ANY),
        out_shape=jax.ShapeDtypeStruct((n_rows * TOK_ROWS, LANES), jnp.float32),
        scratch_shapes=[pltpu.VMEM((2, TOP_K * tm * TOK_ROWS, LANES), jnp.float32),
                        pltpu.VMEM((FFN_TM * TOK_ROWS, LANES), jnp.float32),
                        pltpu.SemaphoreType.DMA((2,)), pltpu.SemaphoreType.DMA(())],
        compiler_params=pltpu.CompilerParams(dimension_semantics=("arbitrary",), vmem_limit_bytes=V7X_VMEM_LIMIT),
        name="dispatch",
    )(seg3, lp3, pad, hn)


def _ffn_kernel(te_ref, tv_ref, nt_ref, xs_ref, w1_ref, b1_ref, w2_ref, b2_ref, y_ref, w1b, w2b):
    i = pl.program_id(0)
    tm = xs_ref.shape[0] // TOK_ROWS

    @pl.when((i == 0) | (te_ref[i] != te_ref[jnp.maximum(i - 1, 0)]))
    def _():
        w1b[...] = w1_ref[0].astype(jnp.bfloat16)
        w2b[...] = w2_ref[0].astype(jnp.bfloat16)

    @pl.when(i < nt_ref[0])
    def _():
        rows = lax.broadcasted_iota(jnp.int32, (tm, 1), 0)
        xt = jnp.where(rows < tv_ref[i], _from_token_tiles(xs_ref, tm), 0.0).astype(jnp.bfloat16)
        u = _f32dot(xt, w1b[...]) + b1_ref[0]
        glu = jnp.minimum(u[:, :D_EXPERT], SWIGLU_LIMIT)
        lin = jnp.clip(u[:, D_EXPERT:], -SWIGLU_LIMIT, SWIGLU_LIMIT)
        a = glu * jax.nn.sigmoid(SWIGLU_ALPHA * glu) * (lin + 1.0)
        y = _f32dot(a.astype(jnp.bfloat16), w2b[...]) + b2_ref[0]
        _to_token_tiles(y_ref, y)

    @pl.when(i >= nt_ref[0])
    def _():
        y_ref[...] = jnp.zeros(y_ref.shape, y_ref.dtype)


def _ffn_call(tile_expert, tile_valid, n_tiles, xs, w1, b1, w2, b2):
    n_rows = xs.shape[0] // TOK_ROWS
    tm = FFN_TM
    t_tiles = n_rows // tm

    def tile_map(i, te, tv, nt):
        return (jnp.minimum(i, nt[0] - 1), 0)

    def out_map(i, te, tv, nt):
        return (i, 0)

    def w_map(i, te, tv, nt):
        return (te[i], 0, 0)

    grid_spec = pltpu.PrefetchScalarGridSpec(
        num_scalar_prefetch=3,
        grid=(t_tiles,),
        in_specs=[
            pl.BlockSpec((tm * TOK_ROWS, LANES), tile_map),
            pl.BlockSpec((1, D_MODEL, 2 * D_EXPERT), w_map),
            pl.BlockSpec((1, 1, 2 * D_EXPERT), w_map),
            pl.BlockSpec((1, D_EXPERT, D_MODEL), w_map),
            pl.BlockSpec((1, 1, D_MODEL), w_map),
        ],
        out_specs=pl.BlockSpec((tm * TOK_ROWS, LANES), out_map),
        scratch_shapes=[pltpu.VMEM((D_MODEL, 2 * D_EXPERT), jnp.bfloat16), pltpu.VMEM((D_EXPERT, D_MODEL), jnp.bfloat16)],
    )
    return pl.pallas_call(
        _ffn_kernel,
        grid_spec=grid_spec,
        out_shape=jax.ShapeDtypeStruct(xs.shape, jnp.float32),
        compiler_params=pltpu.CompilerParams(dimension_semantics=("arbitrary",), vmem_limit_bytes=V7X_VMEM_LIMIT),
        name="ffn",
    )(tile_expert, tile_valid, n_tiles, xs, w1, b1, w2, b2)


def _combine_kernel(seg_ref, snext_ref, lp_ref, w_ref, h_ref, y_ref, o_ref, ybuf, tok_scr, sems):
    step = pl.program_id(0)
    last = pl.num_programs(0) - 1
    tm = h_ref.shape[0]
    par = step % 2

    def fetch(s_ref, slot):
        def run(e, carry):
            _copy_token_run(y_ref, ybuf.at[slot], s_ref[0, 0, 2 * N_EXPERTS + e], s_ref[0, 0, N_EXPERTS + e],
                            s_ref[0, 0, e], sems.at[slot], tm)
            return carry

        lax.fori_loop(0, N_EXPERTS, run, 0)

    @pl.when(step == 0)
    def _():
        fetch(seg_ref, 0)

    @pl.when(step < last)
    def _():
        fetch(snext_ref, 1 - par)

    pltpu.make_async_copy(y_ref.at[pl.ds(0, ybuf.shape[1])], ybuf.at[par], sems.at[par]).wait()
    buf = ybuf.at[par]

    def token(t, carry):
        acc = None
        for kk in range(TOP_K):
            term = w_ref[0, 0, t * TOP_K + kk] * _tok_at(buf, lp_ref[0, 0, t * TOP_K + kk])[...]
            acc = term if acc is None else acc + term
        _tok(tok_scr, t)[...] = acc
        return carry

    lax.fori_loop(0, tm, token, 0, unroll=TOK_UNROLL)
    o_ref[...] = h_ref[...] + _from_token_tiles(tok_scr, tm)


def _combine_call(seg3, lp3, w3, h, y):
    n = h.shape[0]
    nt = seg3.shape[0]
    tm = n // nt
    smem = lambda a, f: pl.BlockSpec((1, 1, a.shape[2]), f, memory_space=pltpu.SMEM)
    here = lambda i: (i, 0, 0)
    return pl.pallas_call(
        _combine_kernel,
        grid=(nt,),
        in_specs=[
            smem(seg3, here), smem(seg3, lambda i: (jnp.minimum(i + 1, nt - 1), 0, 0)), smem(lp3, here), smem(w3, here),
            pl.BlockSpec((tm, D_MODEL), lambda i: (i, 0)),
            pl.BlockSpec(memory_space=pl.ANY),
        ],
        out_specs=pl.BlockSpec((tm, D_MODEL), lambda i: (i, 0)),
        out_shape=jax.ShapeDtypeStruct((n, D_MODEL), jnp.float32),
        scratch_shapes=[pltpu.VMEM((2, TOP_K * tm * TOK_ROWS, LANES), jnp.float32),
                        pltpu.VMEM((tm * TOK_ROWS, LANES), jnp.float32), pltpu.SemaphoreType.DMA((2,))],
        compiler_params=pltpu.CompilerParams(dimension_semantics=("arbitrary",), vmem_limit_bytes=V7X_VMEM_LIMIT),
        name="combine",
    )(seg3, seg3, lp3, w3, h, y)


def _rot_cols(w, group):
    lead = w.shape[:-1]
    g = w.reshape(lead + (-1, group))
    half = group // 2
    return jnp.concatenate([-g[..., half:], g[..., :half]], axis=-1).reshape(w.shape)


def _rope_tables(s, dim):
    inv_freq = 1.0 / (ROPE_THETA ** (jnp.arange(0, dim, 2, dtype=jnp.float32) / dim))
    ang = jnp.arange(s, dtype=jnp.float32)[:, None] * inv_freq[None, :]
    return jnp.cos(ang), jnp.sin(ang)


def _mla_tables(s, gain, scale):
    cos, sin = _rope_tables(s, MLA_ROPE)
    cos2 = jnp.concatenate([cos, cos], axis=-1)
    sin2 = jnp.concatenate([sin, sin], axis=-1)
    g_nope, g_rope = gain[:MLA_NOPE], gain[MLA_NOPE:]
    g_perm = jnp.concatenate([g_rope[MLA_ROPE // 2:], g_rope[:MLA_ROPE // 2]])
    pad = jnp.zeros((s, LANES - MLA_QK), jnp.float32)
    c = jnp.concatenate([jnp.broadcast_to(g_nope, (s, MLA_NOPE)), cos2 * g_rope, pad], axis=-1) * scale
    sn = jnp.concatenate([jnp.zeros((s, MLA_NOPE), jnp.float32), sin2 * g_perm, pad], axis=-1) * scale
    return c, sn


def _diff_tables(s, gain, scale):
    cos, sin = _rope_tables(s, DIFF_D)
    cos2 = jnp.concatenate([cos, cos], axis=-1)
    sin2 = jnp.concatenate([sin, sin], axis=-1)
    g_perm = jnp.concatenate([gain[DIFF_D // 2:], gain[:DIFF_D // 2]])
    c = cos2 * gain * scale
    sn = sin2 * g_perm * scale
    return jnp.concatenate([c, c], axis=-1), jnp.concatenate([sn, sn], axis=-1)


def _pad_heads(w, width):
    kdim = w.shape[0]
    g = w.reshape(kdim, -1, width)
    g = jnp.pad(g, ((0, 0), (0, 0), (0, LANES - width)))
    return g.reshape(kdim, -1)


def kernel(x, attn_norm, w_in, mla_q_lat_norm, mla_kv_lat_norm, mla_w_uq, mla_w_ukv, mla_q_norm, mla_k_norm,
           diff_q_norm, diff_k_norm, diff_lambda_q1, diff_lambda_k1, diff_lambda_q2, diff_lambda_k2, diff_subln,
           w_branch_mla, w_branch_diff, w_out, ffn_norm, w_router, b_router, w_mlp1, b_mlp1, w_mlp2, b_mlp2):
    b, s, d = x.shape
    n = b * s
    bf = jnp.bfloat16
    f32 = jnp.float32
    i = 0

    wi = w_in[i]
    c0 = MLA_Q_LORA + MLA_KV_LORA
    w_kpe = wi[:, c0:c0 + MLA_ROPE]
    c1 = c0 + MLA_ROPE
    w_dq, w_dk, w_dv = (wi[:, c1 + j * DIFF_WIDTH:c1 + (j + 1) * DIFF_WIDTH] for j in range(3))
    w_gate = wi[:, c1 + 3 * DIFF_WIDTH:]
    place = lambda w: jnp.pad(w, ((0, 0), (MLA_NOPE, LANES - MLA_QK)))
    w1 = jnp.concatenate([
        wi[:, :c0], place(w_kpe), place(_rot_cols(w_kpe, MLA_ROPE)),
        w_dq, _rot_cols(w_dq, DIFF_D), w_dk, _rot_cols(w_dk, DIFF_D), w_dv, w_gate], axis=1).astype(bf)

    wuq = mla_w_uq[i].reshape(MLA_Q_LORA, MLA_HEADS, MLA_QK)
    wuq_rot = jnp.concatenate(
        [jnp.zeros((MLA_Q_LORA, MLA_HEADS, MLA_NOPE), f32), _rot_cols(wuq[..., MLA_NOPE:], MLA_ROPE)], axis=-1)
    wq = jnp.concatenate([_pad_heads(wuq.reshape(MLA_Q_LORA, -1), MLA_QK),
                          _pad_heads(wuq_rot.reshape(MLA_Q_LORA, -1), MLA_QK)], axis=1).astype(bf)
    wukv = mla_w_ukv[i].reshape(MLA_KV_LORA, MLA_HEADS, MLA_NOPE + MLA_V)
    wkv = jnp.concatenate([_pad_heads(wukv[..., :MLA_NOPE].reshape(MLA_KV_LORA, -1), MLA_NOPE),
                           wukv[..., MLA_NOPE:].reshape(MLA_KV_LORA, -1)], axis=1).astype(bf)

    cq, sq = _mla_tables(s, mla_q_norm[i], LOG2E / math.sqrt(MLA_QK))
    ck, sk = _mla_tables(s, mla_k_norm[i], 1.0)
    cqd, sqd = _diff_tables(s, diff_q_norm[i], LOG2E / math.sqrt(DIFF_D))
    ckd, skd = _diff_tables(s, diff_k_norm[i], 1.0)
    tabs = jnp.stack([cq, sq, ck, sk, cqd, sqd, ckd, skd])

    glat = jnp.concatenate([mla_q_lat_norm[i], mla_kv_lat_norm[i]])[None, :]
    qa, ka, va, qd, kd, vd, sg = _prep_call(x, attn_norm[i][None, :], w1, glat, wq, wkv, tabs)

    lam = jnp.pad(jnp.stack([diff_lambda_q1[i], diff_lambda_k1[i], diff_lambda_q2[i], diff_lambda_k2[i]]),
                  ((0, 0), (0, LANES - DIFF_D)))
    subln = diff_subln[i][None, :]
    o_a = _attn_call(qa, ka, va, lam, subln, mode="mla")
    o_b = _attn_call(qd, kd, vd, lam, subln, mode="diff")

    wr = jnp.pad(w_router[i], ((0, 0), (0, LANES - N_EXPERTS)))
    wr_hi = wr.astype(bf)
    wr_lo = (wr - wr_hi.astype(f32)).astype(bf)
    br = jnp.concatenate([b_router[i], jnp.full((LANES - N_EXPERTS,), NEG_BIG, f32)])[None, :]
    h, hn, idx, gw, rank, tcnt, tbase, cnt = _route_call(
        x.reshape(n, d), o_a.reshape(n, MLA_WIDTH), o_b.reshape(n, DIFF_WIDTH), sg.reshape(n, 2 * d),
        w_branch_mla[i].astype(bf), w_branch_diff[i].astype(bf), w_out[i].astype(bf), ffn_norm[i][None, :],
        wr_hi, wr_lo, br)

    i32 = jnp.int32
    counts = cnt[0, :N_EXPERTS].astype(i32)
    tiles_per = (counts + FFN_TM - 1) // FFN_TM
    tile_end = jnp.cumsum(tiles_per)
    tile_start = tile_end - tiles_per
    n_tiles = tile_end[-1:]
    t_tiles = n * TOP_K // FFN_TM + N_EXPERTS
    tids = jnp.arange(t_tiles, dtype=i32)
    tile_expert = jnp.minimum(jnp.sum(tids[:, None] >= tile_end[None, :], axis=1), N_EXPERTS - 1).astype(i32)
    tile_valid = jnp.clip(counts[tile_expert] - (tids - tile_start[tile_expert]) * FFN_TM, 0, FFN_TM).astype(i32)

    nt = tcnt.shape[0]
    tc = tcnt[:, 0, :N_EXPERTS].astype(i32)
    tb = tbase[:, 0, :N_EXPERTS].astype(i32)
    off = jnp.cumsum(tc, axis=1) - tc
    first = (tile_start * FFN_TM)[None, :] + tb
    seg3 = jnp.concatenate([tc, off, first], axis=1).reshape(nt, 1, 3 * N_EXPERTS)
    per = n // nt * TOP_K
    by_token = lambda a: a[:, :TOP_K, :].transpose(0, 2, 1).reshape(nt, per)
    picked = by_token(idx).astype(i32)[:, :, None] == jnp.arange(N_EXPERTS, dtype=i32)
    lp3 = ((jnp.sum(jnp.where(picked, off[:, None, :], 0), axis=-1)
            + by_token(rank).astype(i32)) * TOK_ROWS).reshape(nt, 1, per)
    w3 = by_token(gw).reshape(nt, 1, per)
    pad = jnp.concatenate([tile_start * FFN_TM + counts, tiles_per * FFN_TM - counts,
                           n_tiles * FFN_TM, t_tiles - n_tiles]).astype(i32)

    xs = _dispatch_call(seg3, lp3, pad, hn, t_tiles * FFN_TM)
    y = _ffn_call(tile_expert, tile_valid, n_tiles.astype(i32), xs,
                  w_mlp1[i], b_mlp1[i][:, None, :], w_mlp2[i], b_mlp2[i][:, None, :])
    out = _combine_call(seg3, lp3, w3, h, y)
    return out.reshape(b, s, d)
```

```python
import functools
import math

import jax
import jax.numpy as jnp
from jax import lax
from jax.experimental import pallas as pl
from jax.experimental.pallas import tpu as pltpu

D_MODEL = 1024
ROPE_THETA = 10000.0
NORM_EPS = 1e-6
MLA_HEADS = 8
MLA_NOPE = 64
MLA_ROPE = 32
MLA_QK = MLA_NOPE + MLA_ROPE
MLA_V = 64
MLA_Q_LORA = 768
MLA_KV_LORA = 256
DIFF_HEADS = 4
DIFF_D = 64
DIFF_V = 128
MLA_WIDTH = MLA_HEADS * MLA_V
DIFF_WIDTH = DIFF_HEADS * DIFF_V
N_EXPERTS = 32
TOP_K = 4
D_EXPERT = 1024
SWIGLU_ALPHA = 1.702
SWIGLU_LIMIT = 7.0
LAMBDA_INIT = 0.8 - 0.6 * math.exp(-0.3 * 0)

LANES = 128
SUBLANES = 8
HEADS_PADDED = MLA_HEADS * LANES
V7X_VMEM_LIMIT = 56 * 1024 * 1024

PREP_TM = 256
ATT_TQ = 512
ROUTE_TM = 512
VT_ROWS = 144
TOK_ROWS = 8
TOK_UNROLL = 8
FFN_TM = 512

NEG_BIG = -1e30
LOG2E = math.log2(math.e)

_C_LAT = 0
_C_KPE = 1024
_C_DQ = 1280
_C_DK = 2304
_C_DV = 3328
_C_GATE = 3840
_C_END = 5888


def _f32dot(a, b):
    return jnp.dot(a, b, preferred_element_type=jnp.float32)


def _dot_nt(a, b):
    return lax.dot_general(a, b, (((1,), (1,)), ((), ())), preferred_element_type=jnp.float32)


def _rms_scale(x, width):
    return lax.rsqrt(jnp.sum(x * x, axis=-1, keepdims=True) * (1.0 / width) + NORM_EPS)


def _lane_iota(shape):
    return lax.broadcasted_iota(jnp.int32, shape, len(shape) - 1)


def _prep_kernel(x_ref, gattn_ref, w1_ref, glat_ref, wq_ref, wkv_ref, tabs_ref,
                 qa_ref, ka_ref, va_ref, qd_ref, kd_ref, vd_ref, sg_ref):
    x = x_ref[0]
    xn = x * _rms_scale(x, D_MODEL) * gattn_ref[...]
    proj = _f32dot(xn.astype(jnp.bfloat16), w1_ref[...])

    glat = glat_ref[...]
    q_lat = proj[:, 0:MLA_Q_LORA]
    kv_lat = proj[:, MLA_Q_LORA:MLA_Q_LORA + MLA_KV_LORA]
    qn = q_lat * _rms_scale(q_lat, MLA_Q_LORA) * glat[:, 0:MLA_Q_LORA]
    kvn = kv_lat * _rms_scale(kv_lat, MLA_KV_LORA) * glat[:, MLA_Q_LORA:]
    qq = _f32dot(qn.astype(jnp.bfloat16), wq_ref[...])
    kk = _f32dot(kvn.astype(jnp.bfloat16), wkv_ref[...])

    kpe = proj[:, _C_KPE:_C_KPE + LANES]
    kper = proj[:, _C_KPE + LANES:_C_KPE + 2 * LANES]
    cq, sq = tabs_ref[0], tabs_ref[1]
    ck, sk = tabs_ref[2], tabs_ref[3]
    for h in range(MLA_HEADS):
        lo, hi = h * LANES, (h + 1) * LANES
        qh = qq[:, lo:hi]
        qa_ref[0, h] = (_rms_scale(qh, MLA_QK) * (qh * cq + qq[:, HEADS_PADDED + lo:HEADS_PADDED + hi] * sq)).astype(jnp.bfloat16)
        kh = kk[:, lo:hi] + kpe
        ka_ref[0, h] = (_rms_scale(kh, MLA_QK) * (kh * ck + kper * sk)).astype(jnp.bfloat16)
    ones_rows = (lax.broadcasted_iota(jnp.int32, (VT_ROWS - LANES, x.shape[0]), 0) == 0).astype(jnp.bfloat16)
    for j in range(MLA_HEADS // 2):
        va_ref[0, j, 0, :LANES, :] = kk[:, HEADS_PADDED + j * LANES:HEADS_PADDED + (j + 1) * LANES].T.astype(jnp.bfloat16)
        va_ref[0, j, 0, LANES:, :] = ones_rows

    cqd, sqd = tabs_ref[4], tabs_ref[5]
    ckd, skd = tabs_ref[6], tabs_ref[7]
    lane = _lane_iota((x.shape[0], LANES))
    first = lane < DIFF_D

    def half_norm(v):
        sqv = v * v
        s_all = jnp.sum(sqv, axis=-1, keepdims=True)
        s_lo = jnp.sum(jnp.where(first, sqv, 0.0), axis=-1, keepdims=True)
        r_lo = lax.rsqrt(s_lo * (1.0 / DIFF_D) + NORM_EPS)
        r_hi = lax.rsqrt((s_all - s_lo) * (1.0 / DIFF_D) + NORM_EPS)
        return jnp.where(first, r_lo, r_hi)

    for h in range(DIFF_HEADS):
        lo, hi = h * LANES, (h + 1) * LANES
        dq = proj[:, _C_DQ + lo:_C_DQ + hi]
        dqr = proj[:, _C_DQ + DIFF_WIDTH + lo:_C_DQ + DIFF_WIDTH + hi]
        qv = half_norm(dq) * (dq * cqd + dqr * sqd)
        qd_ref[0, 2 * h] = jnp.where(first, qv, 0.0).astype(jnp.bfloat16)
        qd_ref[0, 2 * h + 1] = jnp.where(first, 0.0, qv).astype(jnp.bfloat16)
        dk = proj[:, _C_DK + lo:_C_DK + hi]
        dkr = proj[:, _C_DK + DIFF_WIDTH + lo:_C_DK + DIFF_WIDTH + hi]
        kd_ref[0, h] = (half_norm(dk) * (dk * ckd + dkr * skd)).astype(jnp.bfloat16)
        vd_ref[0, h, 0, :LANES, :] = proj[:, _C_DV + lo:_C_DV + hi].T.astype(jnp.bfloat16)
        vd_ref[0, h, 0, LANES:, :] = ones_rows

    sg_ref[0] = jax.nn.sigmoid(proj[:, _C_GATE:_C_END]).astype(jnp.bfloat16)


def _att_tiles(s):
    tq = min(ATT_TQ, s)
    return tq, tq


def _const_spec(shape):
    nd = len(shape)
    return pl.BlockSpec(shape, lambda *_: (0,) * nd)


def _prep_call(x, gattn, w1, glat, wq, wkv, tabs):
    b, s, _ = x.shape
    tm = min(PREP_TM, s)
    grid = (b, s // tm)
    head_out = lambda nh: pl.BlockSpec((1, nh, tm, LANES), lambda bi, ti: (bi, 0, ti, 0))
    _, tk = _att_tiles(s)
    per = tk // tm
    vt_out = lambda nh: pl.BlockSpec((1, nh, 1, VT_ROWS, tm), lambda bi, ti: (bi, 0, ti // per, 0, ti % per))
    bf = jnp.bfloat16
    out_shape = (
        jax.ShapeDtypeStruct((b, MLA_HEADS, s, LANES), bf),
        jax.ShapeDtypeStruct((b, MLA_HEADS, s, LANES), bf),
        jax.ShapeDtypeStruct((b, MLA_HEADS // 2, s // tk, VT_ROWS, tk), bf),
        jax.ShapeDtypeStruct((b, 2 * DIFF_HEADS, s, LANES), bf),
        jax.ShapeDtypeStruct((b, DIFF_HEADS, s, LANES), bf),
        jax.ShapeDtypeStruct((b, DIFF_HEADS, s // tk, VT_ROWS, tk), bf),
        jax.ShapeDtypeStruct((b, s, 2 * D_MODEL), bf),
    )
    return pl.pallas_call(
        _prep_kernel,
        grid=grid,
        in_specs=[
            pl.BlockSpec((1, tm, D_MODEL), lambda bi, ti: (bi, ti, 0)),
            _const_spec(gattn.shape),
            _const_spec(w1.shape),
            _const_spec(glat.shape),
            _const_spec(wq.shape),
            _const_spec(wkv.shape),
            pl.BlockSpec((8, tm, LANES), lambda bi, ti: (0, ti, 0)),
        ],
        out_specs=(
            head_out(MLA_HEADS), head_out(MLA_HEADS), vt_out(MLA_HEADS // 2),
            head_out(2 * DIFF_HEADS), head_out(DIFF_HEADS), vt_out(DIFF_HEADS),
            pl.BlockSpec((1, tm, 2 * D_MODEL), lambda bi, ti: (bi, ti, 0)),
        ),
        out_shape=out_shape,
        compiler_params=pltpu.CompilerParams(
            dimension_semantics=("arbitrary", "arbitrary"), vmem_limit_bytes=V7X_VMEM_LIMIT),
        name="prep",
    )(x, gattn, w1, glat, wq, wkv, tabs)


def _attn_kernel(q_ref, k_ref, vt_ref, lam_ref, subln_ref, o_ref, st_buf, mx_buf, m_scr, acc_scr, *, mode, tq):
    nq = q_ref.shape[2] // tq
    shared_k = k_ref.shape[1] == 1
    DIAG = 2

    def scores(qr, kb, slot, masked):
        q0 = pl.multiple_of(qr * tq, tq)
        k0 = pl.multiple_of(kb * tq, tq)
        for hh in range(2):
            k = k_ref[0, 0 if shared_k else hh, pl.ds(k0, tq), :]
            st = _dot_nt(k, q_ref[0, hh, pl.ds(q0, tq), :])
            if masked:
                key = lax.broadcasted_iota(jnp.int32, st.shape, 0)
                qry = lax.broadcasted_iota(jnp.int32, st.shape, 1)
                st = jnp.where(key <= qry, st, NEG_BIG)
            st_buf[slot, hh] = st
            mx_buf[slot, hh] = jnp.max(st, axis=0, keepdims=True)

    def softmax_pv(kb, slot):
        vt = vt_ref[0, 0, kb]
        for hh in range(2):
            m_prev = m_scr[hh]
            m_new = jnp.maximum(m_prev, mx_buf[slot, hh])
            alpha = jnp.exp2(m_prev - m_new)
            p = jnp.exp2(st_buf[slot, hh] - m_new)
            acc_scr[hh] = alpha * acc_scr[hh] + _f32dot(vt, p.astype(jnp.bfloat16))
            m_scr[hh] = m_new

    def reset():
        m_scr[...] = jnp.full(m_scr.shape, NEG_BIG, jnp.float32)
        acc_scr[...] = jnp.zeros(acc_scr.shape, jnp.float32)

    def finalize(qr):
        oa = acc_scr[0, :LANES] / acc_scr[0, LANES:LANES + 1]
        ob = acc_scr[1, :LANES] / acc_scr[1, LANES:LANES + 1]
        rows = pl.ds(pl.multiple_of(qr * tq, tq), tq)
        if mode == "mla":
            sub = lax.broadcasted_iota(jnp.int32, oa.shape, 0)
            o_ref[0, rows, :] = jnp.where(sub < MLA_V, oa, ob).T.astype(o_ref.dtype)
        else:
            lp = lam_ref[...]
            lam = (jnp.exp(jnp.sum(lp[0:1] * lp[1:2], axis=-1, keepdims=True))
                   - jnp.exp(jnp.sum(lp[2:3] * lp[3:4], axis=-1, keepdims=True)) + LAMBDA_INIT)
            o = oa - lam * ob
            o = o * lax.rsqrt(jnp.sum(o * o, axis=0, keepdims=True) * (1.0 / DIFF_V) + NORM_EPS)
            o_ref[0, rows, :] = (o.T * subln_ref[...] * (1.0 - LAMBDA_INIT)).astype(o_ref.dtype)

    def next_diag(qr):
        nxt = jnp.minimum(qr + 1, nq - 1)
        scores(nxt, nxt, DIAG, True)

    reset()
    scores(0, 0, DIAG, True)
    softmax_pv(0, DIAG)
    finalize(0)
    if nq > 1:
        scores(1, 1, DIAG, True)

    def row(qr, carry):
        reset()
        scores(qr, 0, 0, False)
        softmax_pv(qr, DIAG)
        rest = qr - 1

        def pair(p):
            scores(qr, 2 * p + 1, 1, False)
            softmax_pv(2 * p, 0)
            scores(qr, 2 * p + 2, 0, False)
            softmax_pv(2 * p + 1, 1)

        def octet(j, c):
            for u in range(4):
                pair(4 * j + u)
            return c

        lax.fori_loop(0, rest // 8, octet, 0)

        @pl.when(rest % 8 >= 4)
        def _():
            pair(rest // 8 * 4)
            pair(rest // 8 * 4 + 1)

        @pl.when(rest % 4 >= 2)
        def _():
            pair(rest // 4 * 2)

        @pl.when(rest % 2 == 0)
        def _():
            softmax_pv(qr - 1, 0)
            next_diag(qr)

        @pl.when(rest % 2 == 1)
        def _():
            scores(qr, qr - 1, 1, False)
            softmax_pv(qr - 2, 0)
            softmax_pv(qr - 1, 1)
            next_diag(qr)

        finalize(qr)
        return carry

    lax.fori_loop(1, nq, row, 0)


def _attn_call(q, k, vt, lam, subln, *, mode):
    b, _, s, _ = q.shape
    n_groups = vt.shape[1]
    kh = k.shape[1] // n_groups
    tq, tk = _att_tiles(s)
    assert tk == tq
    kernel = functools.partial(_attn_kernel, mode=mode, tq=tq)
    return pl.pallas_call(
        kernel,
        grid=(b, n_groups),
        in_specs=[
            pl.BlockSpec((1, 2, s, LANES), lambda bi, g: (bi, g, 0, 0)),
            pl.BlockSpec((1, kh, s, LANES), lambda bi, g: (bi, g, 0, 0)),
            pl.BlockSpec((1, 1, s // tk, VT_ROWS, tk), lambda bi, g: (bi, g, 0, 0, 0)),
            _const_spec(lam.shape),
            _const_spec(subln.shape),
        ],
        out_specs=pl.BlockSpec((1, s, LANES), lambda bi, g: (bi, 0, g)),
        out_shape=jax.ShapeDtypeStruct((b, s, n_groups * LANES), jnp.bfloat16),
        scratch_shapes=[
            pltpu.VMEM((3, 2, tq, tq), jnp.float32),
            pltpu.VMEM((3, 2, 1, tq), jnp.float32),
            pltpu.VMEM((2, 1, tq), jnp.float32),
            pltpu.VMEM((2, VT_ROWS, tq), jnp.float32),
        ],
        compiler_params=pltpu.CompilerParams(
            dimension_semantics=("arbitrary", "arbitrary"), vmem_limit_bytes=V7X_VMEM_LIMIT),
        name="attn_" + mode,
    )(q, k, vt, lam, subln)


def _to_token_tiles(ref, mat):
    n = mat.shape[0]
    for j in range(TOK_ROWS):
        ref[pl.ds(j, n, stride=TOK_ROWS), :] = mat[:, j * LANES:(j + 1) * LANES]


def _from_token_tiles(ref, n):
    return jnp.concatenate([ref[pl.ds(j, n, stride=TOK_ROWS), :] for j in range(TOK_ROWS)], axis=1)


def _tok_at(ref, row):
    return ref.at[pl.ds(pl.multiple_of(row, TOK_ROWS), TOK_ROWS)]


def _tok(ref, t):
    return _tok_at(ref, t * TOK_ROWS)


def _copy_token_run(src, dst, src_tok, dst_tok, count, sem, top_bit):
    bit = top_bit
    while bit >= 1:
        done = count & ~(2 * bit - 1)

        @pl.when((count & bit) != 0)
        def _(bit=bit, done=done):
            s = pl.multiple_of((src_tok + done) * TOK_ROWS, TOK_ROWS)
            d = pl.multiple_of((dst_tok + done) * TOK_ROWS, TOK_ROWS)
            pltpu.make_async_copy(src.at[pl.ds(s, bit * TOK_ROWS)], dst.at[pl.ds(d, bit * TOK_ROWS)], sem).start()

        bit //= 2


def _route_kernel(x_ref, oa_ref, ob_ref, sg_ref, wbm_ref, wbd_ref, wout_ref, gffn_ref, wrh_ref, wrl_ref, br_ref,
                  h_ref, hn_ref, idx_ref, w_ref, rank_ref, tcnt_ref, tbase_ref, cnt_ref, carry_scr):
    @pl.when(pl.program_id(0) == 0)
    def _():
        carry_scr[...] = jnp.zeros(carry_scr.shape, jnp.float32)

    sg = sg_ref[...]
    ma = _f32dot(oa_ref[...], wbm_ref[...])
    mb = _f32dot(ob_ref[...], wbd_ref[...])
    merged = sg[:, :D_MODEL] * ma + sg[:, D_MODEL:] * mb
    h = x_ref[...] + _f32dot(merged.astype(jnp.bfloat16), wout_ref[...])
    h_ref[...] = h

    hn = h * _rms_scale(h, D_MODEL) * gffn_ref[...]
    _to_token_tiles(hn_ref, hn)
    hn_hi = hn.astype(jnp.bfloat16)
    hn_lo = (hn - hn_hi.astype(jnp.float32)).astype(jnp.bfloat16)
    wrh = wrh_ref[...]
    logits = _f32dot(hn_hi, wrh) + _f32dot(hn_lo, wrh) + _f32dot(hn_hi, wrl_ref[...]) + br_ref[...]

    tm = logits.shape[0]
    lane = _lane_iota(logits.shape)
    work = logits
    vals, picks, onehots = [], [], []
    for _k in range(TOP_K):
        mx = jnp.max(work, axis=-1, keepdims=True)
        sel = jnp.min(jnp.where(work == mx, lane, LANES), axis=-1, keepdims=True)
        oh = lane == sel
        vals.append(mx)
        picks.append(sel)
        onehots.append(oh)
        work = jnp.where(oh, NEG_BIG * 2, work)
    exps = [jnp.exp(vv - vals[0]) for vv in vals]
    denom = exps[0] + exps[1] + exps[2] + exps[3]

    chosen = jnp.zeros(logits.shape, jnp.float32)
    for oh in onehots:
        chosen = chosen + oh.astype(jnp.float32)
    r_i = lax.broadcasted_iota(jnp.int32, (tm, tm), 0)
    c_i = lax.broadcasted_iota(jnp.int32, (tm, tm), 1)
    ltri = (c_i < r_i).astype(jnp.bfloat16)
    before = _f32dot(ltri, chosen.astype(jnp.bfloat16))
    tile_counts = jnp.sum(chosen, axis=0, keepdims=True)
    tbase_ref[0] = carry_scr[...]
    tcnt_ref[0] = tile_counts
    carry_scr[...] = carry_scr[...] + tile_counts
    cnt_ref[...] = carry_scr[...]

    idx_out = jnp.zeros(logits.shape, jnp.float32)
    w_out = jnp.zeros(logits.shape, jnp.float32)
    rank_out = jnp.zeros(logits.shape, jnp.float32)
    for kk in range(TOP_K):
        oh = onehots[kk]
        r_k = jnp.sum(jnp.where(oh, before, 0.0), axis=-1, keepdims=True)
        idx_out = jnp.where(lane == kk, picks[kk].astype(jnp.float32), idx_out)
        rank_out = jnp.where(lane == kk, r_k, rank_out)
        w_out = jnp.where(lane == kk, exps[kk] / denom, w_out)
    idx_ref[0] = idx_out.T[:SUBLANES]
    w_ref[0] = w_out.T[:SUBLANES]
    rank_ref[0] = rank_out.T[:SUBLANES]


def _route_call(x2, oa, ob, sg, wbm, wbd, wout, gffn, wrh, wrl, br):
    n = x2.shape[0]
    tm = min(ROUTE_TM, n)
    nt = n // tm
    row = lambda w: pl.BlockSpec((tm, w), lambda i: (i, 0))
    per_tile = pl.BlockSpec((1, 1, LANES), lambda i: (i, 0, 0))
    per_choice = pl.BlockSpec((1, SUBLANES, tm), lambda i: (i, 0, 0))
    out_shape = (
        jax.ShapeDtypeStruct((n, D_MODEL), jnp.float32),
        jax.ShapeDtypeStruct((n * TOK_ROWS, LANES), jnp.float32),
        jax.ShapeDtypeStruct((nt, SUBLANES, tm), jnp.float32),
        jax.ShapeDtypeStruct((nt, SUBLANES, tm), jnp.float32),
        jax.ShapeDtypeStruct((nt, SUBLANES, tm), jnp.float32),
        jax.ShapeDtypeStruct((nt, 1, LANES), jnp.float32),
        jax.ShapeDtypeStruct((nt, 1, LANES), jnp.float32),
        jax.ShapeDtypeStruct((1, LANES), jnp.float32),
    )
    return pl.pallas_call(
        _route_kernel,
        grid=(nt,),
        in_specs=[row(D_MODEL), row(MLA_WIDTH), row(DIFF_WIDTH), row(2 * D_MODEL),
                  _const_spec(wbm.shape), _const_spec(wbd.shape), _const_spec(wout.shape), _const_spec(gffn.shape),
                  _const_spec(wrh.shape), _const_spec(wrl.shape), _const_spec(br.shape)],
        out_specs=(row(D_MODEL), pl.BlockSpec((tm * TOK_ROWS, LANES), lambda i: (i, 0)), per_choice, per_choice,
                   per_choice, per_tile, per_tile, _const_spec((1, LANES))),
        out_shape=out_shape,
        scratch_shapes=[pltpu.VMEM((1, LANES), jnp.float32)],
        compiler_params=pltpu.CompilerParams(dimension_semantics=("arbitrary",), vmem_limit_bytes=V7X_VMEM_LIMIT),
        name="route",
    )(x2, oa, ob, sg, wbm, wbd, wout, gffn, wrh, wrl, br)


def _dispatch_kernel(seg_ref, lp_ref, pad_ref, hn_ref, xs_ref, scr, zscr, sems, zsem):
    step = pl.program_id(0)
    last = pl.num_programs(0) - 1
    tm = hn_ref.shape[0] // TOK_ROWS
    par = step % 2
    buf = scr.at[par]

    def place(t, carry):
        v = _tok(hn_ref, t)[...]
        for kk in range(TOP_K):
            _tok_at(buf, lp_ref[0, 0, t * TOP_K + kk])[...] = v
        return carry

    lax.fori_loop(0, tm, place, 0, unroll=TOK_UNROLL)

    def run(e, carry):
        _copy_token_run(buf, xs_ref, seg_ref[0, 0, N_EXPERTS + e], seg_ref[0, 0, 2 * N_EXPERTS + e],
                        seg_ref[0, 0, e], sems.at[par], tm)
        return carry

    lax.fori_loop(0, N_EXPERTS, run, 0)

    def drain(p):
        pltpu.make_async_copy(scr.at[p], xs_ref.at[pl.ds(0, scr.shape[1])], sems.at[p]).wait()

    @pl.when(step == 0)
    def _():
        zscr[...] = jnp.zeros(zscr.shape, zscr.dtype)

        def pad_run(e, carry):
            _copy_token_run(zscr, xs_ref, 0, pad_ref[e], pad_ref[N_EXPERTS + e], zsem, FFN_TM // 2)
            return carry

        lax.fori_loop(0, N_EXPERTS, pad_run, 0)

        def pad_tile(j, carry):
            d = pl.multiple_of((pad_ref[2 * N_EXPERTS] + j * FFN_TM) * TOK_ROWS, TOK_ROWS)
            pltpu.make_async_copy(zscr, xs_ref.at[pl.ds(d, FFN_TM * TOK_ROWS)], zsem).start()
            return carry

        lax.fori_loop(0, pad_ref[2 * N_EXPERTS + 1], pad_tile, 0)

    @pl.when(step > 0)
    def _():
        drain(1 - par)

    @pl.when(step == last)
    def _():
        drain(par)
        for _e in range(N_EXPERTS):
            pltpu.make_async_copy(zscr, xs_ref.at[pl.ds(0, FFN_TM * TOK_ROWS)], zsem).wait()


def _dispatch_call(seg3, lp3, pad, hn, n_rows):
    nt = seg3.shape[0]
    tm = hn.shape[0] // TOK_ROWS // nt
    smem = lambda a: pl.BlockSpec((1, 1, a.shape[2]), lambda i: (i, 0, 0), memory_space=pltpu.SMEM)
    return pl.pallas_call(
        _dispatch_kernel,
        grid=(nt,),
        in_specs=[
            smem(seg3), smem(lp3),
            pl.BlockSpec(memory_space=pltpu.SMEM),
            pl.BlockSpec((tm * TOK_ROWS, LANES), lambda i: (i, 0)),
        ],
        out_specs=pl.BlockSpec(memory_space=pl.ANY),
        out_shape=jax.ShapeDtypeStruct((n_rows * TOK_ROWS, LANES), jnp.float32),
        scratch_shapes=[pltpu.VMEM((2, TOP_K * tm * TOK_ROWS, LANES), jnp.float32),
                        pltpu.VMEM((FFN_TM * TOK_ROWS, LANES), jnp.float32),
                        pltpu.SemaphoreType.DMA((2,)), pltpu.SemaphoreType.DMA(())],
        compiler_params=pltpu.CompilerParams(dimension_semantics=("arbitrary",), vmem_limit_bytes=V7X_VMEM_LIMIT),
        name="dispatch",
    )(seg3, lp3, pad, hn)


def _ffn_kernel(te_ref, tv_ref, nt_ref, xs_ref, w1_ref, b1_ref, w2_ref, b2_ref, y_ref, w1b, w2b):
    i = pl.program_id(0)
    tm = xs_ref.shape[0] // TOK_ROWS

    @pl.when((i == 0) | (te_ref[i] != te_ref[jnp.maximum(i - 1, 0)]))
    def _():
        w1b[...] = w1_ref[0].astype(jnp.bfloat16)
        w2b[...] = w2_ref[0].astype(jnp.bfloat16)

    @pl.when(i < nt_ref[0])
    def _():
        rows = lax.broadcasted_iota(jnp.int32, (tm, 1), 0)
        xt = jnp.where(rows < tv_ref[i], _from_token_tiles(xs_ref, tm), 0.0).astype(jnp.bfloat16)
        u = _f32dot(xt, w1b[...]) + b1_ref[0]
        glu = jnp.minimum(u[:, :D_EXPERT], SWIGLU_LIMIT)
        lin = jnp.clip(u[:, D_EXPERT:], -SWIGLU_LIMIT, SWIGLU_LIMIT)
        a = glu * jax.nn.sigmoid(SWIGLU_ALPHA * glu) * (lin + 1.0)
        y = _f32dot(a.astype(jnp.bfloat16), w2b[...]) + b2_ref[0]
        _to_token_tiles(y_ref, y)

    @pl.when(i >= nt_ref[0])
    def _():
        y_ref[...] = jnp.zeros(y_ref.shape, y_ref.dtype)


def _ffn_call(tile_expert, tile_valid, n_tiles, xs, w1, b1, w2, b2):
    n_rows = xs.shape[0] // TOK_ROWS
    tm = FFN_TM
    t_tiles = n_rows // tm

    def tile_map(i, te, tv, nt):
        return (jnp.minimum(i, nt[0] - 1), 0)

    def out_map(i, te, tv, nt):
        return (i, 0)

    def w_map(i, te, tv, nt):
        return (te[i], 0, 0)

    grid_spec = pltpu.PrefetchScalarGridSpec(
        num_scalar_prefetch=3,
        grid=(t_tiles,),
        in_specs=[
            pl.BlockSpec((tm * TOK_ROWS, LANES), tile_map),
            pl.BlockSpec((1, D_MODEL, 2 * D_EXPERT), w_map),
            pl.BlockSpec((1, 1, 2 * D_EXPERT), w_map),
            pl.BlockSpec((1, D_EXPERT, D_MODEL), w_map),
            pl.BlockSpec((1, 1, D_MODEL), w_map),
        ],
        out_specs=pl.BlockSpec((tm * TOK_ROWS, LANES), out_map),
        scratch_shapes=[pltpu.VMEM((D_MODEL, 2 * D_EXPERT), jnp.bfloat16), pltpu.VMEM((D_EXPERT, D_MODEL), jnp.bfloat16)],
    )
    return pl.pallas_call(
        _ffn_kernel,
        grid_spec=grid_spec,
        out_shape=jax.ShapeDtypeStruct(xs.shape, jnp.float32),
        compiler_params=pltpu.CompilerParams(dimension_semantics=("arbitrary",), vmem_limit_bytes=V7X_VMEM_LIMIT),
        name="ffn",
    )(tile_expert, tile_valid, n_tiles, xs, w1, b1, w2, b2)


def _combine_kernel(seg_ref, snext_ref, lp_ref, w_ref, h_ref, y_ref, o_ref, ybuf, tok_scr, sems):
    step = pl.program_id(0)
    last = pl.num_programs(0) - 1
    tm = h_ref.shape[0]
    par = step % 2

    def fetch(s_ref, slot):
        def run(e, carry):
            _copy_token_run(y_ref, ybuf.at[slot], s_ref[0, 0, 2 * N_EXPERTS + e], s_ref[0, 0, N_EXPERTS + e],
                            s_ref[0, 0, e], sems.at[slot], tm)
            return carry

        lax.fori_loop(0, N_EXPERTS, run, 0)

    @pl.when(step == 0)
    def _():
        fetch(seg_ref, 0)

    @pl.when(step < last)
    def _():
        fetch(snext_ref, 1 - par)

    pltpu.make_async_copy(y_ref.at[pl.ds(0, ybuf.shape[1])], ybuf.at[par], sems.at[par]).wait()
    buf = ybuf.at[par]

    def token(t, carry):
        acc = None
        for kk in range(TOP_K):
            term = w_ref[0, 0, t * TOP_K + kk] * _tok_at(buf, lp_ref[0, 0, t * TOP_K + kk])[...]
            acc = term if acc is None else acc + term
        _tok(tok_scr, t)[...] = acc
        return carry

    lax.fori_loop(0, tm, token, 0, unroll=TOK_UNROLL)
    o_ref[...] = h_ref[...] + _from_token_tiles(tok_scr, tm)


def _combine_call(seg3, lp3, w3, h, y):
    n = h.shape[0]
    nt = seg3.shape[0]
    tm = n // nt
    smem = lambda a, f: pl.BlockSpec((1, 1, a.shape[2]), f, memory_space=pltpu.SMEM)
    here = lambda i: (i, 0, 0)
    return pl.pallas_call(
        _combine_kernel,
        grid=(nt,),
        in_specs=[
            smem(seg3, here), smem(seg3, lambda i: (jnp.minimum(i + 1, nt - 1), 0, 0)), smem(lp3, here), smem(w3, here),
            pl.BlockSpec((tm, D_MODEL), lambda i: (i, 0)),
            pl.BlockSpec(memory_space=pl.ANY),
        ],
        out_specs=pl.BlockSpec((tm, D_MODEL), lambda i: (i, 0)),
        out_shape=jax.ShapeDtypeStruct((n, D_MODEL), jnp.float32),
        scratch_shapes=[pltpu.VMEM((2, TOP_K * tm * TOK_ROWS, LANES), jnp.float32),
                        pltpu.VMEM((tm * TOK_ROWS, LANES), jnp.float32), pltpu.SemaphoreType.DMA((2,))],
        compiler_params=pltpu.CompilerParams(dimension_semantics=("arbitrary",), vmem_limit_bytes=V7X_VMEM_LIMIT),
        name="combine",
    )(seg3, seg3, lp3, w3, h, y)


def _rot_cols(w, group):
    lead = w.shape[:-1]
    g = w.reshape(lead + (-1, group))
    half = group // 2
    return jnp.concatenate([-g[..., half:], g[..., :half]], axis=-1).reshape(w.shape)


def _rope_tables(s, dim):
    inv_freq = 1.0 / (ROPE_THETA ** (jnp.arange(0, dim, 2, dtype=jnp.float32) / dim))
    ang = jnp.arange(s, dtype=jnp.float32)[:, None] * inv_freq[None, :]
    return jnp.cos(ang), jnp.sin(ang)


def _mla_tables(s, gain, scale):
    cos, sin = _rope_tables(s, MLA_ROPE)
    cos2 = jnp.concatenate([cos, cos], axis=-1)
    sin2 = jnp.concatenate([sin, sin], axis=-1)
    g_nope, g_rope = gain[:MLA_NOPE], gain[MLA_NOPE:]
    g_perm = jnp.concatenate([g_rope[MLA_ROPE // 2:], g_rope[:MLA_ROPE // 2]])
    pad = jnp.zeros((s, LANES - MLA_QK), jnp.float32)
    c = jnp.concatenate([jnp.broadcast_to(g_nope, (s, MLA_NOPE)), cos2 * g_rope, pad], axis=-1) * scale
    sn = jnp.concatenate([jnp.zeros((s, MLA_NOPE), jnp.float32), sin2 * g_perm, pad], axis=-1) * scale
    return c, sn


def _diff_tables(s, gain, scale):
    cos, sin = _rope_tables(s, DIFF_D)
    cos2 = jnp.concatenate([cos, cos], axis=-1)
    sin2 = jnp.concatenate([sin, sin], axis=-1)
    g_perm = jnp.concatenate([gain[DIFF_D // 2:], gain[:DIFF_D // 2]])
    c = cos2 * gain * scale
    sn = sin2 * g_perm * scale
    return jnp.concatenate([c, c], axis=-1), jnp.concatenate([sn, sn], axis=-1)


def _pad_heads(w, width):
    kdim = w.shape[0]
    g = w.reshape(kdim, -1, width)
    g = jnp.pad(g, ((0, 0), (0, 0), (0, LANES - width)))
    return g.reshape(kdim, -1)


def kernel(x, attn_norm, w_in, mla_q_lat_norm, mla_kv_lat_norm, mla_w_uq, mla_w_ukv, mla_q_norm, mla_k_norm,
           diff_q_norm, diff_k_norm, diff_lambda_q1, diff_lambda_k1, diff_lambda_q2, diff_lambda_k2, diff_subln,
           w_branch_mla, w_branch_diff, w_out, ffn_norm, w_router, b_router, w_mlp1, b_mlp1, w_mlp2, b_mlp2):
    b, s, d = x.shape
    n = b * s
    bf = jnp.bfloat16
    f32 = jnp.float32
    i = 0

    wi = w_in[i]
    c0 = MLA_Q_LORA + MLA_KV_LORA
    w_kpe = wi[:, c0:c0 + MLA_ROPE]
    c1 = c0 + MLA_ROPE
    w_dq, w_dk, w_dv = (wi[:, c1 + j * DIFF_WIDTH:c1 + (j + 1) * DIFF_WIDTH] for j in range(3))
    w_gate = wi[:, c1 + 3 * DIFF_WIDTH:]
    place = lambda w: jnp.pad(w, ((0, 0), (MLA_NOPE, LANES - MLA_QK)))
    w1 = jnp.concatenate([
        wi[:, :c0], place(w_kpe), place(_rot_cols(w_kpe, MLA_ROPE)),
        w_dq, _rot_cols(w_dq, DIFF_D), w_dk, _rot_cols(w_dk, DIFF_D), w_dv, w_gate], axis=1).astype(bf)

    wuq = mla_w_uq[i].reshape(MLA_Q_LORA, MLA_HEADS, MLA_QK)
    wuq_rot = jnp.concatenate(
        [jnp.zeros((MLA_Q_LORA, MLA_HEADS, MLA_NOPE), f32), _rot_cols(wuq[..., MLA_NOPE:], MLA_ROPE)], axis=-1)
    wq = jnp.concatenate([_pad_heads(wuq.reshape(MLA_Q_LORA, -1), MLA_QK),
                          _pad_heads(wuq_rot.reshape(MLA_Q_LORA, -1), MLA_QK)], axis=1).astype(bf)
    wukv = mla_w_ukv[i].reshape(MLA_KV_LORA, MLA_HEADS, MLA_NOPE + MLA_V)
    wkv = jnp.concatenate([_pad_heads(wukv[..., :MLA_NOPE].reshape(MLA_KV_LORA, -1), MLA_NOPE),
                           wukv[..., MLA_NOPE:].reshape(MLA_KV_LORA, -1)], axis=1).astype(bf)

    cq, sq = _mla_tables(s, mla_q_norm[i], LOG2E / math.sqrt(MLA_QK))
    ck, sk = _mla_tables(s, mla_k_norm[i], 1.0)
    cqd, sqd = _diff_tables(s, diff_q_norm[i], LOG2E / math.sqrt(DIFF_D))
    ckd, skd = _diff_tables(s, diff_k_norm[i], 1.0)
    tabs = jnp.stack([cq, sq, ck, sk, cqd, sqd, ckd, skd])

    glat = jnp.concatenate([mla_q_lat_norm[i], mla_kv_lat_norm[i]])[None, :]
    qa, ka, va, qd, kd, vd, sg = _prep_call(x, attn_norm[i][None, :], w1, glat, wq, wkv, tabs)

    lam = jnp.pad(jnp.stack([diff_lambda_q1[i], diff_lambda_k1[i], diff_lambda_q2[i], diff_lambda_k2[i]]),
                  ((0, 0), (0, LANES - DIFF_D)))
    subln = diff_subln[i][None, :]
    o_a = _attn_call(qa, ka, va, lam, subln, mode="mla")
    o_b = _attn_call(qd, kd, vd, lam, subln, mode="diff")

    wr = jnp.pad(w_router[i], ((0, 0), (0, LANES - N_EXPERTS)))
    wr_hi = wr.astype(bf)
    wr_lo = (wr - wr_hi.astype(f32)).astype(bf)
    br = jnp.concatenate([b_router[i], jnp.full((LANES - N_EXPERTS,), NEG_BIG, f32)])[None, :]
    h, hn, idx, gw, rank, tcnt, tbase, cnt = _route_call(
        x.reshape(n, d), o_a.reshape(n, MLA_WIDTH), o_b.reshape(n, DIFF_WIDTH), sg.reshape(n, 2 * d),
        w_branch_mla[i].astype(bf), w_branch_diff[i].astype(bf), w_out[i].astype(bf), ffn_norm[i][None, :],
        wr_hi, wr_lo, br)

    i32 = jnp.int32
    counts = cnt[0, :N_EXPERTS].astype(i32)
    tiles_per = (counts + FFN_TM - 1) // FFN_TM
    tile_end = jnp.cumsum(tiles_per)
    tile_start = tile_end - tiles_per
    n_tiles = tile_end[-1:]
    t_tiles = n * TOP_K // FFN_TM + N_EXPERTS
    tids = jnp.arange(t_tiles, dtype=i32)
    tile_expert = jnp.minimum(jnp.sum(tids[:, None] >= tile_end[None, :], axis=1), N_EXPERTS - 1).astype(i32)
    tile_valid = jnp.clip(counts[tile_expert] - (tids - tile_start[tile_expert]) * FFN_TM, 0, FFN_TM).astype(i32)

    nt = tcnt.shape[0]
    tc = tcnt[:, 0, :N_EXPERTS].astype(i32)
    tb = tbase[:, 0, :N_EXPERTS].astype(i32)
    off = jnp.cumsum(tc, axis=1) - tc
    first = (tile_start * FFN_TM)[None, :] + tb
    seg3 = jnp.concatenate([tc, off, first], axis=1).reshape(nt, 1, 3 * N_EXPERTS)
    per = n // nt * TOP_K
    by_token = lambda a: a[:, :TOP_K, :].transpose(0, 2, 1).reshape(nt, per)
    picked = by_token(idx).astype(i32)[:, :, None] == jnp.arange(N_EXPERTS, dtype=i32)
    lp3 = ((jnp.sum(jnp.where(picked, off[:, None, :], 0), axis=-1)
            + by_token(rank).astype(i32)) * TOK_ROWS).reshape(nt, 1, per)
    w3 = by_token(gw).reshape(nt, 1, per)
    pad = jnp.concatenate([tile_start * FFN_TM + counts, tiles_per * FFN_TM - counts,
                           n_tiles * FFN_TM, t_tiles - n_tiles]).astype(i32)

    xs = _dispatch_call(seg3, lp3, pad, hn, t_tiles * FFN_TM)
    y = _ffn_call(tile_expert, tile_valid, n_tiles.astype(i32), xs,
                  w_mlp1[i], b_mlp1[i][:, None, :], w_mlp2[i], b_mlp2[i][:, None, :])
    out = _combine_call(seg3, lp3, w3, h, y)
    return out.reshape(b, s, d)
```

```python
import functools
import math

import jax
import jax.numpy as jnp
from jax import lax
from jax.experimental import pallas as pl
from jax.experimental.pallas import tpu as pltpu

D_MODEL = 1024
ROPE_THETA = 10000.0
NORM_EPS = 1e-6
MLA_HEADS = 8
MLA_NOPE = 64
MLA_ROPE = 32
MLA_QK = MLA_NOPE + MLA_ROPE
MLA_V = 64
MLA_Q_LORA = 768
MLA_KV_LORA = 256
DIFF_HEADS = 4
DIFF_D = 64
DIFF_V = 128
MLA_WIDTH = MLA_HEADS * MLA_V
DIFF_WIDTH = DIFF_HEADS * DIFF_V
N_EXPERTS = 32
TOP_K = 4
D_EXPERT = 1024
SWIGLU_ALPHA = 1.702
SWIGLU_LIMIT = 7.0
LAMBDA_INIT = 0.8 - 0.6 * math.exp(-0.3 * 0)

LANES = 128
SUBLANES = 8
HEADS_PADDED = MLA_HEADS * LANES
V7X_VMEM_LIMIT = 56 * 1024 * 1024

PREP_TM = 256
ATT_TQ = 512
ROUTE_TM = 512
VT_ROWS = 144
TOK_ROWS = 8
TOK_UNROLL = 8
FFN_TM = 512

NEG_BIG = -1e30
LOG2E = math.log2(math.e)

_C_LAT = 0
_C_KPE = 1024
_C_DQ = 1280
_C_DK = 2304
_C_DV = 3328
_C_GATE = 3840
_C_END = 5888


def _f32dot(a, b):
    return jnp.dot(a, b, preferred_element_type=jnp.float32)


def _dot_nt(a, b):
    return lax.dot_general(a, b, (((1,), (1,)), ((), ())), preferred_element_type=jnp.float32)


def _rms_scale(x, width):
    return lax.rsqrt(jnp.sum(x * x, axis=-1, keepdims=True) * (1.0 / width) + NORM_EPS)


def _lane_iota(shape):
    return lax.broadcasted_iota(jnp.int32, shape, len(shape) - 1)


def _prep_kernel(x_ref, gattn_ref, w1_ref, glat_ref, wq_ref, wkv_ref, tabs_ref,
                 qa_ref, ka_ref, va_ref, qd_ref, kd_ref, vd_ref, sg_ref):
    x = x_ref[0]
    xn = x * _rms_scale(x, D_MODEL) * gattn_ref[...]
    proj = _f32dot(xn.astype(jnp.bfloat16), w1_ref[...])

    glat = glat_ref[...]
    q_lat = proj[:, 0:MLA_Q_LORA]
    kv_lat = proj[:, MLA_Q_LORA:MLA_Q_LORA + MLA_KV_LORA]
    qn = q_lat * _rms_scale(q_lat, MLA_Q_LORA) * glat[:, 0:MLA_Q_LORA]
    kvn = kv_lat * _rms_scale(kv_lat, MLA_KV_LORA) * glat[:, MLA_Q_LORA:]
    qq = _f32dot(qn.astype(jnp.bfloat16), wq_ref[...])
    kk = _f32dot(kvn.astype(jnp.bfloat16), wkv_ref[...])

    kpe = proj[:, _C_KPE:_C_KPE + LANES]
    kper = proj[:, _C_KPE + LANES:_C_KPE + 2 * LANES]
    cq, sq = tabs_ref[0], tabs_ref[1]
    ck, sk = tabs_ref[2], tabs_ref[3]
    for h in range(MLA_HEADS):
        lo, hi = h * LANES, (h + 1) * LANES
        qh = qq[:, lo:hi]
        qa_ref[0, h] = (_rms_scale(qh, MLA_QK) * (qh * cq + qq[:, HEADS_PADDED + lo:HEADS_PADDED + hi] * sq)).astype(jnp.bfloat16)
        kh = kk[:, lo:hi] + kpe
        ka_ref[0, h] = (_rms_scale(kh, MLA_QK) * (kh * ck + kper * sk)).astype(jnp.bfloat16)
    ones_rows = (lax.broadcasted_iota(jnp.int32, (VT_ROWS - LANES, x.shape[0]), 0) == 0).astype(jnp.bfloat16)
    for j in range(MLA_HEADS // 2):
        va_ref[0, j, 0, :LANES, :] = kk[:, HEADS_PADDED + j * LANES:HEADS_PADDED + (j + 1) * LANES].T.astype(jnp.bfloat16)
        va_ref[0, j, 0, LANES:, :] = ones_rows

    cqd, sqd = tabs_ref[4], tabs_ref[5]
    ckd, skd = tabs_ref[6], tabs_ref[7]
    lane = _lane_iota((x.shape[0], LANES))
    first = lane < DIFF_D

    def half_norm(v):
        sqv = v * v
        s_all = jnp.sum(sqv, axis=-1, keepdims=True)
        s_lo = jnp.sum(jnp.where(first, sqv, 0.0), axis=-1, keepdims=True)
        r_lo = lax.rsqrt(s_lo * (1.0 / DIFF_D) + NORM_EPS)
        r_hi = lax.rsqrt((s_all - s_lo) * (1.0 / DIFF_D) + NORM_EPS)
        return jnp.where(first, r_lo, r_hi)

    for h in range(DIFF_HEADS):
        lo, hi = h * LANES, (h + 1) * LANES
        dq = proj[:, _C_DQ + lo:_C_DQ + hi]
        dqr = proj[:, _C_DQ + DIFF_WIDTH + lo:_C_DQ + DIFF_WIDTH + hi]
        qv = half_norm(dq) * (dq * cqd + dqr * sqd)
        qd_ref[0, 2 * h] = jnp.where(first, qv, 0.0).astype(jnp.bfloat16)
        qd_ref[0, 2 * h + 1] = jnp.where(first, 0.0, qv).astype(jnp.bfloat16)
        dk = proj[:, _C_DK + lo:_C_DK + hi]
        dkr = proj[:, _C_DK + DIFF_WIDTH + lo:_C_DK + DIFF_WIDTH + hi]
        kd_ref[0, h] = (half_norm(dk) * (dk * ckd + dkr * skd)).astype(jnp.bfloat16)
        vd_ref[0, h, 0, :LANES, :] = proj[:, _C_DV + lo:_C_DV + hi].T.astype(jnp.bfloat16)
        vd_ref[0, h, 0, LANES:, :] = ones_rows

    sg_ref[0] = jax.nn.sigmoid(proj[:, _C_GATE:_C_END]).astype(jnp.bfloat16)


def _att_tiles(s):
    tq = min(ATT_TQ, s)
    return tq, tq


def _const_spec(shape):
    nd = len(shape)
    return pl.BlockSpec(shape, lambda *_: (0,) * nd)


def _prep_call(x, gattn, w1, glat, wq, wkv, tabs):
    b, s, _ = x.shape
    tm = min(PREP_TM, s)
    grid = (b, s // tm)
    head_out = lambda nh: pl.BlockSpec((1, nh, tm, LANES), lambda bi, ti: (bi, 0, ti, 0))
    _, tk = _att_tiles(s)
    per = tk // tm
    vt_out = lambda nh: pl.BlockSpec((1, nh, 1, VT_ROWS, tm), lambda bi, ti: (bi, 0, ti // per, 0, ti % per))
    bf = jnp.bfloat16
    out_shape = (
        jax.ShapeDtypeStruct((b, MLA_HEADS, s, LANES), bf),
        jax.ShapeDtypeStruct((b, MLA_HEADS, s, LANES), bf),
        jax.ShapeDtypeStruct((b, MLA_HEADS // 2, s // tk, VT_ROWS, tk), bf),
        jax.ShapeDtypeStruct((b, 2 * DIFF_HEADS, s, LANES), bf),
        jax.ShapeDtypeStruct((b, DIFF_HEADS, s, LANES), bf),
        jax.ShapeDtypeStruct((b, DIFF_HEADS, s // tk, VT_ROWS, tk), bf),
        jax.ShapeDtypeStruct((b, s, 2 * D_MODEL), bf),
    )
    return pl.pallas_call(
        _prep_kernel,
        grid=grid,
        in_specs=[
            pl.BlockSpec((1, tm, D_MODEL), lambda bi, ti: (bi, ti, 0)),
            _const_spec(gattn.shape),
            _const_spec(w1.shape),
            _const_spec(glat.shape),
            _const_spec(wq.shape),
            _const_spec(wkv.shape),
            pl.BlockSpec((8, tm, LANES), lambda bi, ti: (0, ti, 0)),
        ],
        out_specs=(
            head_out(MLA_HEADS), head_out(MLA_HEADS), vt_out(MLA_HEADS // 2),
            head_out(2 * DIFF_HEADS), head_out(DIFF_HEADS), vt_out(DIFF_HEADS),
            pl.BlockSpec((1, tm, 2 * D_MODEL), lambda bi, ti: (bi, ti, 0)),
        ),
        out_shape=out_shape,
        compiler_params=pltpu.CompilerParams(
            dimension_semantics=("arbitrary", "arbitrary"), vmem_limit_bytes=V7X_VMEM_LIMIT),
        name="prep",
    )(x, gattn, w1, glat, wq, wkv, tabs)


def _attn_kernel(q_ref, k_ref, vt_ref, lam_ref, subln_ref, o_ref, st_buf, mx_buf, m_scr, acc_scr, *, mode, tq):
    nq = q_ref.shape[2] // tq
    shared_k = k_ref.shape[1] == 1
    DIAG = 2

    def scores(qr, kb, slot, masked):
        q0 = pl.multiple_of(qr * tq, tq)
        k0 = pl.multiple_of(kb * tq, tq)
        for hh in range(2):
            k = k_ref[0, 0 if shared_k else hh, pl.ds(k0, tq), :]
            st = _dot_nt(k, q_ref[0, hh, pl.ds(q0, tq), :])
            if masked:
                key = lax.broadcasted_iota(jnp.int32, st.shape, 0)
                qry = lax.broadcasted_iota(jnp.int32, st.shape, 1)
                st = jnp.where(key <= qry, st, NEG_BIG)
            st_buf[slot, hh] = st
            mx_buf[slot, hh] = jnp.max(st, axis=0, keepdims=True)

    def softmax_pv(kb, slot):
        vt = vt_ref[0, 0, kb]
        for hh in range(2):
            m_prev = m_scr[hh]
            m_new = jnp.maximum(m_prev, mx_buf[slot, hh])
            alpha = jnp.exp2(m_prev - m_new)
            p = jnp.exp2(st_buf[slot, hh] - m_new)
            acc_scr[hh] = alpha * acc_scr[hh] + _f32dot(vt, p.astype(jnp.bfloat16))
            m_scr[hh] = m_new

    def reset():
        m_scr[...] = jnp.full(m_scr.shape, NEG_BIG, jnp.float32)
        acc_scr[...] = jnp.zeros(acc_scr.shape, jnp.float32)

    def finalize(qr):
        oa = acc_scr[0, :LANES] / acc_scr[0, LANES:LANES + 1]
        ob = acc_scr[1, :LANES] / acc_scr[1, LANES:LANES + 1]
        rows = pl.ds(pl.multiple_of(qr * tq, tq), tq)
        if mode == "mla":
            sub = lax.broadcasted_iota(jnp.int32, oa.shape, 0)
            o_ref[0, rows, :] = jnp.where(sub < MLA_V, oa, ob).T.astype(o_ref.dtype)
        else:
            lp = lam_ref[...]
            lam = (jnp.exp(jnp.sum(lp[0:1] * lp[1:2], axis=-1, keepdims=True))
                   - jnp.exp(jnp.sum(lp[2:3] * lp[3:4], axis=-1, keepdims=True)) + LAMBDA_INIT)
            o = oa - lam * ob
            o = o * lax.rsqrt(jnp.sum(o * o, axis=0, keepdims=True) * (1.0 / DIFF_V) + NORM_EPS)
            o_ref[0, rows, :] = (o.T * subln_ref[...] * (1.0 - LAMBDA_INIT)).astype(o_ref.dtype)

    def next_diag(qr):
        nxt = jnp.minimum(qr + 1, nq - 1)
        scores(nxt, nxt, DIAG, True)

    reset()
    scores(0, 0, DIAG, True)
    softmax_pv(0, DIAG)
    finalize(0)
    if nq > 1:
        scores(1, 1, DIAG, True)

    def row(qr, carry):
        reset()
        scores(qr, 0, 0, False)
        softmax_pv(qr, DIAG)
        rest = qr - 1

        def pair(p):
            scores(qr, 2 * p + 1, 1, False)
            softmax_pv(2 * p, 0)
            scores(qr, 2 * p + 2, 0, False)
            softmax_pv(2 * p + 1, 1)

        def octet(j, c):
            for u in range(4):
                pair(4 * j + u)
            return c

        lax.fori_loop(0, rest // 8, octet, 0)

        @pl.when(rest % 8 >= 4)
        def _():
            pair(rest // 8 * 4)
            pair(rest // 8 * 4 + 1)

        def ending(with_pair, odd):
            if with_pair:
                pair(rest // 4 * 2)
            if odd:
                scores(qr, qr - 1, 1, False)
                softmax_pv(qr - 2, 0)
                softmax_pv(qr - 1, 1)
            else:
                softmax_pv(qr - 1, 0)
            next_diag(qr)

        for case in range(4):
            pl.when(rest % 4 == case)(functools.partial(ending, case >= 2, case % 2 == 1))

        finalize(qr)
        return carry

    lax.fori_loop(1, nq, row, 0)


def _attn_call(q, k, vt, lam, subln, *, mode):
    b, _, s, _ = q.shape
    n_groups = vt.shape[1]
    kh = k.shape[1] // n_groups
    tq, tk = _att_tiles(s)
    assert tk == tq
    kernel = functools.partial(_attn_kernel, mode=mode, tq=tq)
    return pl.pallas_call(
        kernel,
        grid=(b, n_groups),
        in_specs=[
            pl.BlockSpec((1, 2, s, LANES), lambda bi, g: (bi, g, 0, 0)),
            pl.BlockSpec((1, kh, s, LANES), lambda bi, g: (bi, g, 0, 0)),
            pl.BlockSpec((1, 1, s // tk, VT_ROWS, tk), lambda bi, g: (bi, g, 0, 0, 0)),
            _const_spec(lam.shape),
            _const_spec(subln.shape),
        ],
        out_specs=pl.BlockSpec((1, s, LANES), lambda bi, g: (bi, 0, g)),
        out_shape=jax.ShapeDtypeStruct((b, s, n_groups * LANES), jnp.bfloat16),
        scratch_shapes=[
            pltpu.VMEM((3, 2, tq, tq), jnp.float32),
            pltpu.VMEM((3, 2, 1, tq), jnp.float32),
            pltpu.VMEM((2, 1, tq), jnp.float32),
            pltpu.VMEM((2, VT_ROWS, tq), jnp.float32),
        ],
        compiler_params=pltpu.CompilerParams(
            dimension_semantics=("arbitrary", "arbitrary"), vmem_limit_bytes=V7X_VMEM_LIMIT),
        name="attn_" + mode,
    )(q, k, vt, lam, subln)


def _to_token_tiles(ref, mat):
    n = mat.shape[0]
    for j in range(TOK_ROWS):
        ref[pl.ds(j, n, stride=TOK_ROWS), :] = mat[:, j * LANES:(j + 1) * LANES]


def _from_token_tiles(ref, n):
    return jnp.concatenate([ref[pl.ds(j, n, stride=TOK_ROWS), :] for j in range(TOK_ROWS)], axis=1)


def _tok_at(ref, row):
    return ref.at[pl.ds(pl.multiple_of(row, TOK_ROWS), TOK_ROWS)]


def _tok(ref, t):
    return _tok_at(ref, t * TOK_ROWS)


def _copy_token_run(src, dst, src_tok, dst_tok, count, sem, top_bit):
    bit = top_bit
    while bit >= 1:
        done = count & ~(2 * bit - 1)

        @pl.when((count & bit) != 0)
        def _(bit=bit, done=done):
            s = pl.multiple_of((src_tok + done) * TOK_ROWS, TOK_ROWS)
            d = pl.multiple_of((dst_tok + done) * TOK_ROWS, TOK_ROWS)
            pltpu.make_async_copy(src.at[pl.ds(s, bit * TOK_ROWS)], dst.at[pl.ds(d, bit * TOK_ROWS)], sem).start()

        bit //= 2


def _route_kernel(x_ref, oa_ref, ob_ref, sg_ref, wbm_ref, wbd_ref, wout_ref, gffn_ref, wrh_ref, wrl_ref, br_ref,
                  h_ref, hn_ref, idx_ref, w_ref, rank_ref, tcnt_ref, tbase_ref, cnt_ref, carry_scr):
    @pl.when(pl.program_id(0) == 0)
    def _():
        carry_scr[...] = jnp.zeros(carry_scr.shape, jnp.float32)

    sg = sg_ref[...]
    ma = _f32dot(oa_ref[...], wbm_ref[...])
    mb = _f32dot(ob_ref[...], wbd_ref[...])
    merged = sg[:, :D_MODEL] * ma + sg[:, D_MODEL:] * mb
    h = x_ref[...] + _f32dot(merged.astype(jnp.bfloat16), wout_ref[...])
    h_ref[...] = h

    hn = h * _rms_scale(h, D_MODEL) * gffn_ref[...]
    _to_token_tiles(hn_ref, hn)
    hn_hi = hn.astype(jnp.bfloat16)
    hn_lo = (hn - hn_hi.astype(jnp.float32)).astype(jnp.bfloat16)
    wrh = wrh_ref[...]
    logits = _f32dot(hn_hi, wrh) + _f32dot(hn_lo, wrh) + _f32dot(hn_hi, wrl_ref[...]) + br_ref[...]

    tm = logits.shape[0]
    lane = _lane_iota(logits.shape)
    work = logits
    vals, picks, onehots = [], [], []
    for _k in range(TOP_K):
        mx = jnp.max(work, axis=-1, keepdims=True)
        sel = jnp.min(jnp.where(work == mx, lane, LANES), axis=-1, keepdims=True)
        oh = lane == sel
        vals.append(mx)
        picks.append(sel)
        onehots.append(oh)
        work = jnp.where(oh, NEG_BIG * 2, work)
    exps = [jnp.exp(vv - vals[0]) for vv in vals]
    denom = exps[0] + exps[1] + exps[2] + exps[3]

    chosen = jnp.zeros(logits.shape, jnp.float32)
    for oh in onehots:
        chosen = chosen + oh.astype(jnp.float32)
    r_i = lax.broadcasted_iota(jnp.int32, (tm, tm), 0)
    c_i = lax.broadcasted_iota(jnp.int32, (tm, tm), 1)
    ltri = (c_i < r_i).astype(jnp.bfloat16)
    before = _f32dot(ltri, chosen.astype(jnp.bfloat16))
    tile_counts = jnp.sum(chosen, axis=0, keepdims=True)
    tbase_ref[0] = carry_scr[...]
    tcnt_ref[0] = tile_counts
    carry_scr[...] = carry_scr[...] + tile_counts
    cnt_ref[...] = carry_scr[...]

    idx_out = jnp.zeros(logits.shape, jnp.float32)
    w_out = jnp.zeros(logits.shape, jnp.float32)
    rank_out = jnp.zeros(logits.shape, jnp.float32)
    for kk in range(TOP_K):
        oh = onehots[kk]
        r_k = jnp.sum(jnp.where(oh, before, 0.0), axis=-1, keepdims=True)
        idx_out = jnp.where(lane == kk, picks[kk].astype(jnp.float32), idx_out)
        rank_out = jnp.where(lane == kk, r_k, rank_out)
        w_out = jnp.where(lane == kk, exps[kk] / denom, w_out)
    idx_ref[0] = idx_out.T[:SUBLANES]
    w_ref[0] = w_out.T[:SUBLANES]
    rank_ref[0] = rank_out.T[:SUBLANES]


def _route_call(x2, oa, ob, sg, wbm, wbd, wout, gffn, wrh, wrl, br):
    n = x2.shape[0]
    tm = min(ROUTE_TM, n)
    nt = n // tm
    row = lambda w: pl.BlockSpec((tm, w), lambda i: (i, 0))
    per_tile = pl.BlockSpec((1, 1, LANES), lambda i: (i, 0, 0))
    per_choice = pl.BlockSpec((1, SUBLANES, tm), lambda i: (i, 0, 0))
    out_shape = (
        jax.ShapeDtypeStruct((n, D_MODEL), jnp.float32),
        jax.ShapeDtypeStruct((n * TOK_ROWS, LANES), jnp.float32),
        jax.ShapeDtypeStruct((nt, SUBLANES, tm), jnp.float32),
        jax.ShapeDtypeStruct((nt, SUBLANES, tm), jnp.float32),
        jax.ShapeDtypeStruct((nt, SUBLANES, tm), jnp.float32),
        jax.ShapeDtypeStruct((nt, 1, LANES), jnp.float32),
        jax.ShapeDtypeStruct((nt, 1, LANES), jnp.float32),
        jax.ShapeDtypeStruct((1, LANES), jnp.float32),
    )
    return pl.pallas_call(
        _route_kernel,
        grid=(nt,),
        in_specs=[row(D_MODEL), row(MLA_WIDTH), row(DIFF_WIDTH), row(2 * D_MODEL),
                  _const_spec(wbm.shape), _const_spec(wbd.shape), _const_spec(wout.shape), _const_spec(gffn.shape),
                  _const_spec(wrh.shape), _const_spec(wrl.shape), _const_spec(br.shape)],
        out_specs=(row(D_MODEL), pl.BlockSpec((tm * TOK_ROWS, LANES), lambda i: (i, 0)), per_choice, per_choice,
                   per_choice, per_tile, per_tile, _const_spec((1, LANES))),
        out_shape=out_shape,
        scratch_shapes=[pltpu.VMEM((1, LANES), jnp.float32)],
        compiler_params=pltpu.CompilerParams(dimension_semantics=("arbitrary",), vmem_limit_bytes=V7X_VMEM_LIMIT),
        name="route",
    )(x2, oa, ob, sg, wbm, wbd, wout, gffn, wrh, wrl, br)


def _dispatch_kernel(seg_ref, lp_ref, pad_ref, hn_ref, xs_ref, scr, zscr, sems, zsem):
    step = pl.program_id(0)
    last = pl.num_programs(0) - 1
    tm = hn_ref.shape[0] // TOK_ROWS
    par = step % 2
    buf = scr.at[par]

    def place(t, carry):
        v = _tok(hn_ref, t)[...]
        for kk in range(TOP_K):
            _tok_at(buf, lp_ref[0, 0, t * TOP_K + kk])[...] = v
        return carry

    lax.fori_loop(0, tm, place, 0, unroll=TOK_UNROLL)

    def run(e, carry):
        _copy_token_run(buf, xs_ref, seg_ref[0, 0, N_EXPERTS + e], seg_ref[0, 0, 2 * N_EXPERTS + e],
                        seg_ref[0, 0, e], sems.at[par], tm)
        return carry

    lax.fori_loop(0, N_EXPERTS, run, 0)

    def drain(p):
        pltpu.make_async_copy(scr.at[p], xs_ref.at[pl.ds(0, scr.shape[1])], sems.at[p]).wait()

    @pl.when(step == 0)
    def _():
        zscr[...] = jnp.zeros(zscr.shape, zscr.dtype)

        def pad_run(e, carry):
            _copy_token_run(zscr, xs_ref, 0, pad_ref[e], pad_ref[N_EXPERTS + e], zsem, FFN_TM // 2)
            return carry

        lax.fori_loop(0, N_EXPERTS, pad_run, 0)

        def pad_tile(j, carry):
            d = pl.multiple_of((pad_ref[2 * N_EXPERTS] + j * FFN_TM) * TOK_ROWS, TOK_ROWS)
            pltpu.make_async_copy(zscr, xs_ref.at[pl.ds(d, FFN_TM * TOK_ROWS)], zsem).start()
            return carry

        lax.fori_loop(0, pad_ref[2 * N_EXPERTS + 1], pad_tile, 0)

    @pl.when(step > 0)
    def _():
        drain(1 - par)

    @pl.when(step == last)
    def _():
        drain(par)
        for _e in range(N_EXPERTS):
            pltpu.make_async_copy(zscr, xs_ref.at[pl.ds(0, FFN_TM * TOK_ROWS)], zsem).wait()


def _dispatch_call(seg3, lp3, pad, hn, n_rows):
    nt = seg3.shape[0]
    tm = hn.shape[0] // TOK_ROWS // nt
    smem = lambda a: pl.BlockSpec((1, 1, a.shape[2]), lambda i: (i, 0, 0), memory_space=pltpu.SMEM)
    return pl.pallas_call(
        _dispatch_kernel,
        grid=(nt,),
        in_specs=[
            smem(seg3), smem(lp3),
            pl.BlockSpec(memory_space=pltpu.SMEM),
            pl.BlockSpec((tm * TOK_ROWS, LANES), lambda i: (i, 0)),
        ],
        out_specs=pl.BlockSpec(memory_space=pl.ANY),
        out_shape=jax.ShapeDtypeStruct((n_rows * TOK_ROWS, LANES), jnp.float32),
        scratch_shapes=[pltpu.VMEM((2, TOP_K * tm * TOK_ROWS, LANES), jnp.float32),
                        pltpu.VMEM((FFN_TM * TOK_ROWS, LANES), jnp.float32),
                        pltpu.SemaphoreType.DMA((2,)), pltpu.SemaphoreType.DMA(())],
        compiler_params=pltpu.CompilerParams(dimension_semantics=("arbitrary",), vmem_limit_bytes=V7X_VMEM_LIMIT),
        name="dispatch",
    )(seg3, lp3, pad, hn)


def _ffn_kernel(te_ref, tv_ref, nt_ref, xs_ref, w1_ref, b1_ref, w2_ref, b2_ref, y_ref, w1b, w2b):
    i = pl.program_id(0)
    tm = xs_ref.shape[0] // TOK_ROWS

    @pl.when((i == 0) | (te_ref[i] != te_ref[jnp.maximum(i - 1, 0)]))
    def _():
        w1b[...] = w1_ref[0].astype(jnp.bfloat16)
        w2b[...] = w2_ref[0].astype(jnp.bfloat16)

    @pl.when(i < nt_ref[0])
    def _():
        rows = lax.broadcasted_iota(jnp.int32, (tm, 1), 0)
        xt = jnp.where(rows < tv_ref[i], _from_token_tiles(xs_ref, tm), 0.0).astype(jnp.bfloat16)
        u = _f32dot(xt, w1b[...]) + b1_ref[0]
        glu = jnp.minimum(u[:, :D_EXPERT], SWIGLU_LIMIT)
        lin = jnp.clip(u[:, D_EXPERT:], -SWIGLU_LIMIT, SWIGLU_LIMIT)
        a = glu * jax.nn.sigmoid(SWIGLU_ALPHA * glu) * (lin + 1.0)
        y = _f32dot(a.astype(jnp.bfloat16), w2b[...]) + b2_ref[0]
        _to_token_tiles(y_ref, y)

    @pl.when(i >= nt_ref[0])
    def _():
        y_ref[...] = jnp.zeros(y_ref.shape, y_ref.dtype)


def _ffn_call(tile_expert, tile_valid, n_tiles, xs, w1, b1, w2, b2):
    n_rows = xs.shape[0] // TOK_ROWS
    tm = FFN_TM
    t_tiles = n_rows // tm

    def tile_map(i, te, tv, nt):
        return (jnp.minimum(i, nt[0] - 1), 0)

    def out_map(i, te, tv, nt):
        return (i, 0)

    def w_map(i, te, tv, nt):
        return (te[i], 0, 0)

    grid_spec = pltpu.PrefetchScalarGridSpec(
        num_scalar_prefetch=3,
        grid=(t_tiles,),
        in_specs=[
            pl.BlockSpec((tm * TOK_ROWS, LANES), tile_map),
            pl.BlockSpec((1, D_MODEL, 2 * D_EXPERT), w_map),
            pl.BlockSpec((1, 1, 2 * D_EXPERT), w_map),
            pl.BlockSpec((1, D_EXPERT, D_MODEL), w_map),
            pl.BlockSpec((1, 1, D_MODEL), w_map),
        ],
        out_specs=pl.BlockSpec((tm * TOK_ROWS, LANES), out_map),
        scratch_shapes=[pltpu.VMEM((D_MODEL, 2 * D_EXPERT), jnp.bfloat16), pltpu.VMEM((D_EXPERT, D_MODEL), jnp.bfloat16)],
    )
    return pl.pallas_call(
        _ffn_kernel,
        grid_spec=grid_spec,
        out_shape=jax.ShapeDtypeStruct(xs.shape, jnp.float32),
        compiler_params=pltpu.CompilerParams(dimension_semantics=("arbitrary",), vmem_limit_bytes=V7X_VMEM_LIMIT),
        name="ffn",
    )(tile_expert, tile_valid, n_tiles, xs, w1, b1, w2, b2)


def _combine_kernel(seg_ref, snext_ref, lp_ref, w_ref, h_ref, y_ref, o_ref, ybuf, tok_scr, sems):
    step = pl.program_id(0)
    last = pl.num_programs(0) - 1
    tm = h_ref.shape[0]
    par = step % 2

    def fetch(s_ref, slot):
        def run(e, carry):
            _copy_token_run(y_ref, ybuf.at[slot], s_ref[0, 0, 2 * N_EXPERTS + e], s_ref[0, 0, N_EXPERTS + e],
                            s_ref[0, 0, e], sems.at[slot], tm)
            return carry

        lax.fori_loop(0, N_EXPERTS, run, 0)

    @pl.when(step == 0)
    def _():
        fetch(seg_ref, 0)

    @pl.when(step < last)
    def _():
        fetch(snext_ref, 1 - par)

    pltpu.make_async_copy(y_ref.at[pl.ds(0, ybuf.shape[1])], ybuf.at[par], sems.at[par]).wait()
    buf = ybuf.at[par]

    def token(t, carry):
        acc = None
        for kk in range(TOP_K):
            term = w_ref[0, 0, t * TOP_K + kk] * _tok_at(buf, lp_ref[0, 0, t * TOP_K + kk])[...]
            acc = term if acc is None else acc + term
        _tok(tok_scr, t)[...] = acc
        return carry

    lax.fori_loop(0, tm, token, 0, unroll=TOK_UNROLL)
    o_ref[...] = h_ref[...] + _from_token_tiles(tok_scr, tm)


def _combine_call(seg3, lp3, w3, h, y):
    n = h.shape[0]
    nt = seg3.shape[0]
    tm = n // nt
    smem = lambda a, f: pl.BlockSpec((1, 1, a.shape[2]), f, memory_space=pltpu.SMEM)
    here = lambda i: (i, 0, 0)
    return pl.pallas_call(
        _combine_kernel,
        grid=(nt,),
        in_specs=[
            smem(seg3, here), smem(seg3, lambda i: (jnp.minimum(i + 1, nt - 1), 0, 0)), smem(lp3, here), smem(w3, here),
            pl.BlockSpec((tm, D_MODEL), lambda i: (i, 0)),
            pl.BlockSpec(memory_space=pl.ANY),
        ],
        out_specs=pl.BlockSpec((tm, D_MODEL), lambda i: (i, 0)),
        out_shape=jax.ShapeDtypeStruct((n, D_MODEL), jnp.float32),
        scratch_shapes=[pltpu.VMEM((2, TOP_K * tm * TOK_ROWS, LANES), jnp.float32),
                        pltpu.VMEM((tm * TOK_ROWS, LANES), jnp.float32), pltpu.SemaphoreType.DMA((2,))],
        compiler_params=pltpu.CompilerParams(dimension_semantics=("arbitrary",), vmem_limit_bytes=V7X_VMEM_LIMIT),
        name="combine",
    )(seg3, seg3, lp3, w3, h, y)


def _rot_cols(w, group):
    lead = w.shape[:-1]
    g = w.reshape(lead + (-1, group))
    half = group // 2
    return jnp.concatenate([-g[..., half:], g[..., :half]], axis=-1).reshape(w.shape)


def _rope_tables(s, dim):
    inv_freq = 1.0 / (ROPE_THETA ** (jnp.arange(0, dim, 2, dtype=jnp.float32) / dim))
    ang = jnp.arange(s, dtype=jnp.float32)[:, None] * inv_freq[None, :]
    return jnp.cos(ang), jnp.sin(ang)


def _mla_tables(s, gain, scale):
    cos, sin = _rope_tables(s, MLA_ROPE)
    cos2 = jnp.concatenate([cos, cos], axis=-1)
    sin2 = jnp.concatenate([sin, sin], axis=-1)
    g_nope, g_rope = gain[:MLA_NOPE], gain[MLA_NOPE:]
    g_perm = jnp.concatenate([g_rope[MLA_ROPE // 2:], g_rope[:MLA_ROPE // 2]])
    pad = jnp.zeros((s, LANES - MLA_QK), jnp.float32)
    c = jnp.concatenate([jnp.broadcast_to(g_nope, (s, MLA_NOPE)), cos2 * g_rope, pad], axis=-1) * scale
    sn = jnp.concatenate([jnp.zeros((s, MLA_NOPE), jnp.float32), sin2 * g_perm, pad], axis=-1) * scale
    return c, sn


def _diff_tables(s, gain, scale):
    cos, sin = _rope_tables(s, DIFF_D)
    cos2 = jnp.concatenate([cos, cos], axis=-1)
    sin2 = jnp.concatenate([sin, sin], axis=-1)
    g_perm = jnp.concatenate([gain[DIFF_D // 2:], gain[:DIFF_D // 2]])
    c = cos2 * gain * scale
    sn = sin2 * g_perm * scale
    return jnp.concatenate([c, c], axis=-1), jnp.concatenate([sn, sn], axis=-1)


def _pad_heads(w, width):
    kdim = w.shape[0]
    g = w.reshape(kdim, -1, width)
    g = jnp.pad(g, ((0, 0), (0, 0), (0, LANES - width)))
    return g.reshape(kdim, -1)


def kernel(x, attn_norm, w_in, mla_q_lat_norm, mla_kv_lat_norm, mla_w_uq, mla_w_ukv, mla_q_norm, mla_k_norm,
           diff_q_norm, diff_k_norm, diff_lambda_q1, diff_lambda_k1, diff_lambda_q2, diff_lambda_k2, diff_subln,
           w_branch_mla, w_branch_diff, w_out, ffn_norm, w_router, b_router, w_mlp1, b_mlp1, w_mlp2, b_mlp2):
    b, s, d = x.shape
    n = b * s
    bf = jnp.bfloat16
    f32 = jnp.float32
    i = 0

    wi = w_in[i]
    c0 = MLA_Q_LORA + MLA_KV_LORA
    w_kpe = wi[:, c0:c0 + MLA_ROPE]
    c1 = c0 + MLA_ROPE
    w_dq, w_dk, w_dv = (wi[:, c1 + j * DIFF_WIDTH:c1 + (j + 1) * DIFF_WIDTH] for j in range(3))
    w_gate = wi[:, c1 + 3 * DIFF_WIDTH:]
    place = lambda w: jnp.pad(w, ((0, 0), (MLA_NOPE, LANES - MLA_QK)))
    w1 = jnp.concatenate([
        wi[:, :c0], place(w_kpe), place(_rot_cols(w_kpe, MLA_ROPE)),
        w_dq, _rot_cols(w_dq, DIFF_D), w_dk, _rot_cols(w_dk, DIFF_D), w_dv, w_gate], axis=1).astype(bf)

    wuq = mla_w_uq[i].reshape(MLA_Q_LORA, MLA_HEADS, MLA_QK)
    wuq_rot = jnp.concatenate(
        [jnp.zeros((MLA_Q_LORA, MLA_HEADS, MLA_NOPE), f32), _rot_cols(wuq[..., MLA_NOPE:], MLA_ROPE)], axis=-1)
    wq = jnp.concatenate([_pad_heads(wuq.reshape(MLA_Q_LORA, -1), MLA_QK),
                          _pad_heads(wuq_rot.reshape(MLA_Q_LORA, -1), MLA_QK)], axis=1).astype(bf)
    wukv = mla_w_ukv[i].reshape(MLA_KV_LORA, MLA_HEADS, MLA_NOPE + MLA_V)
    wkv = jnp.concatenate([_pad_heads(wukv[..., :MLA_NOPE].reshape(MLA_KV_LORA, -1), MLA_NOPE),
                           wukv[..., MLA_NOPE:].reshape(MLA_KV_LORA, -1)], axis=1).astype(bf)

    cq, sq = _mla_tables(s, mla_q_norm[i], LOG2E / math.sqrt(MLA_QK))
    ck, sk = _mla_tables(s, mla_k_norm[i], 1.0)
    cqd, sqd = _diff_tables(s, diff_q_norm[i], LOG2E / math.sqrt(DIFF_D))
    ckd, skd = _diff_tables(s, diff_k_norm[i], 1.0)
    tabs = jnp.stack([cq, sq, ck, sk, cqd, sqd, ckd, skd])

    glat = jnp.concatenate([mla_q_lat_norm[i], mla_kv_lat_norm[i]])[None, :]
    qa, ka, va, qd, kd, vd, sg = _prep_call(x, attn_norm[i][None, :], w1, glat, wq, wkv, tabs)

    lam = jnp.pad(jnp.stack([diff_lambda_q1[i], diff_lambda_k1[i], diff_lambda_q2[i], diff_lambda_k2[i]]),
                  ((0, 0), (0, LANES - DIFF_D)))
    subln = diff_subln[i][None, :]
    o_a = _attn_call(qa, ka, va, lam, subln, mode="mla")
    o_b = _attn_call(qd, kd, vd, lam, subln, mode="diff")

    wr = jnp.pad(w_router[i], ((0, 0), (0, LANES - N_EXPERTS)))
    wr_hi = wr.astype(bf)
    wr_lo = (wr - wr_hi.astype(f32)).astype(bf)
    br = jnp.concatenate([b_router[i], jnp.full((LANES - N_EXPERTS,), NEG_BIG, f32)])[None, :]
    h, hn, idx, gw, rank, tcnt, tbase, cnt = _route_call(
        x.reshape(n, d), o_a.reshape(n, MLA_WIDTH), o_b.reshape(n, DIFF_WIDTH), sg.reshape(n, 2 * d),
        w_branch_mla[i].astype(bf), w_branch_diff[i].astype(bf), w_out[i].astype(bf), ffn_norm[i][None, :],
        wr_hi, wr_lo, br)

    i32 = jnp.int32
    counts = cnt[0, :N_EXPERTS].astype(i32)
    tiles_per = (counts + FFN_TM - 1) // FFN_TM
    tile_end = jnp.cumsum(tiles_per)
    tile_start = tile_end - tiles_per
    n_tiles = tile_end[-1:]
    t_tiles = n * TOP_K // FFN_TM + N_EXPERTS
    tids = jnp.arange(t_tiles, dtype=i32)
    tile_expert = jnp.minimum(jnp.sum(tids[:, None] >= tile_end[None, :], axis=1), N_EXPERTS - 1).astype(i32)
    tile_valid = jnp.clip(counts[tile_expert] - (tids - tile_start[tile_expert]) * FFN_TM, 0, FFN_TM).astype(i32)

    nt = tcnt.shape[0]
    tc = tcnt[:, 0, :N_EXPERTS].astype(i32)
    tb = tbase[:, 0, :N_EXPERTS].astype(i32)
    off = jnp.cumsum(tc, axis=1) - tc
    first = (tile_start * FFN_TM)[None, :] + tb
    seg3 = jnp.concatenate([tc, off, first], axis=1).reshape(nt, 1, 3 * N_EXPERTS)
    per = n // nt * TOP_K
    by_token = lambda a: a[:, :TOP_K, :].transpose(0, 2, 1).reshape(nt, per)
    picked = by_token(idx).astype(i32)[:, :, None] == jnp.arange(N_EXPERTS, dtype=i32)
    lp3 = ((jnp.sum(jnp.where(picked, off[:, None, :], 0), axis=-1)
            + by_token(rank).astype(i32)) * TOK_ROWS).reshape(nt, 1, per)
    w3 = by_token(gw).reshape(nt, 1, per)
    pad = jnp.concatenate([tile_start * FFN_TM + counts, tiles_per * FFN_TM - counts,
                           n_tiles * FFN_TM, t_tiles - n_tiles]).astype(i32)

    xs = _dispatch_call(seg3, lp3, pad, hn, t_tiles * FFN_TM)
    y = _ffn_call(tile_expert, tile_valid, n_tiles.astype(i32), xs,
                  w_mlp1[i], b_mlp1[i][:, None, :], w_mlp2[i], b_mlp2[i][:, None, :])
    out = _combine_call(seg3, lp3, w3, h, y)
    return out.reshape(b, s, d)
```
